```python
import functools
import jax, jax.numpy as jnp
from jax import lax
import numpy as np

D_MODEL = 1024
BATCH = 4
SEQ = 4096
DEPTH = 4
DEC_BATCH = 32
DEC_SEQ = 4
PAST_LEN = 8192
PAGE_SIZE = 128

N_HEADS = 8
HEAD_DIM = 64
A_WIDTH = N_HEADS * HEAD_DIM
IDX_HEADS = 8
IDX_DIM = 64
IDX_ROPE = 32
TOPK_MAX = 256
Q_BLOCK = 128
SGU_GROUPS = 4
SGU_WIDTH = 256
SGU_CHUNK = 128
POOL_GROUPS = 4
POOL_WIDTH = 256
POOL_GD = POOL_WIDTH // POOL_GROUPS
POOL_WINDOWS = (2, 4, 8, 16)
POOL_STATE = 15
D_FF = 2816
CONV_W = 3
N_BRANCH = 3
ROPE_THETA = 10000.0
EPS = 1e-6
IN_COLS = 3 * A_WIDTH + IDX_HEADS * IDX_DIM + IDX_DIM + IDX_HEADS + 2 * SGU_WIDTH + POOL_WIDTH + N_BRANCH * D_MODEL

kernel_name = "dsa_gmlp_pool_gated_hybrid_step"


def _split_points():
    sizes = (A_WIDTH, A_WIDTH, A_WIDTH, IDX_HEADS * IDX_DIM, IDX_DIM, IDX_HEADS,
             2 * SGU_WIDTH, POOL_WIDTH, N_BRANCH * D_MODEL)
    pts, acc = [], 0
    for s in sizes[:-1]:
        acc += s
        pts.append(acc)
    return pts


def rmsnorm(x, g):
    xf = x.astype(jnp.float32)
    y = xf * lax.rsqrt(jnp.mean(xf * xf, axis=-1, keepdims=True) + EPS)
    return (y * g.astype(jnp.float32)).astype(x.dtype)


def rope(x, pos):
    half = x.shape[-1] // 2
    inv = ROPE_THETA ** (-jnp.arange(half, dtype=jnp.float32) / half)
    ang = pos.astype(jnp.float32)[:, None] * inv[None, :]
    cos, sin = jnp.cos(ang)[:, None, :], jnp.sin(ang)[:, None, :]
    xf = x.astype(jnp.float32)
    x1, x2 = xf[..., :half], xf[..., half:]
    return jnp.concatenate([x1 * cos - x2 * sin, x2 * cos + x1 * sin], axis=-1).astype(x.dtype)


def rope_partial(x, pos):
    return jnp.concatenate([rope(x[..., :IDX_ROPE], pos), x[..., IDX_ROPE:]], axis=-1)


def indexer_scores(iq, iw, ik):
    s = jnp.einsum('bqhd,bsd->bqhs', iq, ik).astype(jnp.float32) * (IDX_DIM ** -0.5)
    return jnp.einsum('bqhs,bqh->bqs', jax.nn.relu(s), iw.astype(jnp.float32))


def gathered_attention(q, k_sel, v_sel, valid):
    logits = jnp.einsum('bqhd,bqkhd->bqhk', q, k_sel).astype(jnp.float32) * (HEAD_DIM ** -0.5)
    logits = jnp.where(valid[:, :, None, :], logits, -jnp.inf)
    p = jax.nn.softmax(logits, axis=-1).astype(v_sel.dtype)
    return jnp.einsum('bqhk,bqkhd->bqhd', p, v_sel)


def dsa_prompt(q, k, v, iq, iw, ik):
    B, S = q.shape[0], q.shape[1]
    topk = min(TOPK_MAX, S // 4)
    key_pos = jnp.arange(S)
    bidx = jnp.arange(B)[:, None, None]

    def block(i):
        start = i * Q_BLOCK
        qb = lax.dynamic_slice_in_dim(q, start, Q_BLOCK, axis=1)
        iqb = lax.dynamic_slice_in_dim(iq, start, Q_BLOCK, axis=1)
        iwb = lax.dynamic_slice_in_dim(iw, start, Q_BLOCK, axis=1)
        qpos = start + jnp.arange(Q_BLOCK)
        sc = indexer_scores(iqb, iwb, ik)
        sc = jnp.where((key_pos[None, :] <= qpos[:, None])[None], sc, -jnp.inf)
        _, sel = lax.top_k(sc, topk)
        valid = sel <= qpos[None, :, None]
        return gathered_attention(qb, k[bidx, sel], v[bidx, sel], valid)

    outs = lax.map(block, jnp.arange(S // Q_BLOCK))
    return jnp.transpose(outs, (1, 0, 2, 3, 4)).reshape(B, S, A_WIDTH)


def dsa_sample(q, k, v, iq, iw, ik, cache_k, cache_v, cache_idx_k, page_table, layer):
    DB, T = q.shape[0], q.shape[1]
    past = page_table.shape[1] * PAGE_SIZE
    L = past + T
    topk = min(TOPK_MAX, L // 4)
    bidx = jnp.arange(DB)[:, None, None]
    ik_past = cache_idx_k[layer, page_table].reshape(DB, past, IDX_DIM)
    ik_all = jnp.concatenate([ik_past, ik], axis=1)
    qpos = past + jnp.arange(T)
    sc = indexer_scores(iq, iw, ik_all)
    sc = jnp.where((jnp.arange(L)[None, :] <= qpos[:, None])[None], sc, -jnp.inf)
    _, sel = lax.top_k(sc, topk)
    valid = sel <= qpos[None, :, None]
    in_past = (sel < past)[..., None, None]
    ps = jnp.minimum(sel, past - 1)
    phys = page_table[bidx, ps // PAGE_SIZE]
    off = ps % PAGE_SIZE
    ns = jnp.clip(sel - past, 0, T - 1)
    k_sel = jnp.where(in_past, cache_k[layer, phys, off], k[bidx, ns])
    v_sel = jnp.where(in_past, cache_v[layer, phys, off], v[bidx, ns])
    return gathered_attention(q, k_sel, v_sel, valid).reshape(DB, T, A_WIDTH)


def sgu(sg, w_s, b_s, g):
    B, T = sg.shape[0], sg.shape[1]
    u, vv = jnp.split(jax.nn.gelu(sg), 2, axis=-1)
    vv = rmsnorm(vv, g)
    pad = (-T) % SGU_CHUNK
    nc = (T + pad) // SGU_CHUNK
    vp = jnp.pad(vv, ((0, 0), (0, pad), (0, 0))).reshape(B, nc, SGU_CHUNK, SGU_GROUPS, SGU_WIDTH // SGU_GROUPS)
    tril = jnp.tril(jnp.ones((SGU_CHUNK, SGU_CHUNK), dtype=bool))
    ws = jnp.where(tril[None], w_s, jnp.zeros_like(w_s))
    mix = jnp.einsum('gts,bcsgd->bctgd', ws, vp) + jnp.transpose(b_s)[None, None, :, :, None]
    mix = mix.reshape(B, nc * SGU_CHUNK, SGU_WIDTH)[:, :T]
    return u * mix, vv


def pool_mixer(xc, prefix, pos0, w_pool, scale):
    B, T, C = xc.shape
    ext = jnp.concatenate([prefix, xc], axis=1)
    cs = jnp.pad(jnp.cumsum(ext.astype(jnp.float32), axis=1), ((0, 0), (1, 0), (0, 0)))
    hi = cs[:, POOL_STATE + 1:]
    pos = pos0 + jnp.arange(T)
    outs = []
    for g, w in enumerate(POOL_WINDOWS):
        c0, c1 = g * POOL_GD, (g + 1) * POOL_GD
        lo = cs[:, POOL_STATE + 1 - w: POOL_STATE + 1 - w + T, c0:c1]
        cnt = jnp.minimum(pos + 1, w).astype(jnp.float32)[None, :, None]
        outs.append((hi[..., c0:c1] - lo) / cnt)
    pooled = jnp.concatenate(outs, axis=-1).astype(xc.dtype) - xc
    mixed = jnp.einsum('btgc,gcd->btgd', pooled.reshape(B, T, POOL_GROUPS, POOL_GD), w_pool).reshape(B, T, C)
    return mixed * scale, ext[:, -POOL_STATE:]


def conv_ffn(xn, prefix, w_ff_in, cw, cb, w_down):
    a, up = jnp.split(xn @ w_ff_in, 2, axis=-1)
    T = a.shape[1]
    ext = jnp.concatenate([prefix, a], axis=1)
    conv = ext[:, 0:T] * cw[0] + ext[:, 1:T + 1] * cw[1] + ext[:, 2:T + 2] * cw[2] + cb
    return (jax.nn.silu(conv) * up) @ w_down, ext[:, -(CONV_W - 1):]


def trunk_layer(x, pos, attend, pool_prefix, ff_prefix, w):
    (norm1_g, w_in, q_norm_g, k_norm_g, sgu_w, sgu_b, sgu_norm_g, pool_w, pool_scale,
     w_br_a, w_br_b, w_br_c, w_out, norm2_g, w_ff_in, ff_conv_w, ff_conv_b, w_ff_down) = w
    B, T = x.shape[0], x.shape[1]
    xn = rmsnorm(x, norm1_g)
    q, k, v, iq, ik, iw, sg, xc, gt = jnp.split(xn @ w_in, _split_points(), axis=-1)
    q = rope(rmsnorm(q.reshape(B, T, N_HEADS, HEAD_DIM), q_norm_g), pos)
    k = rope(rmsnorm(k.reshape(B, T, N_HEADS, HEAD_DIM), k_norm_g), pos)
    v = v.reshape(B, T, N_HEADS, HEAD_DIM)
    iq = rope_partial(iq.reshape(B, T, IDX_HEADS, IDX_DIM), pos)
    ik = rope_partial(ik[:, :, None, :], pos)[:, :, 0, :]
    iw = iw * (IDX_HEADS ** -0.5)
    att = attend(q, k, v, iq, iw, ik)
    sgo, sgu_v = sgu(sg, sgu_w, sgu_b, sgu_norm_g)
    po, pool_state = pool_mixer(xc, pool_prefix, pos[0], pool_w, pool_scale)
    g_a, g_b, g_c = jnp.split(jax.nn.sigmoid(gt), N_BRANCH, axis=-1)
    merged = g_a * (att @ w_br_a) + g_b * (sgo @ w_br_b) + g_c * (po @ w_br_c)
    x = x + merged @ w_out
    ffo, ff_state = conv_ffn(rmsnorm(x, norm2_g), ff_prefix, w_ff_in, ff_conv_w, ff_conv_b, w_ff_down)
    x = x + ffo
    return x, (k, v, ik, sgu_v, pool_state, ff_state)


def setup_inputs(seed: int = 0) -> dict:
    key = jax.random.key(seed)
    ks = jax.random.split(key, 32)
    f32 = jnp.float32

    def nrm(k, shape, scale):
        return scale * jax.random.normal(k, shape, f32)

    n_pages = PAST_LEN // PAGE_SIZE
    n_pool = (5 * DEC_BATCH * n_pages) // 4
    page_table = jax.random.permutation(ks[7], n_pool)[:DEC_BATCH * n_pages].reshape(DEC_BATCH, n_pages).astype(jnp.int32)
    return {
        'x_prompt': nrm(ks[0], (BATCH, SEQ, D_MODEL), 1.0),
        'x_sample': nrm(ks[1], (DEC_BATCH, DEC_SEQ, D_MODEL), 1.0),
        'cache_k': nrm(ks[2], (DEPTH, n_pool, PAGE_SIZE, N_HEADS, HEAD_DIM), 1.0),
        'cache_v': nrm(ks[3], (DEPTH, n_pool, PAGE_SIZE, N_HEADS, HEAD_DIM), 1.0),
        'cache_idx_k': nrm(ks[4], (DEPTH, n_pool, PAGE_SIZE, IDX_DIM), 1.0),
        'state_pool': nrm(ks[5], (DEPTH, DEC_BATCH, POOL_STATE, POOL_WIDTH), 1.0),
        'state_ffn_conv': nrm(ks[6], (DEPTH, DEC_BATCH, CONV_W - 1, D_FF), 1.0),
        'page_table': page_table,
        'norm1_g': 1.0 + nrm(ks[8], (DEPTH, D_MODEL), 0.1),
        'w_in': nrm(ks[9], (DEPTH, D_MODEL, IN_COLS), D_MODEL ** -0.5),
        'q_norm_g': 1.0 + nrm(ks[10], (DEPTH, HEAD_DIM), 0.1),
        'k_norm_g': 1.0 + nrm(ks[11], (DEPTH, HEAD_DIM), 0.1),
        'sgu_w': nrm(ks[12], (DEPTH, SGU_GROUPS, SGU_CHUNK, SGU_CHUNK), SGU_CHUNK ** -0.5),
        'sgu_b': 1.0 + nrm(ks[13], (DEPTH, SGU_GROUPS, SGU_CHUNK), 0.1),
        'sgu_norm_g': 1.0 + nrm(ks[14], (DEPTH, SGU_WIDTH), 0.1),
        'pool_w': nrm(ks[15], (DEPTH, POOL_GROUPS, POOL_GD, POOL_GD), POOL_GD ** -0.5),
        'pool_scale': 1.0 + nrm(ks[16], (DEPTH, POOL_WIDTH), 0.1),
        'w_br_a': nrm(ks[17], (DEPTH, A_WIDTH, D_MODEL), A_WIDTH ** -0.5),
        'w_br_b': nrm(ks[18], (DEPTH, SGU_WIDTH, D_MODEL), SGU_WIDTH ** -0.5),
        'w_br_c': nrm(ks[19], (DEPTH, POOL_WIDTH, D_MODEL), POOL_WIDTH ** -0.5),
        'w_out': nrm(ks[20], (DEPTH, D_MODEL, D_MODEL), D_MODEL ** -0.5),
        'norm2_g': 1.0 + nrm(ks[21], (DEPTH, D_MODEL), 0.1),
        'w_ff_in': nrm(ks[22], (DEPTH, D_MODEL, 2 * D_FF), D_MODEL ** -0.5),
        'ff_conv_w': nrm(ks[23], (DEPTH, CONV_W, D_FF), CONV_W ** -0.5),
        'ff_conv_b': nrm(ks[24], (DEPTH, D_FF), 0.02),
        'w_ff_down': nrm(ks[25], (DEPTH, D_FF, D_MODEL), D_FF ** -0.5),
    }


def reference(x_prompt, x_sample, cache_k, cache_v, cache_idx_k, state_pool, state_ffn_conv, page_table,
              norm1_g, w_in, q_norm_g, k_norm_g, sgu_w, sgu_b, sgu_norm_g, pool_w, pool_scale,
              w_br_a, w_br_b, w_br_c, w_out, norm2_g, w_ff_in, ff_conv_w, ff_conv_b, w_ff_down):
    weights = (norm1_g, w_in, q_norm_g, k_norm_g, sgu_w, sgu_b, sgu_norm_g, pool_w, pool_scale,
               w_br_a, w_br_b, w_br_c, w_out, norm2_g, w_ff_in, ff_conv_w, ff_conv_b, w_ff_down)
    B = x_prompt.shape[0]
    pos_p = jnp.arange(x_prompt.shape[1])
    past = page_table.shape[1] * PAGE_SIZE
    pos_s = past + jnp.arange(x_sample.shape[1])
    xp, xs = x_prompt, x_sample
    kp, vp, ikp, poolp, ffp = [], [], [], [], []
    ks_, vs_, iks, sgus, pools, ffs = [], [], [], [], [], []
    for l in range(DEPTH):
        wl = tuple(w[l] for w in weights)
        xp, (k1, v1, ik1, _, pool1, ff1) = trunk_layer(
            xp, pos_p, dsa_prompt,
            jnp.zeros((B, POOL_STATE, POOL_WIDTH), xp.dtype),
            jnp.zeros((B, CONV_W - 1, D_FF), xp.dtype), wl)
        kp.append(k1); vp.append(v1); ikp.append(ik1); poolp.append(pool1); ffp.append(ff1)
        attend_s = functools.partial(dsa_sample, cache_k=cache_k, cache_v=cache_v,
                                     cache_idx_k=cache_idx_k, page_table=page_table, layer=l)
        xs, (k2, v2, ik2, sgv2, pool2, ff2) = trunk_layer(
            xs, pos_s, attend_s, state_pool[l], state_ffn_conv[l], wl)
        ks_.append(k2); vs_.append(v2); iks.append(ik2); sgus.append(sgv2); pools.append(pool2); ffs.append(ff2)
    return (xp, xs,
            jnp.stack(kp), jnp.stack(vp), jnp.stack(ikp), jnp.stack(poolp), jnp.stack(ffp),
            jnp.stack(ks_), jnp.stack(vs_), jnp.stack(iks), jnp.stack(sgus), jnp.stack(pools), jnp.stack(ffs))
```

```python
import functools

import jax
import jax.numpy as jnp
import numpy as np
from jax import lax
from jax.experimental import pallas as pl
from jax.experimental.pallas import tpu as pltpu

MXU_DTYPE = jnp.bfloat16
F32 = jnp.float32
I32 = jnp.int32

N_HEADS = 8
HEAD_DIM = 64
A_WIDTH = N_HEADS * HEAD_DIM
IDX_HEADS = 8
IDX_DIM = 64
IDX_ROPE = 32
TOPK_MAX = 256
SGU_GROUPS = 4
SGU_WIDTH = 256
SGU_CHUNK = 128
POOL_GROUPS = 4
POOL_WIDTH = 256
POOL_WINDOWS = (2, 4, 8, 16)
POOL_STATE = 15
CONV_W = 3
ROPE_THETA = 10000.0
EPS = 1e-6

LANES = 128
SUBLANES = 8
INT_MIN = -2 ** 31
NEG_BIG = -1e30
PROJ_COLS = 3072
VMEM_LIMIT = 56 * 1024 * 1024

_NT = (((1,), (1,)), ((), ()))


def _cparams(n_axes):
    return pltpu.CompilerParams(dimension_semantics=("arbitrary",) * n_axes, vmem_limit_bytes=VMEM_LIMIT)


def _const_spec(shape):
    nd = len(shape)
    return pl.BlockSpec(shape, lambda *_: (0,) * nd)


def _rope128(x, cos, sin, half):
    lane = lax.broadcasted_iota(I32, x.shape, 1)
    lo = (lane & (HEAD_DIM - 1)) < half
    rot = jnp.where(lo, pltpu.roll(x, LANES - half, 1), pltpu.roll(x, half, 1))
    return x * cos + rot * sin


def _head_rms(x, seg, g):
    ss = jnp.dot((x * x).astype(seg.dtype), seg, preferred_element_type=F32)
    return x * lax.rsqrt(ss * (1.0 / HEAD_DIM) + EPS) * g


def _rms(x, g):
    return x * lax.rsqrt(jnp.mean(x * x, axis=-1, keepdims=True) + EPS) * g


def _proj_kernel(x_ref, g1_ref, w_ref, qg_ref, kg_ref, sgg_ref, tab_ref, seg_ref,
                 qb_ref, k_ref, kb_ref, v_ref, vb_ref, iqb_ref, ikw_ref, ikd_ref, u_ref, vv_ref, xc_ref):
    mxu = w_ref.dtype
    xn = _rms(x_ref[...], g1_ref[...]).astype(mxu)

    def mm(c0, c1):
        return jnp.dot(xn, w_ref[:, c0:c1], preferred_element_type=F32)

    cqk, sqk = tab_ref[:, 0:128], tab_ref[:, 128:256]
    ciq, siq = tab_ref[:, 256:384], tab_ref[:, 384:512]
    cikw, sikw = tab_ref[:, 512:640], tab_ref[:, 640:768]
    seg = seg_ref[...]
    half_qk = HEAD_DIM // 2
    half_idx = IDX_ROPE // 2

    q = _head_rms(mm(0, 512), seg, qg_ref[...])
    for c in range(4):
        sl = slice(LANES * c, LANES * (c + 1))
        qb_ref[:, sl] = (_rope128(q[:, sl], cqk, sqk, half_qk) * (HEAD_DIM ** -0.5)).astype(mxu)
    k = _head_rms(mm(512, 1024), seg, kg_ref[...])
    for c in range(4):
        sl = slice(LANES * c, LANES * (c + 1))
        kr = _rope128(k[:, sl], cqk, sqk, half_qk)
        k_ref[:, sl] = kr
        kb_ref[:, sl] = kr.astype(mxu)
    v = mm(1024, 1536)
    v_ref[...] = v
    vb_ref[...] = v.astype(mxu)
    iq = mm(1536, 2048)
    for c in range(4):
        sl = slice(LANES * c, LANES * (c + 1))
        iqb_ref[:, sl] = (_rope128(iq[:, sl], ciq, siq, half_idx) * (IDX_DIM ** -0.5)).astype(mxu)
    ikw_ref[...] = _rope128(mm(2048, 2176), cikw, sikw, half_idx)
    ikd_ref[...] = _rope128(mm(2176, 2304), ciq, siq, half_idx).astype(mxu)
    sg = jax.nn.gelu(mm(2304, 2816))
    u_ref[...] = sg[:, :SGU_WIDTH]
    vv_ref[...] = _rms(sg[:, SGU_WIDTH:], sgg_ref[...])
    xc_ref[...] = mm(2816, 3072)


def _proj(x, g1, w, qg, kg, sgg, tab, seg, *, tm, tab_tiles):
    T, D = x.shape
    mxu = w.dtype
    n = T // tm
    row = lambda width: pl.BlockSpec((tm, width), lambda i: (i, 0))
    out_shape = [
        jax.ShapeDtypeStruct((T, A_WIDTH), mxu),
        jax.ShapeDtypeStruct((T, A_WIDTH), F32),
        jax.ShapeDtypeStruct((T, A_WIDTH), mxu),
        jax.ShapeDtypeStruct((T, A_WIDTH), F32),
        jax.ShapeDtypeStruct((T, A_WIDTH), mxu),
        jax.ShapeDtypeStruct((T, IDX_HEADS * IDX_DIM), mxu),
        jax.ShapeDtypeStruct((T, LANES), F32),
        jax.ShapeDtypeStruct((T, LANES), mxu),
        jax.ShapeDtypeStruct((T, SGU_WIDTH), F32),
        jax.ShapeDtypeStruct((T, SGU_WIDTH), F32),
        jax.ShapeDtypeStruct((T, POOL_WIDTH), F32),
    ]
    out_specs = [row(s.shape[1]) for s in out_shape]
    in_specs = [
        row(D), _const_spec(g1.shape), _const_spec(w.shape), _const_spec(qg.shape), _const_spec(kg.shape),
        _const_spec(sgg.shape),
        pl.BlockSpec((tm, tab.shape[1]), lambda i: (i % tab_tiles, 0)),
        _const_spec(seg.shape),
    ]
    return pl.pallas_call(
        _proj_kernel, grid=(n,), in_specs=in_specs, out_specs=out_specs, out_shape=out_shape,
        compiler_params=_cparams(1), name="proj",
    )(x, g1, w, qg, kg, sgg, tab, seg)


def _sortable_key(s):
    b = lax.bitcast_convert_type(s, I32)
    key = jnp.where(b < 0, b ^ jnp.int32(0x7FFFFFFF), b)
    return jnp.where(key == -1, 0, key)


def _fold_lanes(m):
    out = m[:, 0:LANES]
    for c in range(1, m.shape[1] // LANES):
        out = out + m[:, c * LANES:(c + 1) * LANES]
    return out


def _topk_threshold(kint_ref, thr_ref, cut_ref, n_tiles, tile, rows, topk, n_cols):
    def count(pred):
        def body(j, c):
            off = pl.multiple_of(j * tile, tile)
            key = kint_ref[:, pl.ds(off, tile)]
            col = off + lax.broadcasted_iota(I32, (rows, tile), 1)
            return c + _fold_lanes(pred(key, col).astype(F32))
        part = lax.fori_loop(0, n_tiles, body, jnp.zeros((rows, LANES), F32))
        return jnp.sum(part, axis=1, keepdims=True)

    kf = float(topk)
    c0 = count(lambda key, col: key >= 0)
    t = jnp.where(c0 >= kf, jnp.int32(0), jnp.int32(INT_MIN))

    def bit_body(b, t):
        cand = t + jnp.left_shift(jnp.int32(1), 30 - b)
        cnt = count(lambda key, col: key >= cand)
        return jnp.where(cnt >= kf, cand, t)

    t = lax.fori_loop(0, 31, bit_body, t)
    t = jnp.maximum(t, jnp.int32(INT_MIN + 1))
    cge = count(lambda key, col: key >= t)
    thr_ref[...] = t
    cut_ref[...] = jnp.full((rows, 1), n_cols, I32)

    @pl.when(jnp.max(cge) > kf)
    def _():
        need = kf - count(lambda key, col: key > t)
        nbits = max(1, (n_cols - 1).bit_length())

        def idx_body(b, p):
            cand = p + jnp.left_shift(jnp.int32(1), nbits - 1 - b)
            cnt = count(lambda key, col: (key == t) & (col < cand))
            return jnp.where(cnt < need, cand, p)

        p = lax.fori_loop(0, nbits, idx_body, jnp.zeros((rows, 1), I32))
        cut_ref[...] = jnp.where(cge > kf, p, jnp.int32(n_cols))


def _selected(key, col, t, cut):
    return (key > t) | ((key == t) & (col <= cut))


def _dsa_prompt_kernel(qb_ref, iqb_ref, ikw_ref, kb_ref, vb_ref, ikd_ref, o_ref,
                       kint_ref, qm_ref, iqm_ref, acc_ref, m_ref, l_ref, thr_ref, cut_ref, *, topk, qb, seq):
    i = pl.program_id(1)
    n_tiles = i + 1
    lane = lax.broadcasted_iota(I32, (qb, LANES), 1)
    lo = lane < HEAD_DIM
    for h in range(N_HEADS):
        sl = slice(LANES * (h // 2), LANES * (h // 2 + 1))
        msk = lo if h % 2 == 0 else jnp.logical_not(lo)
        qm_ref[h] = jnp.where(msk, qb_ref[0, :, sl], jnp.zeros((), qb_ref.dtype))
        iqm_ref[h] = jnp.where(msk, iqb_ref[0, :, sl], jnp.zeros((), iqb_ref.dtype))
    qpos = i * qb + lax.broadcasted_iota(I32, (qb, 1), 0)

    def score_tile(j, carry):
        off = pl.multiple_of(j * qb, qb)
        ik = ikd_ref[0, pl.ds(off, qb), :]
        sc = jnp.zeros((qb, qb), F32)
        for h in range(IDX_HEADS):
            s = lax.dot_general(iqm_ref[h], ik, _NT, preferred_element_type=F32)
            sc = sc + jnp.maximum(s, 0.0) * ikw_ref[0, :, IDX_DIM + h:IDX_DIM + h + 1]
        kpos = off + lax.broadcasted_iota(I32, (qb, qb), 1)
        kint_ref[:, pl.ds(off, qb)] = jnp.where(kpos <= qpos, _sortable_key(sc), jnp.int32(INT_MIN))
        return carry

    lax.fori_loop(0, n_tiles, score_tile, 0)
    _topk_threshold(kint_ref, thr_ref, cut_ref, n_tiles, qb, qb, topk, seq)
    t = thr_ref[...]
    cut = cut_ref[...]

    acc_ref[...] = jnp.zeros_like(acc_ref)
    m_ref[...] = jnp.full(m_ref.shape, NEG_BIG, F32)
    l_ref[...] = jnp.zeros_like(l_ref)

    def att_tile(j, carry):
        off = pl.multiple_of(j * qb, qb)
        kpos = off + lax.broadcasted_iota(I32, (qb, qb), 1)
        sel = _selected(kint_ref[:, pl.ds(off, qb)], kpos, t, cut)
        for p in range(N_HEADS // 2):
            sl = slice(LANES * p, LANES * (p + 1))
            kt = kb_ref[0, pl.ds(off, qb), sl]
            vt = vb_ref[0, pl.ds(off, qb), sl]
            alphas, pvs = [], []
            for h in (2 * p, 2 * p + 1):
                s = lax.dot_general(qm_ref[h], kt, _NT, preferred_element_type=F32)
                s = jnp.where(sel, s, NEG_BIG)
                m_old = m_ref[h]
                m_new = jnp.maximum(m_old, jnp.max(s, axis=1, keepdims=True))
                alpha = jnp.exp(m_old - m_new)
                pm = jnp.exp(s - m_new)
                l_ref[h] = alpha * l_ref[h] + jnp.sum(pm, axis=1, keepdims=True)
                m_ref[h] = m_new
                alphas.append(alpha)
                pvs.append(jnp.dot(pm.astype(vt.dtype), vt, preferred_element_type=F32))
            acc_ref[:, sl] = jnp.where(lo, alphas[0], alphas[1]) * acc_ref[:, sl] + jnp.where(lo, pvs[0], pvs[1])
        return carry

    lax.fori_loop(0, n_tiles, att_tile, 0)
    for p in range(N_HEADS // 2):
        sl = slice(LANES * p, LANES * (p + 1))
        o_ref[0, :, sl] = (acc_ref[:, sl] / jnp.where(lo, l_ref[2 * p], l_ref[2 * p + 1])).astype(o_ref.dtype)


def _dsa_prompt(qb_, iqb, ikw, kb, vb, ikd, *, qb, topk):
    B, S, _ = qb_.shape
    mxu = qb_.dtype
    blk = lambda width: pl.BlockSpec((1, qb, width), lambda b, i: (b, i, 0))
    full = lambda width: pl.BlockSpec((1, S, width), lambda b, i: (b, 0, 0))
    kern = functools.partial(_dsa_prompt_kernel, topk=topk, qb=qb, seq=S)
    return pl.pallas_call(
        kern, grid=(B, S // qb),
        in_specs=[blk(A_WIDTH), blk(A_WIDTH), blk(LANES), full(A_WIDTH), full(A_WIDTH), full(LANES)],
        out_specs=blk(A_WIDTH),
        out_shape=jax.ShapeDtypeStruct((B, S, A_WIDTH), mxu),
        scratch_shapes=[
            pltpu.VMEM((qb, S), I32),
            pltpu.VMEM((N_HEADS, qb, LANES), mxu),
            pltpu.VMEM((N_HEADS, qb, LANES), mxu),
            pltpu.VMEM((qb, A_WIDTH), F32),
            pltpu.VMEM((N_HEADS, qb, 1), F32),
            pltpu.VMEM((N_HEADS, qb, 1), F32),
            pltpu.VMEM((qb, 1), I32),
            pltpu.VMEM((qb, 1), I32),
        ],
        compiler_params=_cparams(2), name="dsa_prompt",
    )(qb_, iqb, ikw, kb, vb, ikd)


PAGES_PER_STEP = 8


def _head_sum(s):
    return jnp.concatenate(
        [jnp.sum(s[IDX_HEADS * t:IDX_HEADS * (t + 1)], axis=0, keepdims=True) for t in range(s.shape[0] // IDX_HEADS)],
        axis=0)


def _sample_scores_kernel(pt_ref, iq_ref, w_ref, ikn_ref, *rest, n_past_steps, page):
    pages = rest[:PAGES_PER_STEP]
    sc_ref = rest[PAGES_PER_STEP]
    c = pl.program_id(1)
    iq = iq_ref[0]
    w = w_ref[0]
    nt = iq.shape[0] // IDX_HEADS

    def scores(ik):
        s = jnp.dot(iq, ik.astype(iq.dtype), preferred_element_type=F32)
        return _head_sum(jnp.maximum(s, 0.0) * w)

    @pl.when(c < n_past_steps)
    def _():
        for r in range(PAGES_PER_STEP):
            sc_ref[0, :, r * page:(r + 1) * page] = scores(pages[r][...])

    @pl.when(c == n_past_steps)
    def _():
        s = scores(ikn_ref[0])
        col = lax.broadcasted_iota(I32, s.shape, 1)
        row = lax.broadcasted_iota(I32, s.shape, 0)
        sc_ref[0, :, 0:page] = jnp.where(col <= row, s, -jnp.inf)
        sc_ref[0, :, page:] = jnp.full((nt, (PAGES_PER_STEP - 1) * page), -jnp.inf, F32)


def _page_specs(n_past_steps, layer, rows, page):
    def make(r):
        def imap(b, c, pt):
            return (layer, pt[b, jnp.minimum(c, n_past_steps - 1) * PAGES_PER_STEP + r], 0, 0)
        return pl.BlockSpec((None, None, rows, page), imap)
    return [make(r) for r in range(PAGES_PER_STEP)]


def _sample_scores(page_table, iq_rows, w_rows, ik_new, cache_ik, *, layer):
    DB, n_pages = page_table.shape
    page = cache_ik.shape[3]
    n_past_steps = n_pages // PAGES_PER_STEP
    rows = iq_rows.shape[1]
    nt = rows // IDX_HEADS
    step_cols = PAGES_PER_STEP * page
    grid_spec = pltpu.PrefetchScalarGridSpec(
        num_scalar_prefetch=1, grid=(DB, n_past_steps + 1),
        in_specs=[
            pl.BlockSpec((1, rows, IDX_DIM), lambda b, c, pt: (b, 0, 0)),
            pl.BlockSpec((1, rows, 1), lambda b, c, pt: (b, 0, 0)),
            pl.BlockSpec((1, IDX_DIM, page), lambda b, c, pt: (b, 0, 0)),
        ] + _page_specs(n_past_steps, layer, IDX_DIM, page),
        out_specs=pl.BlockSpec((1, nt, step_cols), lambda b, c, pt: (b, 0, c)),
    )
    kern = functools.partial(_sample_scores_kernel, n_past_steps=n_past_steps, page=page)
    return pl.pallas_call(
        kern, grid_spec=grid_spec,
        out_shape=jax.ShapeDtypeStruct((DB, nt, (n_past_steps + 1) * step_cols), F32),
        compiler_params=_cparams(2), name="sample_scores",
    )(page_table, iq_rows, w_rows, ik_new, *([cache_ik] * PAGES_PER_STEP))


def _sample_threshold_kernel(sc_ref, thr_ref, cut_ref, kint_ref, t_ref, c_ref, *, topk, tile):
    rows, cols = sc_ref.shape
    kint_ref[...] = jnp.where(sc_ref[...] == -jnp.inf, jnp.int32(INT_MIN), _sortable_key(sc_ref[...]))
    _topk_threshold(kint_ref, t_ref, c_ref, cols // tile, tile, rows, topk, cols)
    thr_ref[...] = jnp.broadcast_to(t_ref[...], thr_ref.shape)
    cut_ref[...] = jnp.broadcast_to(c_ref[...], cut_ref.shape)


def _sample_threshold(sc, *, topk):
    rows, cols = sc.shape
    kern = functools.partial(_sample_threshold_kernel, topk=topk, tile=2 * LANES)
    return pl.pallas_call(
        kern, grid=(1,),
        in_specs=[_const_spec(sc.shape)],
        out_specs=[_const_spec((rows, LANES)), _const_spec((rows, LANES))],
        out_shape=[jax.ShapeDtypeStruct((rows, LANES), I32)] * 2,
        scratch_shapes=[pltpu.VMEM((rows, cols), I32), pltpu.VMEM((rows, 1), I32), pltpu.VMEM((rows, 1), I32)],
        compiler_params=_cparams(1), name="sample_threshold",
    )(sc)


def _sample_attn_kernel(pt_ref, q_ref, sc_ref, thr_ref, cut_ref, kn_ref, vn_ref, hm_ref, *rest, n_past_steps, page):
    kp = rest[:PAGES_PER_STEP]
    vp = rest[PAGES_PER_STEP:2 * PAGES_PER_STEP]
    o_ref = rest[2 * PAGES_PER_STEP]
    m_ref, l_ref, acc_ref = rest[2 * PAGES_PER_STEP + 1:]
    c = pl.program_id(1)
    q = q_ref[0]
    mxu = q.dtype
    nt = sc_ref.shape[1]
    step_cols = PAGES_PER_STEP * page

    @pl.when(c == 0)
    def _():
        m_ref[...] = jnp.full(m_ref.shape, NEG_BIG, F32)
        l_ref[...] = jnp.zeros_like(l_ref)
        acc_ref[...] = jnp.zeros_like(acc_ref)

    def rows_th(x):
        return jnp.concatenate([jnp.broadcast_to(x[t:t + 1], (N_HEADS, x.shape[1])) for t in range(nt)], axis=0)

    def update(keys, vals, n_cols):
        sc = sc_ref[0, :, 0:n_cols]
        key = jnp.where(sc == -jnp.inf, jnp.int32(INT_MIN), _sortable_key(sc))
        col = c * step_cols + lax.broadcasted_iota(I32, key.shape, 1)
        sel = rows_th(_selected(key, col, thr_ref[0, :, 0:1], cut_ref[0, :, 0:1]).astype(F32)) > 0.5
        s = jnp.concatenate([jnp.dot(q, kk.astype(mxu), preferred_element_type=F32) for kk in keys], axis=1)
        s = jnp.where(sel, s, NEG_BIG)
        m_old = m_ref[...]
        m_new = jnp.maximum(m_old, jnp.max(s, axis=1, keepdims=True))
        alpha = jnp.exp(m_old - m_new)
        pm = jnp.exp(s - m_new)
        l_ref[...] = alpha * l_ref[...] + jnp.sum(pm, axis=1, keepdims=True)
        m_ref[...] = m_new
        pv = jnp.zeros(acc_ref.shape, F32)
        for r, vv in enumerate(vals):
            pv = pv + lax.dot_general(pm[:, r * page:(r + 1) * page].astype(mxu), vv.astype(mxu), _NT,
                                      preferred_element_type=F32)
        acc_ref[...] = alpha * acc_ref[...] + pv

    @pl.when(c < n_past_steps)
    def _():
        update([r[...] for r in kp], [r[...] for r in vp], step_cols)

    @pl.when(c == n_past_steps)
    def _():
        update([kn_ref[0]], [vn_ref[0]], page)
        out = acc_ref[...] / l_ref[...] * hm_ref[...]
        o_ref[0] = _head_sum(out).astype(o_ref.dtype)


def _sample_attn(page_table, q_rows, sc, thr, cut, k_new, v_new, head_mask, cache_k, cache_v, *, layer):
    DB, n_pages = page_table.shape
    page = cache_k.shape[3]
    n_past_steps = n_pages // PAGES_PER_STEP
    rows = q_rows.shape[1]
    nt = rows // N_HEADS
    step_cols = PAGES_PER_STEP * page
    per_b = lambda shape: pl.BlockSpec((1,) + shape, lambda b, c, pt: (b, 0, 0))
    grid_spec = pltpu.PrefetchScalarGridSpec(
        num_scalar_prefetch=1, grid=(DB, n_past_steps + 1),
        in_specs=[
            per_b((rows, A_WIDTH)),
            pl.BlockSpec((1, nt, step_cols), lambda b, c, pt: (b, 0, c)),
            per_b((nt, LANES)), per_b((nt, LANES)),
            per_b((A_WIDTH, page)), per_b((A_WIDTH, page)),
            pl.BlockSpec((rows, A_WIDTH), lambda b, c, pt: (0, 0)),
        ] + _page_specs(n_past_steps, layer, A_WIDTH, page) * 2,
        out_specs=per_b((nt, A_WIDTH)),
        scratch_shapes=[pltpu.VMEM((rows, 1), F32), pltpu.VMEM((rows, 1), F32), pltpu.VMEM((rows, A_WIDTH), F32)],
    )
    kern = functools.partial(_sample_attn_kernel, n_past_steps=n_past_steps, page=page)
    return pl.pallas_call(
        kern, grid_spec=grid_spec,
        out_shape=jax.ShapeDtypeStruct((DB, nt, A_WIDTH), q_rows.dtype),
        compiler_params=_cparams(2), name="sample_attn",
    )(page_table, q_rows, sc, thr, cut, k_new, v_new, head_mask,
      *([cache_k] * PAGES_PER_STEP), *([cache_v] * PAGES_PER_STEP))


def _shift_rows(e, k):
    return pltpu.roll(e, k, 0)


def _merge_kernel(x_ref, att_ref, u_ref, vv_ref, xc_ref, halo_ref, cnt_ref,
                  g1_ref, wgt_ref, wa_ref, wb_ref, wc_ref, wout_ref, ws_ref, btab_ref, wpool_ref, pscale_ref,
                  o_ref, *, row_stride, tiles_per_seq, zero_first_halo):
    mxu = wgt_ref.dtype
    tm = x_ref.shape[0]
    x = x_ref[...]
    xn = _rms(x, g1_ref[...]).astype(mxu)
    d = x.shape[1]

    def gate(n):
        return jax.nn.sigmoid(jnp.dot(xn, wgt_ref[:, n * d:(n + 1) * d], preferred_element_type=F32))

    merged = gate(0) * jnp.dot(att_ref[...], wa_ref[...], preferred_element_type=F32)

    lane = lax.broadcasted_iota(I32, (SGU_CHUNK, LANES), 1)
    lo = lane < (SGU_WIDTH // SGU_GROUPS)
    sgo = []
    for c in range(tm // SGU_CHUNK):
        rs = slice(c * SGU_CHUNK, (c + 1) * SGU_CHUNK)
        vvb = vv_ref[rs, :].astype(mxu)
        mix = []
        for p in range(SGU_GROUPS // 2):
            pair = vvb[:, LANES * p:LANES * (p + 1)]
            r0 = jnp.dot(ws_ref[2 * p], pair, preferred_element_type=F32)
            r1 = jnp.dot(ws_ref[2 * p + 1], pair, preferred_element_type=F32)
            mix.append(jnp.where(lo, r0, r1))
        sgo.append(u_ref[rs, :] * (jnp.concatenate(mix, axis=1) + btab_ref[...]))
    sgo = jnp.concatenate(sgo, axis=0).astype(mxu)
    merged = merged + gate(1) * jnp.dot(sgo, wb_ref[...], preferred_element_type=F32)

    xc = xc_ref[...]
    halo = halo_ref[...]
    if zero_first_halo:
        first = (pl.program_id(0) % tiles_per_seq) == 0
        halo = jnp.where(first, jnp.zeros_like(halo), halo)
    hp = halo.shape[0]
    e = jnp.concatenate([halo, xc], axis=0)
    s1 = e + _shift_rows(e, row_stride)
    s2 = s1 + _shift_rows(s1, 2 * row_stride)
    s3 = s2 + _shift_rows(s2, 4 * row_stride)
    s4 = s3 + _shift_rows(s3, 8 * row_stride)
    gd = POOL_WIDTH // POOL_GROUPS
    wsum = jnp.concatenate([s[hp:, g * gd:(g + 1) * gd] for g, s in enumerate((s1, s2, s3, s4))], axis=1)
    pooled = (wsum / cnt_ref[...] - xc).astype(mxu)
    po = (jnp.dot(pooled, wpool_ref[...], preferred_element_type=F32) * pscale_ref[...]).astype(mxu)
    merged = merged + gate(2) * jnp.dot(po, wc_ref[...], preferred_element_type=F32)

    o_ref[...] = x + jnp.dot(merged.astype(mxu), wout_ref[...], preferred_element_type=F32)


def _merge(x, att, u, vv, xc, halo, cnt, g1, wgt, wa, wb, wc, wout, ws, btab, wpool, pscale,
           *, tm, row_stride, tiles_per_seq, halo_rows, halo_from_xc, cnt_tiles):
    T, D = x.shape
    n = T // tm
    row = lambda width: pl.BlockSpec((tm, width), lambda i: (i, 0))
    if halo_from_xc:
        per = tm // halo_rows
        halo_spec = pl.BlockSpec((halo_rows, xc.shape[1]), lambda i: (jnp.maximum(i * per - 1, 0), 0))
    else:
        halo_spec = _const_spec(halo.shape)
    consts = [g1, wgt, wa, wb, wc, wout, ws, btab, wpool, pscale]
    kern = functools.partial(_merge_kernel, row_stride=row_stride, tiles_per_seq=tiles_per_seq,
                             zero_first_halo=halo_from_xc)
    return pl.pallas_call(
        kern, grid=(n,),
        in_specs=[row(D), row(att.shape[1]), row(u.shape[1]), row(vv.shape[1]), row(xc.shape[1]), halo_spec,
                  pl.BlockSpec((tm, cnt.shape[1]), lambda i: (i % cnt_tiles, 0))]
        + [_const_spec(c.shape) for c in consts],
        out_specs=row(D), out_shape=jax.ShapeDtypeStruct((T, D), F32),
        compiler_params=_cparams(1), name="merge",
    )(x, att, u, vv, xc, halo, cnt, *consts)


FFN_CHUNK = 256


def _ffn_kernel(x_ref, g2_ref, wa_ref, wu_ref, cw_ref, cb_ref, wd_ref, halo_ref, o_ref, tail_ref, carry_ref,
                *, row_stride, tiles_per_seq, use_carry):
    mxu = wa_ref.dtype
    x = x_ref[...]
    tm = x.shape[0]
    xn = _rms(x, g2_ref[...]).astype(mxu)
    dff = wa_ref.shape[1]
    tail = tail_ref.shape[1]
    acc = jnp.zeros(x.shape, F32)
    if use_carry:
        first = (pl.program_id(0) % tiles_per_seq) == 0
    for c0 in range(0, dff, FFN_CHUNK):
        cs = slice(c0, c0 + FFN_CHUNK)
        a = jnp.dot(xn, wa_ref[:, cs], preferred_element_type=F32)
        up = jnp.dot(xn, wu_ref[:, cs], preferred_element_type=F32)
        if use_carry:
            halo = jnp.where(first, 0.0, carry_ref[:, cs])
            carry_ref[:, cs] = a[tm - carry_ref.shape[0]:, :]
        else:
            halo = halo_ref[:, cs]
        hp = halo.shape[0]
        e = jnp.concatenate([halo, a], axis=0)
        p1 = _shift_rows(e, row_stride)[hp:]
        p2 = _shift_rows(e, 2 * row_stride)[hp:]
        conv = p2 * cw_ref[0:1, cs] + p1 * cw_ref[1:2, cs] + a * cw_ref[2:3, cs] + cb_ref[:, cs]
        act = (jax.nn.silu(conv) * up).astype(mxu)
        acc = acc + jnp.dot(act, wd_ref[cs, :], preferred_element_type=F32)
        tail_ref[0, :, cs] = a[tm - tail:, :]
    o_ref[...] = x + acc


def _ffn(x, g2, wa, wu, cw, cb, wd, halo, *, tm, row_stride, tiles_per_seq, use_carry, tail):
    T, D = x.shape
    n = T // tm
    dff = wa.shape[1]
    row = pl.BlockSpec((tm, D), lambda i: (i, 0))
    consts = [g2, wa, wu, cw, cb, wd, halo]
    kern = functools.partial(_ffn_kernel, row_stride=row_stride, tiles_per_seq=tiles_per_seq, use_carry=use_carry)
    return pl.pallas_call(
        kern, grid=(n,),
        in_specs=[row] + [_const_spec(c.shape) for c in consts],
        out_specs=[row, pl.BlockSpec((1, tail, dff), lambda i: (i, 0, 0))],
        out_shape=[jax.ShapeDtypeStruct((T, D), F32), jax.ShapeDtypeStruct((n, tail, dff), F32)],
        scratch_shapes=[pltpu.VMEM((SUBLANES, dff), F32)],
        compiler_params=_cparams(1), name="ffn",
    )(x, *consts)


def _pack_w_in(w_in):
    d = w_in.shape[0]
    o = 0
    q, k, v = (w_in[:, o + i * A_WIDTH:o + (i + 1) * A_WIDTH] for i in range(3))
    o += 3 * A_WIDTH
    iq = w_in[:, o:o + IDX_HEADS * IDX_DIM]
    o += IDX_HEADS * IDX_DIM
    ik = w_in[:, o:o + IDX_DIM]
    o += IDX_DIM
    iw = w_in[:, o:o + IDX_HEADS]
    o += IDX_HEADS
    sg = w_in[:, o:o + 2 * SGU_WIDTH]
    o += 2 * SGU_WIDTH
    xc = w_in[:, o:o + POOL_WIDTH]
    o += POOL_WIDTH
    gt = w_in[:, o:]
    pad = jnp.zeros((d, LANES - IDX_DIM - IDX_HEADS), w_in.dtype)
    packed = jnp.concatenate([q, k, v, iq, ik, iw, pad, ik, ik, sg, xc], axis=1)
    assert packed.shape[1] == PROJ_COLS
    return packed, gt


def _rope_tables(pos):
    posf = pos.astype(F32)[:, None]

    def cs(half):
        inv = ROPE_THETA ** (-jnp.arange(half, dtype=F32) / half)
        ang = posf * inv[None, :]
        return jnp.cos(ang), jnp.sin(ang)

    t = pos.shape[0]
    c32, s32 = cs(HEAD_DIM // 2)
    cqk = jnp.tile(jnp.concatenate([c32, c32], axis=1), (1, 2))
    sqk = jnp.tile(jnp.concatenate([-s32, s32], axis=1), (1, 2))
    c16, s16 = cs(IDX_ROPE // 2)
    rest = IDX_DIM - IDX_ROPE
    ci = jnp.concatenate([c16, c16, jnp.ones((t, rest), F32)], axis=1)
    si = jnp.concatenate([-s16, s16, jnp.zeros((t, rest), F32)], axis=1)
    ciq, siq = jnp.tile(ci, (1, 2)), jnp.tile(si, (1, 2))
    wpad = LANES - IDX_DIM - IDX_HEADS
    cikw = jnp.concatenate([ci, jnp.full((t, IDX_HEADS), IDX_HEADS ** -0.5, F32), jnp.zeros((t, wpad), F32)], axis=1)
    sikw = jnp.concatenate([si, jnp.zeros((t, LANES - IDX_DIM), F32)], axis=1)
    return jnp.concatenate([cqk, sqk, ciq, siq, cikw, sikw], axis=1)


def _block_diag(blocks):
    n = len(blocks)
    r, c = blocks[0].shape
    out = jnp.zeros((n * r, n * c), blocks[0].dtype)
    for i, b in enumerate(blocks):
        out = out.at[i * r:(i + 1) * r, i * c:(i + 1) * c].set(b)
    return out


def _pool_counts(pos):
    gd = POOL_WIDTH // POOL_GROUPS
    cols = [jnp.broadcast_to(jnp.minimum(pos + 1, w).astype(F32)[:, None], (pos.shape[0], gd)) for w in POOL_WINDOWS]
    return jnp.concatenate(cols, axis=1)


def _layer_weights(l, norm1_g, w_in, q_norm_g, k_norm_g, sgu_w, sgu_b, sgu_norm_g, pool_w, pool_scale,
                   w_br_a, w_br_b, w_br_c, w_out, norm2_g, w_ff_in, ff_conv_w, ff_conv_b, w_ff_down):
    mxu = MXU_DTYPE
    packed, gt = _pack_w_in(w_in[l])
    dff = w_ff_in.shape[2] // 2
    tril = jnp.tril(jnp.ones((SGU_CHUNK, SGU_CHUNK), bool))
    return dict(
        g1=norm1_g[l][None, :], w_proj=packed.astype(mxu), w_gt=gt.astype(mxu),
        qg=jnp.tile(q_norm_g[l], N_HEADS)[None, :], kg=jnp.tile(k_norm_g[l], N_HEADS)[None, :],
        sgg=sgu_norm_g[l][None, :],
        ws=jnp.where(tril[None], sgu_w[l], 0.0),
        sgu_b=sgu_b[l],
        wpool=_block_diag([pool_w[l, g] for g in range(POOL_GROUPS)]).astype(mxu),
        pscale=pool_scale[l][None, :],
        wa=w_br_a[l].astype(mxu), wb=w_br_b[l].astype(mxu), wc=w_br_c[l].astype(mxu), wout=w_out[l].astype(mxu),
        g2=norm2_g[l][None, :],
        w_ffa=w_ff_in[l, :, :dff].astype(mxu), w_ffu=w_ff_in[l, :, dff:].astype(mxu),
        cw=jnp.pad(ff_conv_w[l], ((0, SUBLANES - CONV_W), (0, 0))), cb=ff_conv_b[l][None, :],
        w_ffd=w_ff_down[l].astype(mxu),
    )


def _sgu_bias_table(sgu_b, t_of_row):
    gd = SGU_WIDTH // SGU_GROUPS
    return jnp.repeat(jnp.transpose(sgu_b)[t_of_row], gd, axis=1)


def _prompt_layer(x, lw, tabs, cnt, *, B, S):
    mxu = MXU_DTYPE
    T = B * S
    tm_proj, tm_merge, tm_ffn, qb = 512, 256, 512, 256
    seg = _block_diag([jnp.ones((HEAD_DIM, HEAD_DIM), F32)] * N_HEADS).astype(mxu)
    qb_, k, kb, v, vb, iqb, ikw, ikd, u, vv, xc = _proj(
        x, lw["g1"], lw["w_proj"], lw["qg"], lw["kg"], lw["sgg"], tabs, seg, tm=tm_proj, tab_tiles=S // tm_proj)
    r3 = lambda a: a.reshape(B, S, a.shape[1])
    topk = min(TOPK_MAX, S // 4)
    att = _dsa_prompt(r3(qb_), r3(iqb), r3(ikw), r3(kb), r3(vb), r3(ikd), qb=qb, topk=topk).reshape(T, A_WIDTH)
    btab = _sgu_bias_table(lw["sgu_b"], jnp.arange(SGU_CHUNK))
    x1 = _merge(x, att, u, vv, xc, xc, cnt, lw["g1"], lw["w_gt"], lw["wa"], lw["wb"], lw["wc"], lw["wout"],
                lw["ws"].astype(mxu), btab, lw["wpool"], lw["pscale"],
                tm=tm_merge, row_stride=1, tiles_per_seq=S // tm_merge, halo_rows=16, halo_from_xc=True,
                cnt_tiles=S // tm_merge)
    dff = lw["w_ffa"].shape[1]
    x2, tails = _ffn(x1, lw["g2"], lw["w_ffa"], lw["w_ffu"], lw["cw"], lw["cb"], lw["w_ffd"],
                     jnp.zeros((SUBLANES, dff), F32),
                     tm=tm_ffn, row_stride=1, tiles_per_seq=S // tm_ffn, use_carry=True, tail=SUBLANES)
    n_t = S // tm_ffn
    ff_state = tails.reshape(B, n_t, SUBLANES, dff)[:, -1, SUBLANES - (CONV_W - 1):, :]
    pool_state = xc.reshape(B, S, POOL_WIDTH)[:, S - POOL_STATE:, :]
    return x2, (k.reshape(B, S, N_HEADS, HEAD_DIM), v.reshape(B, S, N_HEADS, HEAD_DIM),
                ikw.reshape(B, S, LANES)[:, :, :IDX_DIM], pool_state, ff_state)


def _sample_layer(x, lw, tabs, cnt, l, cache_k, cache_v, cache_ik, state_pool, state_ffn, page_table, *, DB, TS):
    mxu = MXU_DTYPE
    T = TS * DB
    page = cache_ik.shape[3]
    past = page_table.shape[1] * page
    seg = _block_diag([jnp.ones((HEAD_DIM, HEAD_DIM), F32)] * N_HEADS).astype(mxu)
    qb_, k, kb, v, vb, iqb, ikw, ikd, u, vv, xc = _proj(
        x, lw["g1"], lw["w_proj"], lw["qg"], lw["kg"], lw["sgg"], tabs, seg, tm=T, tab_tiles=1)
    bm = lambda a: jnp.transpose(a.reshape(TS, DB, a.shape[1]), (1, 0, 2))

    iq_rows = bm(iqb).reshape(DB, TS * IDX_HEADS, IDX_DIM)
    w_rows = bm(ikw)[:, :, IDX_DIM:IDX_DIM + IDX_HEADS].reshape(DB, TS * IDX_HEADS, 1)
    new_t = lambda a: jnp.pad(jnp.transpose(a, (0, 2, 1)), ((0, 0), (0, 0), (0, page - TS)))
    ik_new = new_t(bm(ikd)[:, :, :IDX_DIM])
    sc = _sample_scores(page_table, iq_rows, w_rows, ik_new, cache_ik, layer=l)
    topk = min(TOPK_MAX, (past + TS) // 4)
    thr, cut = _sample_threshold(sc.reshape(DB * TS, sc.shape[2]), topk=topk)
    hm = jnp.repeat(jnp.eye(N_HEADS, dtype=F32), HEAD_DIM, axis=1)
    q_rows = (bm(qb_)[:, :, None, :] * hm[None, None].astype(mxu)).reshape(DB, TS * N_HEADS, A_WIDTH)
    att = _sample_attn(page_table, q_rows, sc, thr.reshape(DB, TS, LANES), cut.reshape(DB, TS, LANES),
                       new_t(bm(kb)), new_t(bm(vb)), jnp.tile(hm, (TS, 1)), cache_k, cache_v, layer=l)
    att = jnp.transpose(att, (1, 0, 2)).reshape(T, A_WIDTH)

    eye = jnp.eye(DB, dtype=F32)
    ws = jnp.stack([jnp.kron(lw["ws"][g, :TS, :TS], eye) for g in range(SGU_GROUPS)])
    pc = SGU_CHUNK - T
    ws = jnp.pad(ws, ((0, 0), (0, pc), (0, pc)))
    btab = _sgu_bias_table(lw["sgu_b"], jnp.minimum(jnp.arange(SGU_CHUNK) // DB, SGU_CHUNK - 1))
    halo = jnp.concatenate([jnp.zeros((DB, POOL_WIDTH), F32),
                            jnp.transpose(state_pool[l], (1, 0, 2)).reshape(POOL_STATE * DB, POOL_WIDTH)], axis=0)
    x1 = _merge(x, att, u, vv, xc, halo, cnt, lw["g1"], lw["w_gt"], lw["wa"], lw["wb"], lw["wc"], lw["wout"],
                ws.astype(mxu), btab, lw["wpool"], lw["pscale"],
                tm=T, row_stride=DB, tiles_per_seq=1, halo_rows=halo.shape[0], halo_from_xc=False, cnt_tiles=1)
    dff = lw["w_ffa"].shape[1]
    ff_halo = jnp.transpose(state_ffn[l], (1, 0, 2)).reshape((CONV_W - 1) * DB, dff)
    tail = (CONV_W - 1) * DB
    x2, tails = _ffn(x1, lw["g2"], lw["w_ffa"], lw["w_ffu"], lw["cw"], lw["cb"], lw["w_ffd"], ff_halo,
                     tm=T, row_stride=DB, tiles_per_seq=1, use_carry=False, tail=tail)
    ff_state = jnp.transpose(tails.reshape(CONV_W - 1, DB, dff), (1, 0, 2))
    pool_state = jnp.concatenate([state_pool[l], bm(xc)], axis=1)[:, -POOL_STATE:, :]
    hd = lambda a: bm(a).reshape(DB, TS, N_HEADS, HEAD_DIM)
    return x2, (hd(k), hd(v), bm(ikw)[:, :, :IDX_DIM], bm(vv), pool_state, ff_state)


def kernel(x_prompt, x_sample, cache_k, cache_v, cache_idx_k, state_pool, state_ffn_conv, page_table, norm1_g, w_in, q_norm_g, k_norm_g, sgu_w, sgu_b, sgu_norm_g, pool_w, pool_scale, w_br_a, w_br_b, w_br_c, w_out, norm2_g, w_ff_in, ff_conv_w, ff_conv_b, w_ff_down):
    weights = (norm1_g, w_in, q_norm_g, k_norm_g, sgu_w, sgu_b, sgu_norm_g, pool_w, pool_scale,
               w_br_a, w_br_b, w_br_c, w_out, norm2_g, w_ff_in, ff_conv_w, ff_conv_b, w_ff_down)
    B, S, D = x_prompt.shape
    DB, TS, _ = x_sample.shape
    depth = w_in.shape[0]
    page = cache_idx_k.shape[2]
    past = page_table.shape[1] * page
    pool = cache_k.shape[1]
    cache_kt = jnp.transpose(cache_k, (0, 1, 3, 4, 2)).reshape(depth, pool, A_WIDTH, page)
    cache_vt = jnp.transpose(cache_v, (0, 1, 3, 4, 2)).reshape(depth, pool, A_WIDTH, page)
    cache_ikt = jnp.transpose(cache_idx_k, (0, 1, 3, 2))
    assert S % 512 == 0 and S >= POOL_STATE and TS * DB == SGU_CHUNK and TS >= CONV_W - 1 and TS <= page
    assert past % SGU_CHUNK == 0 and page_table.shape[1] % PAGES_PER_STEP == 0

    pos_p = jnp.arange(S)
    pos_s = past + jnp.repeat(jnp.arange(TS), DB)
    tabs_p, tabs_s = _rope_tables(pos_p), _rope_tables(pos_s)
    cnt_p, cnt_s = _pool_counts(pos_p), _pool_counts(pos_s)

    xp = x_prompt.reshape(B * S, D)
    xs = jnp.transpose(x_sample, (1, 0, 2)).reshape(TS * DB, D)
    outs_p, outs_s = [], []
    for l in range(depth):
        lw = _layer_weights(l, *weights)
        xp, st_p = _prompt_layer(xp, lw, tabs_p, cnt_p, B=B, S=S)
        xs, st_s = _sample_layer(xs, lw, tabs_s, cnt_s, l, cache_kt, cache_vt, cache_ikt, state_pool,
                                 state_ffn_conv, page_table, DB=DB, TS=TS)
        outs_p.append(st_p)
        outs_s.append(st_s)
    stack = lambda outs, i: jnp.stack([o[i] for o in outs])
    y_p = xp.reshape(B, S, D)
    y_s = jnp.transpose(xs.reshape(TS, DB, D), (1, 0, 2))
    return (y_p, y_s,
            stack(outs_p, 0), stack(outs_p, 1), stack(outs_p, 2), stack(outs_p, 3), stack(outs_p, 4),
            stack(outs_s, 0), stack(outs_s, 1), stack(outs_s, 2), stack(outs_s, 3), stack(outs_s, 4), stack(outs_s, 5))
```

```python
import functools

import jax
import jax.numpy as jnp
import numpy as np
from jax import lax
from jax.experimental import pallas as pl
from jax.experimental.pallas import tpu as pltpu

MXU_DTYPE = jnp.bfloat16
F32 = jnp.float32
I32 = jnp.int32

N_HEADS = 8
HEAD_DIM = 64
A_WIDTH = N_HEADS * HEAD_DIM
IDX_HEADS = 8
IDX_DIM = 64
IDX_ROPE = 32
TOPK_MAX = 256
SGU_GROUPS = 4
SGU_WIDTH = 256
SGU_CHUNK = 128
POOL_GROUPS = 4
POOL_WIDTH = 256
POOL_WINDOWS = (2, 4, 8, 16)
POOL_STATE = 15
CONV_W = 3
ROPE_THETA = 10000.0
EPS = 1e-6

LANES = 128
SUBLANES = 8
INT_MIN = -2 ** 31
NEG_BIG = -1e30
Q_SCALE = HEAD_DIM ** -0.5 * float(np.log2(np.e))
PROJ_COLS = 3072
VMEM_LIMIT = 56 * 1024 * 1024

_NT = (((1,), (1,)), ((), ()))


def _cparams(n_axes):
    return pltpu.CompilerParams(dimension_semantics=("arbitrary",) * n_axes, vmem_limit_bytes=VMEM_LIMIT)


def _const_spec(shape):
    nd = len(shape)
    return pl.BlockSpec(shape, lambda *_: (0,) * nd)


def _rope128(x, cos, sin, half):
    lane = lax.broadcasted_iota(I32, x.shape, 1)
    lo = (lane & (HEAD_DIM - 1)) < half
    rot = jnp.where(lo, pltpu.roll(x, LANES - half, 1), pltpu.roll(x, half, 1))
    return x * cos + rot * sin


def _head_rms(x, seg, g):
    ss = jnp.dot((x * x).astype(seg.dtype), seg, preferred_element_type=F32)
    return x * lax.rsqrt(ss * (1.0 / HEAD_DIM) + EPS) * g


def _rms(x, g):
    return x * lax.rsqrt(jnp.mean(x * x, axis=-1, keepdims=True) + EPS) * g


def _proj_kernel(x_ref, g1_ref, w_ref, qg_ref, kg_ref, sgg_ref, tab_ref, seg_ref,
                 qb_ref, k_ref, kb_ref, v_ref, vt_ref, iqb_ref, ikw_ref, ikd_ref, u_ref, vv_ref, xc_ref):
    mxu = w_ref.dtype
    xn = _rms(x_ref[...], g1_ref[...]).astype(mxu)

    def mm(c0, c1):
        return jnp.dot(xn, w_ref[:, c0:c1], preferred_element_type=F32)

    cqk, sqk = tab_ref[:, 0:128], tab_ref[:, 128:256]
    ciq, siq = tab_ref[:, 256:384], tab_ref[:, 384:512]
    cikw, sikw = tab_ref[:, 512:640], tab_ref[:, 640:768]
    seg = seg_ref[...]
    half_qk = HEAD_DIM // 2
    half_idx = IDX_ROPE // 2

    q = _head_rms(mm(0, 512), seg, qg_ref[...])
    for c in range(4):
        sl = slice(LANES * c, LANES * (c + 1))
        qb_ref[:, sl] = (_rope128(q[:, sl], cqk, sqk, half_qk) * Q_SCALE).astype(mxu)
    k = _head_rms(mm(512, 1024), seg, kg_ref[...])
    for c in range(4):
        sl = slice(LANES * c, LANES * (c + 1))
        kr = _rope128(k[:, sl], cqk, sqk, half_qk)
        k_ref[:, sl] = kr
        kb_ref[:, sl] = kr.astype(mxu)
    v = mm(1024, 1536)
    v_ref[...] = v
    vt_ref[...] = jnp.transpose(v).astype(mxu)
    iq = mm(1536, 2048)
    for c in range(4):
        sl = slice(LANES * c, LANES * (c + 1))
        iqb_ref[:, sl] = (_rope128(iq[:, sl], ciq, siq, half_idx) * (IDX_DIM ** -0.5)).astype(mxu)
    ikw_ref[...] = _rope128(mm(2048, 2176), cikw, sikw, half_idx)
    ikd_ref[...] = _rope128(mm(2176, 2304), ciq, siq, half_idx).astype(mxu)
    sg = jax.nn.gelu(mm(2304, 2816))
    u_ref[...] = sg[:, :SGU_WIDTH]
    vv_ref[...] = _rms(sg[:, SGU_WIDTH:], sgg_ref[...])
    xc_ref[...] = mm(2816, 3072)


def _proj(x, g1, w, qg, kg, sgg, tab, seg, *, tm, tab_tiles):
    T, D = x.shape
    mxu = w.dtype
    n = T // tm
    row = lambda width: pl.BlockSpec((tm, width), lambda i: (i, 0))
    out_shape = [
        jax.ShapeDtypeStruct((T, A_WIDTH), mxu),
        jax.ShapeDtypeStruct((T, A_WIDTH), F32),
        jax.ShapeDtypeStruct((T, A_WIDTH), mxu),
        jax.ShapeDtypeStruct((T, A_WIDTH), F32),
        jax.ShapeDtypeStruct((A_WIDTH, T), mxu),
        jax.ShapeDtypeStruct((T, IDX_HEADS * IDX_DIM), mxu),
        jax.ShapeDtypeStruct((T, LANES), F32),
        jax.ShapeDtypeStruct((T, LANES), mxu),
        jax.ShapeDtypeStruct((T, SGU_WIDTH), F32),
        jax.ShapeDtypeStruct((T, SGU_WIDTH), F32),
        jax.ShapeDtypeStruct((T, POOL_WIDTH), F32),
    ]
    out_specs = [row(s.shape[1]) for s in out_shape]
    out_specs[4] = pl.BlockSpec((A_WIDTH, tm), lambda i: (0, i))
    in_specs = [
        row(D), _const_spec(g1.shape), _const_spec(w.shape), _const_spec(qg.shape), _const_spec(kg.shape),
        _const_spec(sgg.shape),
        pl.BlockSpec((tm, tab.shape[1]), lambda i: (i % tab_tiles, 0)),
        _const_spec(seg.shape),
    ]
    return pl.pallas_call(
        _proj_kernel, grid=(n,), in_specs=in_specs, out_specs=out_specs, out_shape=out_shape,
        compiler_params=_cparams(1), name="proj",
    )(x, g1, w, qg, kg, sgg, tab, seg)


def _sortable_key(s):
    b = lax.bitcast_convert_type(s, I32)
    key = jnp.where(b < 0, b ^ jnp.int32(0x7FFFFFFF), b)
    return jnp.where(key == -1, 0, key)


def _fold_lanes(m):
    out = m[:, 0:LANES]
    for c in range(1, m.shape[1] // LANES):
        out = out + m[:, c * LANES:(c + 1) * LANES]
    return out


def _fold_sublanes(m):
    out = m[0:SUBLANES]
    for r in range(1, m.shape[0] // SUBLANES):
        out = out + m[r * SUBLANES:(r + 1) * SUBLANES]
    return out


def _topk_threshold(kint_ref, thr_ref, cut_ref, n_tiles, tile, rows, topk, n_cols, keys_axis=1):
    stat_shape = (rows, 1) if keys_axis == 1 else (1, rows)

    def count(pred):
        def body(j, c):
            off = pl.multiple_of(j * tile, tile)
            if keys_axis == 1:
                key = kint_ref[:, pl.ds(off, tile)]
                idx = off + lax.broadcasted_iota(I32, (rows, tile), 1)
                return c + _fold_lanes(pred(key, idx).astype(F32))
            key = kint_ref[pl.ds(off, tile), :]
            idx = off + lax.broadcasted_iota(I32, (tile, rows), 0)
            return c + _fold_sublanes(pred(key, idx).astype(F32))
        init = jnp.zeros((rows, LANES) if keys_axis == 1 else (SUBLANES, rows), F32)
        part = lax.fori_loop(0, n_tiles, body, init)
        return jnp.sum(part, axis=keys_axis, keepdims=True)

    kf = float(topk)
    c0 = count(lambda key, col: key >= 0)
    t = jnp.where(c0 >= kf, jnp.int32(0), jnp.int32(INT_MIN))

    def bit_body(b, t):
        cand = t + jnp.left_shift(jnp.int32(1), 30 - b)
        cnt = count(lambda key, col: key >= cand)
        return jnp.where(cnt >= kf, cand, t)

    t = lax.fori_loop(0, 31, bit_body, t)
    t = jnp.maximum(t, jnp.int32(INT_MIN + 1))
    cge = count(lambda key, col: key >= t)
    thr_ref[...] = t
    cut_ref[...] = jnp.full(stat_shape, n_cols, I32)

    @pl.when(jnp.max(cge) > kf)
    def _():
        need = kf - count(lambda key, col: key > t)
        nbits = max(1, (n_cols - 1).bit_length())

        def idx_body(b, p):
            cand = p + jnp.left_shift(jnp.int32(1), nbits - 1 - b)
            cnt = count(lambda key, col: (key == t) & (col < cand))
            return jnp.where(cnt < need, cand, p)

        p = lax.fori_loop(0, nbits, idx_body, jnp.zeros(stat_shape, I32))
        cut_ref[...] = jnp.where(cge > kf, p, jnp.int32(n_cols))


def _selected(key, col, t, cut):
    return (key > t) | ((key == t) & (col <= cut))


KEY_TILE = 128


def _dsa_prompt_kernel(qb_ref, iqb_ref, ikw_ref, kb_ref, vt_ref, ikd_ref, o_ref,
                       kint_ref, qm_ref, iqm_ref, w_ref, s0_ref, s1_ref, p0_ref, p1_ref, a0_ref, a1_ref,
                       acc_ref, m_ref, l_ref, thr_ref, cut_ref, *, topk, qb, seq):
    i = pl.program_id(1)
    kt = KEY_TILE
    assert qb == 2 * kt
    n_tiles = 2 * (i + 1)
    last = n_tiles - 1
    s_slots, p_slots, a_slots = (s0_ref, s1_ref), (p0_ref, p1_ref), (a0_ref, a1_ref)
    lane = lax.broadcasted_iota(I32, (qb, LANES), 1)
    lo = lane < HEAD_DIM
    for h in range(N_HEADS):
        sl = slice(LANES * (h // 2), LANES * (h // 2 + 1))
        msk = lo if h % 2 == 0 else jnp.logical_not(lo)
        qm_ref[h] = jnp.where(msk, qb_ref[0, :, sl], jnp.zeros((), qb_ref.dtype))
        iqm_ref[h] = jnp.where(msk, iqb_ref[0, :, sl], jnp.zeros((), iqb_ref.dtype))
    w_ref[...] = jnp.transpose(ikw_ref[0])[IDX_DIM:IDX_DIM + IDX_HEADS, :]
    qpos = i * qb + lax.broadcasted_iota(I32, (1, qb), 1)

    def tile_off(tile):
        return pl.multiple_of(jnp.clip(tile, 0, last) * kt, kt)

    def idx_dots(tile, s_out):
        ik = ikd_ref[0, pl.ds(tile_off(tile), kt), :]
        for h in range(IDX_HEADS):
            s_out[h] = lax.dot_general(ik, iqm_ref[h], _NT, preferred_element_type=F32)

    def combine(tile, s_in):
        sc = jnp.zeros((kt, qb), F32)
        for h in range(IDX_HEADS):
            sc = sc + jnp.maximum(s_in[h], 0.0) * w_ref[h:h + 1, :]
        off = tile_off(tile)
        kpos = off + lax.broadcasted_iota(I32, (kt, qb), 0)
        kint_ref[pl.ds(off, kt), :] = jnp.where(kpos <= qpos, _sortable_key(sc), jnp.int32(INT_MIN))

    idx_dots(0, s0_ref)

    def score_pair(jj, carry):
        for step in range(2):
            tile = 2 * jj + step
            combine(tile, s_slots[step])
            idx_dots(tile + 1, s_slots[1 - step])
        return carry

    lax.fori_loop(0, i + 1, score_pair, 0)
    _topk_threshold(kint_ref, thr_ref, cut_ref, i + 1, qb, qb, topk, seq, keys_axis=0)
    thr = thr_ref[...]
    cut = cut_ref[...]

    acc_ref[...] = jnp.zeros_like(acc_ref)
    m_ref[...] = jnp.full(m_ref.shape, NEG_BIG, F32)
    l_ref[...] = jnp.zeros_like(l_ref)
    p1_ref[...] = jnp.zeros_like(p1_ref)
    a1_ref[...] = jnp.ones_like(a1_ref)

    def qk_dots(tile, s_out):
        off = tile_off(tile)
        for h in range(N_HEADS):
            kk = kb_ref[0, pl.ds(off, kt), LANES * (h // 2):LANES * (h // 2 + 1)]
            s_out[h] = lax.dot_general(kk, qm_ref[h], _NT, preferred_element_type=F32)

    def softmax(tile, s_in, p_out, a_out):
        off = tile_off(tile)
        live = tile <= last
        kpos = off + lax.broadcasted_iota(I32, (kt, qb), 0)
        sel = _selected(kint_ref[pl.ds(off, kt), :], kpos,
                        jnp.where(live, thr, jnp.int32(2 ** 31 - 1)), jnp.where(live, cut, jnp.int32(-1)))
        bias = jnp.where(sel, 0.0, NEG_BIG)
        for h in range(N_HEADS):
            s = s_in[h] + bias
            m_old = m_ref[h]
            m_new = jnp.maximum(m_old, jnp.max(s, axis=0, keepdims=True))
            alpha = jnp.exp2(m_old - m_new)
            pm = jnp.exp2(s - m_new)
            l_ref[h] = alpha * l_ref[h] + jnp.sum(pm, axis=0, keepdims=True)
            m_ref[h] = m_new
            p_out[h] = pm.astype(p_out.dtype)
            a_out[h] = alpha

    def pv_dots(tile, p_in, a_in):
        off = tile_off(tile)
        for h in range(N_HEADS):
            dr = slice(HEAD_DIM * h, HEAD_DIM * (h + 1))
            pv = jnp.dot(vt_ref[dr, pl.ds(off, kt)], p_in[h], preferred_element_type=F32)
            acc_ref[dr, :] = a_in[h] * acc_ref[dr, :] + pv

    qk_dots(0, s0_ref)

    def att_pair(jj, carry):
        for step in range(2):
            tile = 2 * jj + step
            softmax(tile, s_slots[step], p_slots[step], a_slots[step])
            qk_dots(tile + 1, s_slots[1 - step])
            pv_dots(tile - 1, p_slots[1 - step], a_slots[1 - step])
        return carry

    lax.fori_loop(0, i + 2, att_pair, 0)
    for h in range(N_HEADS):
        dr = slice(HEAD_DIM * h, HEAD_DIM * (h + 1))
        acc_ref[dr, :] = acc_ref[dr, :] / l_ref[h]
    o_ref[0] = jnp.transpose(acc_ref[...]).astype(o_ref.dtype)


def _dsa_prompt(qb_, iqb, ikw, kb, vt, ikd, *, qb, topk):
    B, S, _ = qb_.shape
    mxu = qb_.dtype
    blk = lambda width: pl.BlockSpec((1, qb, width), lambda b, i: (b, i, 0))
    full = lambda width: pl.BlockSpec((1, S, width), lambda b, i: (b, 0, 0))
    kern = functools.partial(_dsa_prompt_kernel, topk=topk, qb=qb, seq=S)
    return pl.pallas_call(
        kern, grid=(B, S // qb),
        in_specs=[blk(A_WIDTH), blk(A_WIDTH), blk(LANES), full(A_WIDTH),
                  pl.BlockSpec((A_WIDTH, S), lambda b, i: (0, b)), full(LANES)],
        out_specs=blk(A_WIDTH),
        out_shape=jax.ShapeDtypeStruct((B, S, A_WIDTH), mxu),
        scratch_shapes=[
            pltpu.VMEM((S, qb), I32),
            pltpu.VMEM((N_HEADS, qb, LANES), mxu),
            pltpu.VMEM((N_HEADS, qb, LANES), mxu),
            pltpu.VMEM((IDX_HEADS, qb), F32),
            pltpu.VMEM((N_HEADS, KEY_TILE, qb), F32),
            pltpu.VMEM((N_HEADS, KEY_TILE, qb), F32),
            pltpu.VMEM((N_HEADS, KEY_TILE, qb), mxu),
            pltpu.VMEM((N_HEADS, KEY_TILE, qb), mxu),
            pltpu.VMEM((N_HEADS, 1, qb), F32),
            pltpu.VMEM((N_HEADS, 1, qb), F32),
            pltpu.VMEM((A_WIDTH, qb), F32),
            pltpu.VMEM((N_HEADS, 1, qb), F32),
            pltpu.VMEM((N_HEADS, 1, qb), F32),
            pltpu.VMEM((1, qb), I32),
            pltpu.VMEM((1, qb), I32),
        ],
        compiler_params=_cparams(2), name="dsa_prompt",
    )(qb_, iqb, ikw, kb, vt, ikd)


PAGES_PER_STEP = 8


def _head_sum(s):
    return jnp.concatenate(
        [jnp.sum(s[IDX_HEADS * t:IDX_HEADS * (t + 1)], axis=0, keepdims=True) for t in range(s.shape[0] // IDX_HEADS)],
        axis=0)


def _sample_scores_kernel(pt_ref, iq_ref, w_ref, ikn_ref, *rest, n_past_steps, page):
    pages = rest[:PAGES_PER_STEP]
    sc_ref = rest[PAGES_PER_STEP]
    c = pl.program_id(1)
    iq = iq_ref[0]
    w = w_ref[0]
    nt = iq.shape[0] // IDX_HEADS

    def scores(ik):
        s = jnp.dot(iq, ik.astype(iq.dtype), preferred_element_type=F32)
        return _head_sum(jnp.maximum(s, 0.0) * w)

    @pl.when(c < n_past_steps)
    def _():
        for r in range(PAGES_PER_STEP):
            sc_ref[0, :, r * page:(r + 1) * page] = scores(pages[r][...])

    @pl.when(c == n_past_steps)
    def _():
        s = scores(ikn_ref[0])
        col = lax.broadcasted_iota(I32, s.shape, 1)
        row = lax.broadcasted_iota(I32, s.shape, 0)
        sc_ref[0, :, 0:page] = jnp.where(col <= row, s, -jnp.inf)
        sc_ref[0, :, page:] = jnp.full((nt, (PAGES_PER_STEP - 1) * page), -jnp.inf, F32)


def _page_specs(n_past_steps, layer, rows, page):
    def make(r):
        def imap(b, c, pt):
            return (layer, pt[b, jnp.minimum(c, n_past_steps - 1) * PAGES_PER_STEP + r], 0, 0)
        return pl.BlockSpec((None, None, rows, page), imap)
    return [make(r) for r in range(PAGES_PER_STEP)]


def _sample_scores(page_table, iq_rows, w_rows, ik_new, cache_ik, *, layer):
    DB, n_pages = page_table.shape
    page = cache_ik.shape[3]
    n_past_steps = n_pages // PAGES_PER_STEP
    rows = iq_rows.shape[1]
    nt = rows // IDX_HEADS
    step_cols = PAGES_PER_STEP * page
    grid_spec = pltpu.PrefetchScalarGridSpec(
        num_scalar_prefetch=1, grid=(DB, n_past_steps + 1),
        in_specs=[
            pl.BlockSpec((1, rows, IDX_DIM), lambda b, c, pt: (b, 0, 0)),
            pl.BlockSpec((1, rows, 1), lambda b, c, pt: (b, 0, 0)),
            pl.BlockSpec((1, IDX_DIM, page), lambda b, c, pt: (b, 0, 0)),
        ] + _page_specs(n_past_steps, layer, IDX_DIM, page),
        out_specs=pl.BlockSpec((1, nt, step_cols), lambda b, c, pt: (b, 0, c)),
    )
    kern = functools.partial(_sample_scores_kernel, n_past_steps=n_past_steps, page=page)
    return pl.pallas_call(
        kern, grid_spec=grid_spec,
        out_shape=jax.ShapeDtypeStruct((DB, nt, (n_past_steps + 1) * step_cols), F32),
        compiler_params=_cparams(2), name="sample_scores",
    )(page_table, iq_rows, w_rows, ik_new, *([cache_ik] * PAGES_PER_STEP))


def _sample_threshold_kernel(sc_ref, thr_ref, cut_ref, kint_ref, t_ref, c_ref, *, topk, tile):
    rows, cols = sc_ref.shape
    kint_ref[...] = jnp.where(sc_ref[...] == -jnp.inf, jnp.int32(INT_MIN), _sortable_key(sc_ref[...]))
    _topk_threshold(kint_ref, t_ref, c_ref, cols // tile, tile, rows, topk, cols)
    thr_ref[...] = jnp.broadcast_to(t_ref[...], thr_ref.shape)
    cut_ref[...] = jnp.broadcast_to(c_ref[...], cut_ref.shape)


def _sample_threshold(sc, *, topk):
    rows, cols = sc.shape
    kern = functools.partial(_sample_threshold_kernel, topk=topk, tile=2 * LANES)
    return pl.pallas_call(
        kern, grid=(1,),
        in_specs=[_const_spec(sc.shape)],
        out_specs=[_const_spec((rows, LANES)), _const_spec((rows, LANES))],
        out_shape=[jax.ShapeDtypeStruct((rows, LANES), I32)] * 2,
        scratch_shapes=[pltpu.VMEM((rows, cols), I32), pltpu.VMEM((rows, 1), I32), pltpu.VMEM((rows, 1), I32)],
        compiler_params=_cparams(1), name="sample_threshold",
    )(sc)


def _sample_attn_kernel(pt_ref, q_ref, sc_ref, thr_ref, cut_ref, kn_ref, vn_ref, hm_ref, *rest, n_past_steps, page):
    kp = rest[:PAGES_PER_STEP]
    vp = rest[PAGES_PER_STEP:2 * PAGES_PER_STEP]
    o_ref = rest[2 * PAGES_PER_STEP]
    m_ref, l_ref, acc_ref = rest[2 * PAGES_PER_STEP + 1:]
    c = pl.program_id(1)
    q = q_ref[0]
    mxu = q.dtype
    nt = sc_ref.shape[1]
    step_cols = PAGES_PER_STEP * page

    @pl.when(c == 0)
    def _():
        m_ref[...] = jnp.full(m_ref.shape, NEG_BIG, F32)
        l_ref[...] = jnp.zeros_like(l_ref)
        acc_ref[...] = jnp.zeros_like(acc_ref)

    def rows_th(x):
        return jnp.concatenate([jnp.broadcast_to(x[t:t + 1], (N_HEADS, x.shape[1])) for t in range(nt)], axis=0)

    def update(keys, vals, n_cols):
        sc = sc_ref[0, :, 0:n_cols]
        key = jnp.where(sc == -jnp.inf, jnp.int32(INT_MIN), _sortable_key(sc))
        col = c * step_cols + lax.broadcasted_iota(I32, key.shape, 1)
        sel = rows_th(_selected(key, col, thr_ref[0, :, 0:1], cut_ref[0, :, 0:1]).astype(F32)) > 0.5
        s = jnp.concatenate([jnp.dot(q, kk.astype(mxu), preferred_element_type=F32) for kk in keys], axis=1)
        s = jnp.where(sel, s, NEG_BIG)
        m_old = m_ref[...]
        m_new = jnp.maximum(m_old, jnp.max(s, axis=1, keepdims=True))
        alpha = jnp.exp2(m_old - m_new)
        pm = jnp.exp2(s - m_new)
        l_ref[...] = alpha * l_ref[...] + jnp.sum(pm, axis=1, keepdims=True)
        m_ref[...] = m_new
        pv = jnp.zeros(acc_ref.shape, F32)
        for r, vv in enumerate(vals):
            pv = pv + lax.dot_general(pm[:, r * page:(r + 1) * page].astype(mxu), vv.astype(mxu), _NT,
                                      preferred_element_type=F32)
        acc_ref[...] = alpha * acc_ref[...] + pv

    @pl.when(c < n_past_steps)
    def _():
        update([r[...] for r in kp], [r[...] for r in vp], step_cols)

    @pl.when(c == n_past_steps)
    def _():
        update([kn_ref[0]], [vn_ref[0]], page)
        out = acc_ref[...] / l_ref[...] * hm_ref[...]
        o_ref[0] = _head_sum(out).astype(o_ref.dtype)


def _sample_attn(page_table, q_rows, sc, thr, cut, k_new, v_new, head_mask, cache_k, cache_v, *, layer):
    DB, n_pages = page_table.shape
    page = cache_k.shape[3]
    n_past_steps = n_pages // PAGES_PER_STEP
    rows = q_rows.shape[1]
    nt = rows // N_HEADS
    step_cols = PAGES_PER_STEP * page
    per_b = lambda shape: pl.BlockSpec((1,) + shape, lambda b, c, pt: (b, 0, 0))
    grid_spec = pltpu.PrefetchScalarGridSpec(
        num_scalar_prefetch=1, grid=(DB, n_past_steps + 1),
        in_specs=[
            per_b((rows, A_WIDTH)),
            pl.BlockSpec((1, nt, step_cols), lambda b, c, pt: (b, 0, c)),
            per_b((nt, LANES)), per_b((nt, LANES)),
            per_b((A_WIDTH, page)), per_b((A_WIDTH, page)),
            pl.BlockSpec((rows, A_WIDTH), lambda b, c, pt: (0, 0)),
        ] + _page_specs(n_past_steps, layer, A_WIDTH, page) * 2,
        out_specs=per_b((nt, A_WIDTH)),
        scratch_shapes=[pltpu.VMEM((rows, 1), F32), pltpu.VMEM((rows, 1), F32), pltpu.VMEM((rows, A_WIDTH), F32)],
    )
    kern = functools.partial(_sample_attn_kernel, n_past_steps=n_past_steps, page=page)
    return pl.pallas_call(
        kern, grid_spec=grid_spec,
        out_shape=jax.ShapeDtypeStruct((DB, nt, A_WIDTH), q_rows.dtype),
        compiler_params=_cparams(2), name="sample_attn",
    )(page_table, q_rows, sc, thr, cut, k_new, v_new, head_mask,
      *([cache_k] * PAGES_PER_STEP), *([cache_v] * PAGES_PER_STEP))


def _shift_rows(e, k):
    return pltpu.roll(e, k, 0)


def _merge_kernel(x_ref, att_ref, u_ref, vv_ref, xc_ref, halo_ref, cnt_ref,
                  g1_ref, wgt_ref, wa_ref, wb_ref, wc_ref, wout_ref, ws_ref, btab_ref, wpool_ref, pscale_ref,
                  o_ref, *, row_stride, tiles_per_seq, zero_first_halo):
    mxu = wgt_ref.dtype
    tm = x_ref.shape[0]
    x = x_ref[...]
    xn = _rms(x, g1_ref[...]).astype(mxu)
    d = x.shape[1]

    def gate(n):
        return jax.nn.sigmoid(jnp.dot(xn, wgt_ref[:, n * d:(n + 1) * d], preferred_element_type=F32))

    merged = gate(0) * jnp.dot(att_ref[...], wa_ref[...], preferred_element_type=F32)

    lane = lax.broadcasted_iota(I32, (SGU_CHUNK, LANES), 1)
    lo = lane < (SGU_WIDTH // SGU_GROUPS)
    sgo = []
    for c in range(tm // SGU_CHUNK):
        rs = slice(c * SGU_CHUNK, (c + 1) * SGU_CHUNK)
        vvb = vv_ref[rs, :].astype(mxu)
        mix = []
        for p in range(SGU_GROUPS // 2):
            pair = vvb[:, LANES * p:LANES * (p + 1)]
            r0 = jnp.dot(ws_ref[2 * p], pair, preferred_element_type=F32)
            r1 = jnp.dot(ws_ref[2 * p + 1], pair, preferred_element_type=F32)
            mix.append(jnp.where(lo, r0, r1))
        sgo.append(u_ref[rs, :] * (jnp.concatenate(mix, axis=1) + btab_ref[...]))
    sgo = jnp.concatenate(sgo, axis=0).astype(mxu)
    merged = merged + gate(1) * jnp.dot(sgo, wb_ref[...], preferred_element_type=F32)

    xc = xc_ref[...]
    halo = halo_ref[...]
    if zero_first_halo:
        first = (pl.program_id(0) % tiles_per_seq) == 0
        halo = jnp.where(first, jnp.zeros_like(halo), halo)
    hp = halo.shape[0]
    e = jnp.concatenate([halo, xc], axis=0)
    s1 = e + _shift_rows(e, row_stride)
    s2 = s1 + _shift_rows(s1, 2 * row_stride)
    s3 = s2 + _shift_rows(s2, 4 * row_stride)
    s4 = s3 + _shift_rows(s3, 8 * row_stride)
    gd = POOL_WIDTH // POOL_GROUPS
    wsum = jnp.concatenate([s[hp:, g * gd:(g + 1) * gd] for g, s in enumerate((s1, s2, s3, s4))], axis=1)
    pooled = (wsum / cnt_ref[...] - xc).astype(mxu)
    po = (jnp.dot(pooled, wpool_ref[...], preferred_element_type=F32) * pscale_ref[...]).astype(mxu)
    merged = merged + gate(2) * jnp.dot(po, wc_ref[...], preferred_element_type=F32)

    o_ref[...] = x + jnp.dot(merged.astype(mxu), wout_ref[...], preferred_element_type=F32)


def _merge(x, att, u, vv, xc, halo, cnt, g1, wgt, wa, wb, wc, wout, ws, btab, wpool, pscale,
           *, tm, row_stride, tiles_per_seq, halo_rows, halo_from_xc, cnt_tiles):
    T, D = x.shape
    n = T // tm
    row = lambda width: pl.BlockSpec((tm, width), lambda i: (i, 0))
    if halo_from_xc:
        per = tm // halo_rows
        halo_spec = pl.BlockSpec((halo_rows, xc.shape[1]), lambda i: (jnp.maximum(i * per - 1, 0), 0))
    else:
        halo_spec = _const_spec(halo.shape)
    consts = [g1, wgt, wa, wb, wc, wout, ws, btab, wpool, pscale]
    kern = functools.partial(_merge_kernel, row_stride=row_stride, tiles_per_seq=tiles_per_seq,
                             zero_first_halo=halo_from_xc)
    return pl.pallas_call(
        kern, grid=(n,),
        in_specs=[row(D), row(att.shape[1]), row(u.shape[1]), row(vv.shape[1]), row(xc.shape[1]), halo_spec,
                  pl.BlockSpec((tm, cnt.shape[1]), lambda i: (i % cnt_tiles, 0))]
        + [_const_spec(c.shape) for c in consts],
        out_specs=row(D), out_shape=jax.ShapeDtypeStruct((T, D), F32),
        compiler_params=_cparams(1), name="merge",
    )(x, att, u, vv, xc, halo, cnt, *consts)


FFN_CHUNK = 256


def _ffn_kernel(x_ref, g2_ref, wa_ref, wu_ref, cw_ref, cb_ref, wd_ref, halo_ref, o_ref, tail_ref, carry_ref,
                *, row_stride, tiles_per_seq, use_carry):
    mxu = wa_ref.dtype
    x = x_ref[...]
    tm = x.shape[0]
    xn = _rms(x, g2_ref[...]).astype(mxu)
    dff = wa_ref.shape[1]
    tail = tail_ref.shape[1]
    acc = jnp.zeros(x.shape, F32)
    if use_carry:
        first = (pl.program_id(0) % tiles_per_seq) == 0
    for c0 in range(0, dff, FFN_CHUNK):
        cs = slice(c0, c0 + FFN_CHUNK)
        a = jnp.dot(xn, wa_ref[:, cs], preferred_element_type=F32)
        up = jnp.dot(xn, wu_ref[:, cs], preferred_element_type=F32)
        if use_carry:
            halo = jnp.where(first, 0.0, carry_ref[:, cs])
            carry_ref[:, cs] = a[tm - carry_ref.shape[0]:, :]
        else:
            halo = halo_ref[:, cs]
        hp = halo.shape[0]
        e = jnp.concatenate([halo, a], axis=0)
        p1 = _shift_rows(e, row_stride)[hp:]
        p2 = _shift_rows(e, 2 * row_stride)[hp:]
        conv = p2 * cw_ref[0:1, cs] + p1 * cw_ref[1:2, cs] + a * cw_ref[2:3, cs] + cb_ref[:, cs]
        act = (jax.nn.silu(conv) * up).astype(mxu)
        acc = acc + jnp.dot(act, wd_ref[cs, :], preferred_element_type=F32)
        tail_ref[0, :, cs] = a[tm - tail:, :]
    o_ref[...] = x + acc


def _ffn(x, g2, wa, wu, cw, cb, wd, halo, *, tm, row_stride, tiles_per_seq, use_carry, tail):
    T, D = x.shape
    n = T // tm
    dff = wa.shape[1]
    row = pl.BlockSpec((tm, D), lambda i: (i, 0))
    consts = [g2, wa, wu, cw, cb, wd, halo]
    kern = functools.partial(_ffn_kernel, row_stride=row_stride, tiles_per_seq=tiles_per_seq, use_carry=use_carry)
    return pl.pallas_call(
        kern, grid=(n,),
        in_specs=[row] + [_const_spec(c.shape) for c in consts],
        out_specs=[row, pl.BlockSpec((1, tail, dff), lambda i: (i, 0, 0))],
        out_shape=[jax.ShapeDtypeStruct((T, D), F32), jax.ShapeDtypeStruct((n, tail, dff), F32)],
        scratch_shapes=[pltpu.VMEM((SUBLANES, dff), F32)],
        compiler_params=_cparams(1), name="ffn",
    )(x, *consts)


def _pack_w_in(w_in):
    d = w_in.shape[0]
    o = 0
    q, k, v = (w_in[:, o + i * A_WIDTH:o + (i + 1) * A_WIDTH] for i in range(3))
    o += 3 * A_WIDTH
    iq = w_in[:, o:o + IDX_HEADS * IDX_DIM]
    o += IDX_HEADS * IDX_DIM
    ik = w_in[:, o:o + IDX_DIM]
    o += IDX_DIM
    iw = w_in[:, o:o + IDX_HEADS]
    o += IDX_HEADS
    sg = w_in[:, o:o + 2 * SGU_WIDTH]
    o += 2 * SGU_WIDTH
    xc = w_in[:, o:o + POOL_WIDTH]
    o += POOL_WIDTH
    gt = w_in[:, o:]
    pad = jnp.zeros((d, LANES - IDX_DIM - IDX_HEADS), w_in.dtype)
    packed = jnp.concatenate([q, k, v, iq, ik, iw, pad, ik, ik, sg, xc], axis=1)
    assert packed.shape[1] == PROJ_COLS
    return packed, gt


def _rope_tables(pos):
    posf = pos.astype(F32)[:, None]

    def cs(half):
        inv = ROPE_THETA ** (-jnp.arange(half, dtype=F32) / half)
        ang = posf * inv[None, :]
        return jnp.cos(ang), jnp.sin(ang)

    t = pos.shape[0]
    c32, s32 = cs(HEAD_DIM // 2)
    cqk = jnp.tile(jnp.concatenate([c32, c32], axis=1), (1, 2))
    sqk = jnp.tile(jnp.concatenate([-s32, s32], axis=1), (1, 2))
    c16, s16 = cs(IDX_ROPE // 2)
    rest = IDX_DIM - IDX_ROPE
    ci = jnp.concatenate([c16, c16, jnp.ones((t, rest), F32)], axis=1)
    si = jnp.concatenate([-s16, s16, jnp.zeros((t, rest), F32)], axis=1)
    ciq, siq = jnp.tile(ci, (1, 2)), jnp.tile(si, (1, 2))
    wpad = LANES - IDX_DIM - IDX_HEADS
    cikw = jnp.concatenate([ci, jnp.full((t, IDX_HEADS), IDX_HEADS ** -0.5, F32), jnp.zeros((t, wpad), F32)], axis=1)
    sikw = jnp.concatenate([si, jnp.zeros((t, LANES - IDX_DIM), F32)], axis=1)
    return jnp.concatenate([cqk, sqk, ciq, siq, cikw, sikw], axis=1)


def _block_diag(blocks):
    n = len(blocks)
    r, c = blocks[0].shape
    out = jnp.zeros((n * r, n * c), blocks[0].dtype)
    for i, b in enumerate(blocks):
        out = out.at[i * r:(i + 1) * r, i * c:(i + 1) * c].set(b)
    return out


def _pool_counts(pos):
    gd = POOL_WIDTH // POOL_GROUPS
    cols = [jnp.broadcast_to(jnp.minimum(pos + 1, w).astype(F32)[:, None], (pos.shape[0], gd)) for w in POOL_WINDOWS]
    return jnp.concatenate(cols, axis=1)


def _layer_weights(l, norm1_g, w_in, q_norm_g, k_norm_g, sgu_w, sgu_b, sgu_norm_g, pool_w, pool_scale,
                   w_br_a, w_br_b, w_br_c, w_out, norm2_g, w_ff_in, ff_conv_w, ff_conv_b, w_ff_down):
    mxu = MXU_DTYPE
    packed, gt = _pack_w_in(w_in[l])
    dff = w_ff_in.shape[2] // 2
    tril = jnp.tril(jnp.ones((SGU_CHUNK, SGU_CHUNK), bool))
    return dict(
        g1=norm1_g[l][None, :], w_proj=packed.astype(mxu), w_gt=gt.astype(mxu),
        qg=jnp.tile(q_norm_g[l], N_HEADS)[None, :], kg=jnp.tile(k_norm_g[l], N_HEADS)[None, :],
        sgg=sgu_norm_g[l][None, :],
        ws=jnp.where(tril[None], sgu_w[l], 0.0),
        sgu_b=sgu_b[l],
        wpool=_block_diag([pool_w[l, g] for g in range(POOL_GROUPS)]).astype(mxu),
        pscale=pool_scale[l][None, :],
        wa=w_br_a[l].astype(mxu), wb=w_br_b[l].astype(mxu), wc=w_br_c[l].astype(mxu), wout=w_out[l].astype(mxu),
        g2=norm2_g[l][None, :],
        w_ffa=w_ff_in[l, :, :dff].astype(mxu), w_ffu=w_ff_in[l, :, dff:].astype(mxu),
        cw=jnp.pad(ff_conv_w[l], ((0, SUBLANES - CONV_W), (0, 0))), cb=ff_conv_b[l][None, :],
        w_ffd=w_ff_down[l].astype(mxu),
    )


def _sgu_bias_table(sgu_b, t_of_row):
    gd = SGU_WIDTH // SGU_GROUPS
    return jnp.repeat(jnp.transpose(sgu_b)[t_of_row], gd, axis=1)


def _prompt_layer(x, lw, tabs, cnt, *, B, S):
    mxu = MXU_DTYPE
    T = B * S
    tm_proj, tm_merge, tm_ffn, qb = 512, 256, 512, 256
    seg = _block_diag([jnp.ones((HEAD_DIM, HEAD_DIM), F32)] * N_HEADS).astype(mxu)
    qb_, k, kb, v, vt, iqb, ikw, ikd, u, vv, xc = _proj(
        x, lw["g1"], lw["w_proj"], lw["qg"], lw["kg"], lw["sgg"], tabs, seg, tm=tm_proj, tab_tiles=S // tm_proj)
    r3 = lambda a: a.reshape(B, S, a.shape[1])
    topk = min(TOPK_MAX, S // 4)
    att = _dsa_prompt(r3(qb_), r3(iqb), r3(ikw), r3(kb), vt, r3(ikd), qb=qb, topk=topk).reshape(T, A_WIDTH)
    btab = _sgu_bias_table(lw["sgu_b"], jnp.arange(SGU_CHUNK))
    x1 = _merge(x, att, u, vv, xc, xc, cnt, lw["g1"], lw["w_gt"], lw["wa"], lw["wb"], lw["wc"], lw["wout"],
                lw["ws"].astype(mxu), btab, lw["wpool"], lw["pscale"],
                tm=tm_merge, row_stride=1, tiles_per_seq=S // tm_merge, halo_rows=16, halo_from_xc=True,
                cnt_tiles=S // tm_merge)
    dff = lw["w_ffa"].shape[1]
    x2, tails = _ffn(x1, lw["g2"], lw["w_ffa"], lw["w_ffu"], lw["cw"], lw["cb"], lw["w_ffd"],
                     jnp.zeros((SUBLANES, dff), F32),
                     tm=tm_ffn, row_stride=1, tiles_per_seq=S // tm_ffn, use_carry=True, tail=SUBLANES)
    n_t = S // tm_ffn
    ff_state = tails.reshape(B, n_t, SUBLANES, dff)[:, -1, SUBLANES - (CONV_W - 1):, :]
    pool_state = xc.reshape(B, S, POOL_WIDTH)[:, S - POOL_STATE:, :]
    return x2, (k.reshape(B, S, N_HEADS, HEAD_DIM), v.reshape(B, S, N_HEADS, HEAD_DIM),
                ikw.reshape(B, S, LANES)[:, :, :IDX_DIM], pool_state, ff_state)


def _sample_layer(x, lw, tabs, cnt, l, cache_k, cache_v, cache_ik, state_pool, state_ffn, page_table, *, DB, TS):
    mxu = MXU_DTYPE
    T = TS * DB
    page = cache_ik.shape[3]
    past = page_table.shape[1] * page
    seg = _block_diag([jnp.ones((HEAD_DIM, HEAD_DIM), F32)] * N_HEADS).astype(mxu)
    qb_, k, kb, v, vt, iqb, ikw, ikd, u, vv, xc = _proj(
        x, lw["g1"], lw["w_proj"], lw["qg"], lw["kg"], lw["sgg"], tabs, seg, tm=T, tab_tiles=1)
    bm = lambda a: jnp.transpose(a.reshape(TS, DB, a.shape[1]), (1, 0, 2))

    iq_rows = bm(iqb).reshape(DB, TS * IDX_HEADS, IDX_DIM)
    w_rows = bm(ikw)[:, :, IDX_DIM:IDX_DIM + IDX_HEADS].reshape(DB, TS * IDX_HEADS, 1)
    new_t = lambda a: jnp.pad(jnp.transpose(a, (0, 2, 1)), ((0, 0), (0, 0), (0, page - TS)))
    ik_new = new_t(bm(ikd)[:, :, :IDX_DIM])
    sc = _sample_scores(page_table, iq_rows, w_rows, ik_new, cache_ik, layer=l)
    topk = min(TOPK_MAX, (past + TS) // 4)
    thr, cut = _sample_threshold(sc.reshape(DB * TS, sc.shape[2]), topk=topk)
    hm = jnp.repeat(jnp.eye(N_HEADS, dtype=F32), HEAD_DIM, axis=1)
    q_rows = (bm(qb_)[:, :, None, :] * hm[None, None].astype(mxu)).reshape(DB, TS * N_HEADS, A_WIDTH)
    att = _sample_attn(page_table, q_rows, sc, thr.reshape(DB, TS, LANES), cut.reshape(DB, TS, LANES),
                       new_t(bm(kb)), new_t(bm(jnp.transpose(vt))), jnp.tile(hm, (TS, 1)), cache_k, cache_v, layer=l)
    att = jnp.transpose(att, (1, 0, 2)).reshape(T, A_WIDTH)

    eye = jnp.eye(DB, dtype=F32)
    ws = jnp.stack([jnp.kron(lw["ws"][g, :TS, :TS], eye) for g in range(SGU_GROUPS)])
    pc = SGU_CHUNK - T
    ws = jnp.pad(ws, ((0, 0), (0, pc), (0, pc)))
    btab = _sgu_bias_table(lw["sgu_b"], jnp.minimum(jnp.arange(SGU_CHUNK) // DB, SGU_CHUNK - 1))
    halo = jnp.concatenate([jnp.zeros((DB, POOL_WIDTH), F32),
                            jnp.transpose(state_pool[l], (1, 0, 2)).reshape(POOL_STATE * DB, POOL_WIDTH)], axis=0)
    x1 = _merge(x, att, u, vv, xc, halo, cnt, lw["g1"], lw["w_gt"], lw["wa"], lw["wb"], lw["wc"], lw["wout"],
                ws.astype(mxu), btab, lw["wpool"], lw["pscale"],
                tm=T, row_stride=DB, tiles_per_seq=1, halo_rows=halo.shape[0], halo_from_xc=False, cnt_tiles=1)
    dff = lw["w_ffa"].shape[1]
    ff_halo = jnp.transpose(state_ffn[l], (1, 0, 2)).reshape((CONV_W - 1) * DB, dff)
    tail = (CONV_W - 1) * DB
    x2, tails = _ffn(x1, lw["g2"], lw["w_ffa"], lw["w_ffu"], lw["cw"], lw["cb"], lw["w_ffd"], ff_halo,
                     tm=T, row_stride=DB, tiles_per_seq=1, use_carry=False, tail=tail)
    ff_state = jnp.transpose(tails.reshape(CONV_W - 1, DB, dff), (1, 0, 2))
    pool_state = jnp.concatenate([state_pool[l], bm(xc)], axis=1)[:, -POOL_STATE:, :]
    hd = lambda a: bm(a).reshape(DB, TS, N_HEADS, HEAD_DIM)
    return x2, (hd(k), hd(v), bm(ikw)[:, :, :IDX_DIM], bm(vv), pool_state, ff_state)


def kernel(x_prompt, x_sample, cache_k, cache_v, cache_idx_k, state_pool, state_ffn_conv, page_table, norm1_g, w_in, q_norm_g, k_norm_g, sgu_w, sgu_b, sgu_norm_g, pool_w, pool_scale, w_br_a, w_br_b, w_br_c, w_out, norm2_g, w_ff_in, ff_conv_w, ff_conv_b, w_ff_down):
    weights = (norm1_g, w_in, q_norm_g, k_norm_g, sgu_w, sgu_b, sgu_norm_g, pool_w, pool_scale,
               w_br_a, w_br_b, w_br_c, w_out, norm2_g, w_ff_in, ff_conv_w, ff_conv_b, w_ff_down)
    B, S, D = x_prompt.shape
    DB, TS, _ = x_sample.shape
    depth = w_in.shape[0]
    page = cache_idx_k.shape[2]
    past = page_table.shape[1] * page
    pool = cache_k.shape[1]
    cache_kt = jnp.transpose(cache_k, (0, 1, 3, 4, 2)).reshape(depth, pool, A_WIDTH, page)
    cache_vt = jnp.transpose(cache_v, (0, 1, 3, 4, 2)).reshape(depth, pool, A_WIDTH, page)
    cache_ikt = jnp.transpose(cache_idx_k, (0, 1, 3, 2))
    assert S % 512 == 0 and S >= POOL_STATE and TS * DB == SGU_CHUNK and TS >= CONV_W - 1 and TS <= page
    assert past % SGU_CHUNK == 0 and page_table.shape[1] % PAGES_PER_STEP == 0

    pos_p = jnp.arange(S)
    pos_s = past + jnp.repeat(jnp.arange(TS), DB)
    tabs_p, tabs_s = _rope_tables(pos_p), _rope_tables(pos_s)
    cnt_p, cnt_s = _pool_counts(pos_p), _pool_counts(pos_s)

    xp = x_prompt.reshape(B * S, D)
    xs = jnp.transpose(x_sample, (1, 0, 2)).reshape(TS * DB, D)
    outs_p, outs_s = [], []
    for l in range(depth):
        lw = _layer_weights(l, *weights)
        xp, st_p = _prompt_layer(xp, lw, tabs_p, cnt_p, B=B, S=S)
        xs, st_s = _sample_layer(xs, lw, tabs_s, cnt_s, l, cache_kt, cache_vt, cache_ikt, state_pool,
                                 state_ffn_conv, page_table, DB=DB, TS=TS)
        outs_p.append(st_p)
        outs_s.append(st_s)
    stack = lambda outs, i: jnp.stack([o[i] for o in outs])
    y_p = xp.reshape(B, S, D)
    y_s = jnp.transpose(xs.reshape(TS, DB, D), (1, 0, 2))
    return (y_p, y_s,
            stack(outs_p, 0), stack(outs_p, 1), stack(outs_p, 2), stack(outs_p, 3), stack(outs_p, 4),
            stack(outs_s, 0), stack(outs_s, 1), stack(outs_s, 2), stack(outs_s, 3), stack(outs_s, 4), stack(outs_s, 5))
```

```python
import functools

import jax
import jax.numpy as jnp
import numpy as np
from jax import lax
from jax.experimental import pallas as pl
from jax.experimental.pallas import tpu as pltpu

MXU_DTYPE = jnp.bfloat16
F32 = jnp.float32
I32 = jnp.int32

N_HEADS = 8
HEAD_DIM = 64
A_WIDTH = N_HEADS * HEAD_DIM
IDX_HEADS = 8
IDX_DIM = 64
IDX_ROPE = 32
TOPK_MAX = 256
SGU_GROUPS = 4
SGU_WIDTH = 256
SGU_CHUNK = 128
POOL_GROUPS = 4
POOL_WIDTH = 256
POOL_WINDOWS = (2, 4, 8, 16)
POOL_STATE = 15
CONV_W = 3
ROPE_THETA = 10000.0
EPS = 1e-6

LANES = 128
SUBLANES = 8
INT_MIN = -2 ** 31
NEG_BIG = -1e30
Q_SCALE = HEAD_DIM ** -0.5 * float(np.log2(np.e))
PROJ_COLS = 3072
VMEM_LIMIT = 56 * 1024 * 1024

_NT = (((1,), (1,)), ((), ()))


def _cparams(n_axes):
    return pltpu.CompilerParams(dimension_semantics=("arbitrary",) * n_axes, vmem_limit_bytes=VMEM_LIMIT)


def _const_spec(shape):
    nd = len(shape)
    return pl.BlockSpec(shape, lambda *_: (0,) * nd)


def _rope128(x, cos, sin, half):
    lane = lax.broadcasted_iota(I32, x.shape, 1)
    lo = (lane & (HEAD_DIM - 1)) < half
    rot = jnp.where(lo, pltpu.roll(x, LANES - half, 1), pltpu.roll(x, half, 1))
    return x * cos + rot * sin


def _head_rms(x, seg, g):
    ss = jnp.dot((x * x).astype(seg.dtype), seg, preferred_element_type=F32)
    return x * lax.rsqrt(ss * (1.0 / HEAD_DIM) + EPS) * g


def _rms(x, g):
    return x * lax.rsqrt(jnp.mean(x * x, axis=-1, keepdims=True) + EPS) * g


def _proj_kernel(x_ref, g1_ref, w_ref, qg_ref, kg_ref, sgg_ref, tab_ref, seg_ref, *rest, stacked, n_alias):
    rest = rest[n_alias:]
    if stacked:
        qb_ref, k_ref, kb_ref, v_ref, vt_ref, iqb_ref, ikw_ref, ik_ref, ikd_ref, u_ref, vv_ref, xc_ref = rest
    else:
        qb_ref, k_ref, kb_ref, v_ref, vt_ref, iqb_ref, ikw_ref, ikd_ref, u_ref, vv_ref, xc_ref = rest
    mxu = w_ref.dtype
    xn = _rms(x_ref[...], g1_ref[...]).astype(mxu)

    def mm(c0, c1):
        return jnp.dot(xn, w_ref[:, c0:c1], preferred_element_type=F32)

    cqk, sqk = tab_ref[:, 0:128], tab_ref[:, 128:256]
    ciq, siq = tab_ref[:, 256:384], tab_ref[:, 384:512]
    cikw, sikw = tab_ref[:, 512:640], tab_ref[:, 640:768]
    seg = seg_ref[...]
    half_qk = HEAD_DIM // 2
    half_idx = IDX_ROPE // 2

    q = _head_rms(mm(0, 512), seg, qg_ref[...])
    for c in range(4):
        sl = slice(LANES * c, LANES * (c + 1))
        qb_ref[:, sl] = (_rope128(q[:, sl], cqk, sqk, half_qk) * Q_SCALE).astype(mxu)
    k = _head_rms(mm(512, 1024), seg, kg_ref[...])
    for c in range(4):
        sl = slice(LANES * c, LANES * (c + 1))
        kr = _rope128(k[:, sl], cqk, sqk, half_qk)
        if stacked:
            k_ref[sl, :] = jnp.transpose(kr)
        else:
            k_ref[:, sl] = kr
        kb_ref[:, sl] = kr.astype(mxu)
    v = mm(1024, 1536)
    vt = jnp.transpose(v)
    v_ref[...] = vt if stacked else v
    vt_ref[...] = vt.astype(mxu)
    iq = mm(1536, 2048)
    for c in range(4):
        sl = slice(LANES * c, LANES * (c + 1))
        iqb_ref[:, sl] = (_rope128(iq[:, sl], ciq, siq, half_idx) * (IDX_DIM ** -0.5)).astype(mxu)
    ikw = _rope128(mm(2048, 2176), cikw, sikw, half_idx)
    ikw_ref[...] = ikw
    if stacked:
        ik_ref[...] = jnp.transpose(ikw)[:IDX_DIM, :]
    ikd_ref[...] = _rope128(mm(2176, 2304), ciq, siq, half_idx).astype(mxu)
    sg = jax.nn.gelu(mm(2304, 2816))
    u_ref[...] = sg[:, :SGU_WIDTH]
    vv_ref[...] = _rms(sg[:, SGU_WIDTH:], sgg_ref[...])
    xc_ref[...] = mm(2816, 3072)


def _proj(x, g1, w, qg, kg, sgg, tab, seg, *, tm, tab_tiles, stack=None):
    T, D = x.shape
    mxu = w.dtype
    n = T // tm
    row = lambda width: pl.BlockSpec((tm, width), lambda i: (i, 0))
    natural = lambda width, dt: (jax.ShapeDtypeStruct((T, width), dt), row(width))
    outs = dict(
        qb=natural(A_WIDTH, mxu),
        k=natural(A_WIDTH, F32), kb=natural(A_WIDTH, mxu), v=natural(A_WIDTH, F32),
        vt=(jax.ShapeDtypeStruct((A_WIDTH, T), mxu), pl.BlockSpec((A_WIDTH, tm), lambda i: (0, i))),
        iqb=natural(IDX_HEADS * IDX_DIM, mxu),
        ikw=natural(LANES, F32),
        ik=None,
        ikd=natural(LANES, mxu),
        u=natural(SGU_WIDTH, F32), vv=natural(SGU_WIDTH, F32), xc=natural(POOL_WIDTH, F32),
    )
    in_specs = [
        row(D), _const_spec(g1.shape), _const_spec(w.shape), _const_spec(qg.shape), _const_spec(kg.shape),
        _const_spec(sgg.shape),
        pl.BlockSpec((tm, tab.shape[1]), lambda i: (i % tab_tiles, 0)),
        _const_spec(seg.shape),
    ]
    args = [x, g1, w, qg, kg, sgg, tab, seg]
    aliases = {}
    if stack is not None:
        layer, depth, B, S, state = stack
        tps = S // tm
        state_out = lambda feat: (jax.ShapeDtypeStruct((depth, B, feat, S), F32),
                                  pl.BlockSpec((None, None, feat, tm), lambda i: (layer, i // tps, 0, i % tps)))
        outs.update(k=state_out(A_WIDTH), v=state_out(A_WIDTH), ik=state_out(IDX_DIM))
        if state is not None:
            names = list(k for k, o in outs.items() if o is not None)
            for buf, name in zip(state, ("k", "v", "ik")):
                aliases[len(args)] = names.index(name)
                args.append(buf)
                in_specs.append(pl.BlockSpec(memory_space=pl.ANY))
    outs = {k: o for k, o in outs.items() if o is not None}
    kern = functools.partial(_proj_kernel, stacked=stack is not None, n_alias=len(aliases))
    res = pl.pallas_call(
        kern, grid=(n,), in_specs=in_specs, out_specs=[o[1] for o in outs.values()],
        out_shape=[o[0] for o in outs.values()], input_output_aliases=aliases,
        compiler_params=_cparams(1), name="proj",
    )(*args)
    return dict(zip(outs.keys(), res))


def _sortable_key(s):
    b = lax.bitcast_convert_type(s, I32)
    key = jnp.where(b < 0, b ^ jnp.int32(0x7FFFFFFF), b)
    return jnp.where(key == -1, 0, key)


def _fold_lanes(m):
    out = m[:, 0:LANES]
    for c in range(1, m.shape[1] // LANES):
        out = out + m[:, c * LANES:(c + 1) * LANES]
    return out


def _fold_sublanes(m):
    out = m[0:SUBLANES]
    for r in range(1, m.shape[0] // SUBLANES):
        out = out + m[r * SUBLANES:(r + 1) * SUBLANES]
    return out


I16 = jnp.int16
PACK16 = 2 * SUBLANES


def _kth_largest_by_halves(khi_ref, klo_ref, low_ref, n_tiles, tile, rows, kf):
    def count_ge(ref, cand):
        c16 = cand.astype(I16)

        def body(j, c):
            off = pl.multiple_of(j * tile, tile)
            m = jnp.where(ref[pl.ds(off, tile), :] >= c16, jnp.int16(1), jnp.int16(0))
            out = m[0:PACK16]
            for r in range(1, tile // PACK16):
                out = out + m[r * PACK16:(r + 1) * PACK16]
            return c + out
        part = lax.fori_loop(0, n_tiles, body, jnp.zeros((PACK16, rows), I16))
        return jnp.sum(part.astype(F32), axis=0, keepdims=True)

    def bisect(ref):
        c0 = count_ge(ref, jnp.zeros((1, rows), I32))
        t = jnp.where(c0 >= kf, jnp.int32(0), jnp.int32(-32768))

        def bit_body(b, t):
            cand = t + jnp.left_shift(jnp.int32(1), 14 - b)
            return jnp.where(count_ge(ref, cand) >= kf, cand, t)

        return lax.fori_loop(0, 15, bit_body, t)

    hi = bisect(khi_ref)
    hi16 = hi.astype(I16)

    def build(j, carry):
        off = pl.multiple_of(j * tile, tile)
        kh = khi_ref[pl.ds(off, tile), :]
        low_ref[pl.ds(off, tile), :] = jnp.where(
            kh > hi16, jnp.int16(32767), jnp.where(kh == hi16, klo_ref[pl.ds(off, tile), :], jnp.int16(-32768)))
        return carry

    lax.fori_loop(0, n_tiles, build, 0)
    lo = bisect(low_ref)
    return jnp.left_shift(hi, 16) + (lo + 32768)


def _topk_threshold(kint_ref, thr_ref, cut_ref, n_tiles, tile, rows, topk, n_cols, keys_axis=1, halves=None):
    stat_shape = (rows, 1) if keys_axis == 1 else (1, rows)

    def count(pred):
        def body(j, c):
            off = pl.multiple_of(j * tile, tile)
            if keys_axis == 1:
                key = kint_ref[:, pl.ds(off, tile)]
                idx = off + lax.broadcasted_iota(I32, (rows, tile), 1)
                return c + _fold_lanes(pred(key, idx).astype(F32))
            key = kint_ref[pl.ds(off, tile), :]
            idx = off + lax.broadcasted_iota(I32, (tile, rows), 0)
            return c + _fold_sublanes(pred(key, idx).astype(F32))
        init = jnp.zeros((rows, LANES) if keys_axis == 1 else (SUBLANES, rows), F32)
        part = lax.fori_loop(0, n_tiles, body, init)
        return jnp.sum(part, axis=keys_axis, keepdims=True)

    kf = float(topk)
    if halves is None:
        c0 = count(lambda key, col: key >= 0)
        t = jnp.where(c0 >= kf, jnp.int32(0), jnp.int32(INT_MIN))

        def bit_body(b, t):
            cand = t + jnp.left_shift(jnp.int32(1), 30 - b)
            cnt = count(lambda key, col: key >= cand)
            return jnp.where(cnt >= kf, cand, t)

        t = lax.fori_loop(0, 31, bit_body, t)
    else:
        t = _kth_largest_by_halves(*halves, n_tiles, tile, rows, kf)
    t = jnp.maximum(t, jnp.int32(INT_MIN + 1))
    cge = count(lambda key, col: key >= t)
    thr_ref[...] = t
    cut_ref[...] = jnp.full(stat_shape, n_cols, I32)

    @pl.when(jnp.max(cge) > kf)
    def _():
        need = kf - count(lambda key, col: key > t)
        nbits = max(1, (n_cols - 1).bit_length())

        def idx_body(b, p):
            cand = p + jnp.left_shift(jnp.int32(1), nbits - 1 - b)
            cnt = count(lambda key, col: (key == t) & (col < cand))
            return jnp.where(cnt < need, cand, p)

        p = lax.fori_loop(0, nbits, idx_body, jnp.zeros(stat_shape, I32))
        cut_ref[...] = jnp.where(cge > kf, p, jnp.int32(n_cols))


def _selected(key, col, t, cut):
    return (key > t) | ((key == t) & (col <= cut))


KEY_TILE = 128


def _dsa_prompt_kernel(qb_ref, iqb_ref, ikw_ref, kb_ref, vt_ref, ikd_ref, o_ref,
                       kint_ref, khi_ref, klo_ref, low_ref,
                       qm_ref, iqm_ref, w_ref, s0_ref, s1_ref, p0_ref, p1_ref, a0_ref, a1_ref,
                       acc_ref, m_ref, l_ref, thr_ref, cut_ref, *, topk, qb, seq):
    i = pl.program_id(1)
    kt = KEY_TILE
    assert qb == 2 * kt
    n_tiles = 2 * (i + 1)
    last = n_tiles - 1
    s_slots, p_slots, a_slots = (s0_ref, s1_ref), (p0_ref, p1_ref), (a0_ref, a1_ref)
    lane = lax.broadcasted_iota(I32, (qb, LANES), 1)
    lo = lane < HEAD_DIM
    for h in range(N_HEADS):
        sl = slice(LANES * (h // 2), LANES * (h // 2 + 1))
        msk = lo if h % 2 == 0 else jnp.logical_not(lo)
        qm_ref[h] = jnp.where(msk, qb_ref[0, :, sl], jnp.zeros((), qb_ref.dtype))
        iqm_ref[h] = jnp.where(msk, iqb_ref[0, :, sl], jnp.zeros((), iqb_ref.dtype))
    w_ref[...] = jnp.transpose(ikw_ref[0])[IDX_DIM:IDX_DIM + IDX_HEADS, :]
    qpos = i * qb + lax.broadcasted_iota(I32, (1, qb), 1)

    def tile_off(tile):
        return pl.multiple_of(jnp.clip(tile, 0, last) * kt, kt)

    def idx_dots(tile, s_out):
        ik = ikd_ref[0, pl.ds(tile_off(tile), kt), :]
        for h in range(IDX_HEADS):
            s_out[h] = lax.dot_general(ik, iqm_ref[h], _NT, preferred_element_type=F32)

    def combine(tile, s_in):
        sc = jnp.zeros((kt, qb), F32)
        for h in range(IDX_HEADS):
            sc = sc + jnp.maximum(s_in[h], 0.0) * w_ref[h:h + 1, :]
        off = tile_off(tile)
        kpos = off + lax.broadcasted_iota(I32, (kt, qb), 0)
        key = jnp.where(kpos <= qpos, _sortable_key(sc), jnp.int32(INT_MIN))
        kint_ref[pl.ds(off, kt), :] = key
        khi_ref[pl.ds(off, kt), :] = jnp.right_shift(key, 16).astype(I16)
        klo_ref[pl.ds(off, kt), :] = ((key & 0xFFFF) - 32768).astype(I16)

    idx_dots(0, s0_ref)

    def score_pair(jj, carry):
        for step in range(2):
            tile = 2 * jj + step
            combine(tile, s_slots[step])
            idx_dots(tile + 1, s_slots[1 - step])
        return carry

    lax.fori_loop(0, i + 1, score_pair, 0)
    pad = pl.ds(pl.multiple_of((i + 1) * qb, qb), qb)
    kint_ref[pad, :] = jnp.full((qb, qb), INT_MIN, I32)
    khi_ref[pad, :] = jnp.full((qb, qb), -32768, I16)
    klo_ref[pad, :] = jnp.full((qb, qb), -32768, I16)
    _topk_threshold(kint_ref, thr_ref, cut_ref, (i + 2) // 2, 2 * qb, qb, topk, seq, keys_axis=0,
                    halves=(khi_ref, klo_ref, low_ref))
    thr = thr_ref[...]
    cut = cut_ref[...]

    acc_ref[...] = jnp.zeros_like(acc_ref)
    m_ref[...] = jnp.full(m_ref.shape, NEG_BIG, F32)
    l_ref[...] = jnp.zeros_like(l_ref)
    p1_ref[...] = jnp.zeros_like(p1_ref)
    a1_ref[...] = jnp.ones_like(a1_ref)

    def qk_dots(tile, s_out):
        off = tile_off(tile)
        for h in range(N_HEADS):
            kk = kb_ref[0, pl.ds(off, kt), LANES * (h // 2):LANES * (h // 2 + 1)]
            s_out[h] = lax.dot_general(kk, qm_ref[h], _NT, preferred_element_type=F32)

    def softmax(tile, s_in, p_out, a_out):
        off = tile_off(tile)
        live = tile <= last
        kpos = off + lax.broadcasted_iota(I32, (kt, qb), 0)
        sel = _selected(kint_ref[pl.ds(off, kt), :], kpos,
                        jnp.where(live, thr, jnp.int32(2 ** 31 - 1)), jnp.where(live, cut, jnp.int32(-1)))
        bias = jnp.where(sel, 0.0, NEG_BIG)
        for h in range(N_HEADS):
            s = s_in[h] + bias
            m_old = m_ref[h]
            m_new = jnp.maximum(m_old, jnp.max(s, axis=0, keepdims=True))
            alpha = jnp.exp2(m_old - m_new)
            pm = jnp.exp2(s - m_new)
            l_ref[h] = alpha * l_ref[h] + jnp.sum(pm, axis=0, keepdims=True)
            m_ref[h] = m_new
            p_out[h] = pm.astype(p_out.dtype)
            a_out[h] = alpha

    def pv_dots(tile, p_in, a_in):
        off = tile_off(tile)
        for h in range(N_HEADS):
            dr = slice(HEAD_DIM * h, HEAD_DIM * (h + 1))
            pv = jnp.dot(vt_ref[dr, pl.ds(off, kt)], p_in[h], preferred_element_type=F32)
            acc_ref[dr, :] = a_in[h] * acc_ref[dr, :] + pv

    qk_dots(0, s0_ref)

    def att_pair(jj, carry):
        for step in range(2):
            tile = 2 * jj + step
            softmax(tile, s_slots[step], p_slots[step], a_slots[step])
            qk_dots(tile + 1, s_slots[1 - step])
            pv_dots(tile - 1, p_slots[1 - step], a_slots[1 - step])
        return carry

    lax.fori_loop(0, i + 2, att_pair, 0)
    for h in range(N_HEADS):
        dr = slice(HEAD_DIM * h, HEAD_DIM * (h + 1))
        acc_ref[dr, :] = acc_ref[dr, :] / l_ref[h]
    o_ref[0] = jnp.transpose(acc_ref[...]).astype(o_ref.dtype)


def _dsa_prompt(qb_, iqb, ikw, kb, vt, ikd, *, qb, topk):
    B, S, _ = qb_.shape
    mxu = qb_.dtype
    blk = lambda width: pl.BlockSpec((1, qb, width), lambda b, i: (b, i, 0))
    full = lambda width: pl.BlockSpec((1, S, width), lambda b, i: (b, 0, 0))
    kern = functools.partial(_dsa_prompt_kernel, topk=topk, qb=qb, seq=S)
    return pl.pallas_call(
        kern, grid=(B, S // qb),
        in_specs=[blk(A_WIDTH), blk(A_WIDTH), blk(LANES), full(A_WIDTH),
                  pl.BlockSpec((A_WIDTH, S), lambda b, i: (0, b)), full(LANES)],
        out_specs=blk(A_WIDTH),
        out_shape=jax.ShapeDtypeStruct((B, S, A_WIDTH), mxu),
        scratch_shapes=[
            pltpu.VMEM((S + qb, qb), I32),
            pltpu.VMEM((S + qb, qb), I16), pltpu.VMEM((S + qb, qb), I16), pltpu.VMEM((S + qb, qb), I16),
            pltpu.VMEM((N_HEADS, qb, LANES), mxu),
            pltpu.VMEM((N_HEADS, qb, LANES), mxu),
            pltpu.VMEM((IDX_HEADS, qb), F32),
            pltpu.VMEM((N_HEADS, KEY_TILE, qb), F32),
            pltpu.VMEM((N_HEADS, KEY_TILE, qb), F32),
            pltpu.VMEM((N_HEADS, KEY_TILE, qb), mxu),
            pltpu.VMEM((N_HEADS, KEY_TILE, qb), mxu),
            pltpu.VMEM((N_HEADS, 1, qb), F32),
            pltpu.VMEM((N_HEADS, 1, qb), F32),
            pltpu.VMEM((A_WIDTH, qb), F32),
            pltpu.VMEM((N_HEADS, 1, qb), F32),
            pltpu.VMEM((N_HEADS, 1, qb), F32),
            pltpu.VMEM((1, qb), I32),
            pltpu.VMEM((1, qb), I32),
        ],
        compiler_params=_cparams(2), name="dsa_prompt",
    )(qb_, iqb, ikw, kb, vt, ikd)


PAGES_PER_STEP = 16


def _head_sum(s):
    return jnp.concatenate(
        [jnp.sum(s[IDX_HEADS * t:IDX_HEADS * (t + 1)], axis=0, keepdims=True) for t in range(s.shape[0] // IDX_HEADS)],
        axis=0)


def _sample_scores_kernel(pt_ref, iq_ref, w_ref, ikn_ref, *rest, n_past_steps, page):
    pages = rest[:PAGES_PER_STEP]
    sc_ref = rest[PAGES_PER_STEP]
    c = pl.program_id(1)
    iq = iq_ref[0]
    w = w_ref[0]
    nt = iq.shape[0] // IDX_HEADS

    def scores(ik):
        s = jnp.dot(iq, ik.astype(iq.dtype), preferred_element_type=F32)
        return _head_sum(jnp.maximum(s, 0.0) * w)

    @pl.when(c < n_past_steps)
    def _():
        for r in range(PAGES_PER_STEP):
            sc_ref[0, :, r * page:(r + 1) * page] = scores(pages[r][...])

    @pl.when(c == n_past_steps)
    def _():
        s = scores(ikn_ref[0])
        col = lax.broadcasted_iota(I32, s.shape, 1)
        row = lax.broadcasted_iota(I32, s.shape, 0)
        sc_ref[0, :, 0:page] = jnp.where(col <= row, s, -jnp.inf)
        sc_ref[0, :, page:] = jnp.full((nt, (PAGES_PER_STEP - 1) * page), -jnp.inf, F32)


def _page_specs(n_past_steps, layer, rows, page):
    def make(r):
        def imap(b, c, pt):
            return (layer, pt[b, jnp.minimum(c, n_past_steps - 1) * PAGES_PER_STEP + r], 0, 0)
        return pl.BlockSpec((None, None, rows, page), imap)
    return [make(r) for r in range(PAGES_PER_STEP)]


def _sample_scores(page_table, iq_rows, w_rows, ik_new, cache_ik, *, layer):
    DB, n_pages = page_table.shape
    page = cache_ik.shape[3]
    n_past_steps = n_pages // PAGES_PER_STEP
    rows = iq_rows.shape[1]
    nt = rows // IDX_HEADS
    step_cols = PAGES_PER_STEP * page
    grid_spec = pltpu.PrefetchScalarGridSpec(
        num_scalar_prefetch=1, grid=(DB, n_past_steps + 1),
        in_specs=[
            pl.BlockSpec((1, rows, IDX_DIM), lambda b, c, pt: (b, 0, 0)),
            pl.BlockSpec((1, rows, 1), lambda b, c, pt: (b, 0, 0)),
            pl.BlockSpec((1, IDX_DIM, page), lambda b, c, pt: (b, 0, 0)),
        ] + _page_specs(n_past_steps, layer, IDX_DIM, page),
        out_specs=pl.BlockSpec((1, nt, step_cols), lambda b, c, pt: (b, 0, c)),
    )
    kern = functools.partial(_sample_scores_kernel, n_past_steps=n_past_steps, page=page)
    return pl.pallas_call(
        kern, grid_spec=grid_spec,
        out_shape=jax.ShapeDtypeStruct((DB, nt, (n_past_steps + 1) * step_cols), F32),
        compiler_params=_cparams(2), name="sample_scores",
    )(page_table, iq_rows, w_rows, ik_new, *([cache_ik] * PAGES_PER_STEP))


def _sample_threshold_kernel(sc_ref, thr_ref, cut_ref, kint_ref, t_ref, c_ref, *, topk, tile):
    rows, cols = sc_ref.shape
    kint_ref[...] = jnp.where(sc_ref[...] == -jnp.inf, jnp.int32(INT_MIN), _sortable_key(sc_ref[...]))
    _topk_threshold(kint_ref, t_ref, c_ref, cols // tile, tile, rows, topk, cols)
    thr_ref[...] = jnp.broadcast_to(t_ref[...], thr_ref.shape)
    cut_ref[...] = jnp.broadcast_to(c_ref[...], cut_ref.shape)


def _sample_threshold(sc, *, topk):
    rows, cols = sc.shape
    kern = functools.partial(_sample_threshold_kernel, topk=topk, tile=2 * LANES)
    return pl.pallas_call(
        kern, grid=(1,),
        in_specs=[_const_spec(sc.shape)],
        out_specs=[_const_spec((rows, LANES)), _const_spec((rows, LANES))],
        out_shape=[jax.ShapeDtypeStruct((rows, LANES), I32)] * 2,
        scratch_shapes=[pltpu.VMEM((rows, cols), I32), pltpu.VMEM((rows, 1), I32), pltpu.VMEM((rows, 1), I32)],
        compiler_params=_cparams(1), name="sample_threshold",
    )(sc)


def _sample_attn_kernel(pt_ref, q_ref, sc_ref, thr_ref, cut_ref, kn_ref, vn_ref, hm_ref, *rest, n_past_steps, page):
    kp = rest[:PAGES_PER_STEP]
    vp = rest[PAGES_PER_STEP:2 * PAGES_PER_STEP]
    o_ref = rest[2 * PAGES_PER_STEP]
    m_ref, l_ref, acc_ref = rest[2 * PAGES_PER_STEP + 1:]
    c = pl.program_id(1)
    q = q_ref[0]
    mxu = q.dtype
    nt = sc_ref.shape[1]
    step_cols = PAGES_PER_STEP * page

    @pl.when(c == 0)
    def _():
        m_ref[...] = jnp.full(m_ref.shape, NEG_BIG, F32)
        l_ref[...] = jnp.zeros_like(l_ref)
        acc_ref[...] = jnp.zeros_like(acc_ref)

    def rows_th(x):
        return jnp.concatenate([jnp.broadcast_to(x[t:t + 1], (N_HEADS, x.shape[1])) for t in range(nt)], axis=0)

    def update(keys, vals, n_cols):
        sc = sc_ref[0, :, 0:n_cols]
        key = jnp.where(sc == -jnp.inf, jnp.int32(INT_MIN), _sortable_key(sc))
        col = c * step_cols + lax.broadcasted_iota(I32, key.shape, 1)
        sel = rows_th(_selected(key, col, thr_ref[0, :, 0:1], cut_ref[0, :, 0:1]).astype(F32)) > 0.5
        s = jnp.concatenate([jnp.dot(q, kk.astype(mxu), preferred_element_type=F32) for kk in keys], axis=1)
        s = jnp.where(sel, s, NEG_BIG)
        m_old = m_ref[...]
        m_new = jnp.maximum(m_old, jnp.max(s, axis=1, keepdims=True))
        alpha = jnp.exp2(m_old - m_new)
        pm = jnp.exp2(s - m_new)
        l_ref[...] = alpha * l_ref[...] + jnp.sum(pm, axis=1, keepdims=True)
        m_ref[...] = m_new
        pv = jnp.zeros(acc_ref.shape, F32)
        for r, vv in enumerate(vals):
            pv = pv + lax.dot_general(pm[:, r * page:(r + 1) * page].astype(mxu), vv.astype(mxu), _NT,
                                      preferred_element_type=F32)
        acc_ref[...] = alpha * acc_ref[...] + pv

    @pl.when(c < n_past_steps)
    def _():
        update([r[...] for r in kp], [r[...] for r in vp], step_cols)

    @pl.when(c == n_past_steps)
    def _():
        update([kn_ref[0]], [vn_ref[0]], page)
        out = acc_ref[...] / l_ref[...] * hm_ref[...]
        o_ref[0] = _head_sum(out).astype(o_ref.dtype)


def _sample_attn(page_table, q_rows, sc, thr, cut, k_new, v_new, head_mask, cache_k, cache_v, *, layer):
    DB, n_pages = page_table.shape
    page = cache_k.shape[3]
    n_past_steps = n_pages // PAGES_PER_STEP
    rows = q_rows.shape[1]
    nt = rows // N_HEADS
    step_cols = PAGES_PER_STEP * page
    per_b = lambda shape: pl.BlockSpec((1,) + shape, lambda b, c, pt: (b, 0, 0))
    grid_spec = pltpu.PrefetchScalarGridSpec(
        num_scalar_prefetch=1, grid=(DB, n_past_steps + 1),
        in_specs=[
            per_b((rows, A_WIDTH)),
            pl.BlockSpec((1, nt, step_cols), lambda b, c, pt: (b, 0, c)),
            per_b((nt, LANES)), per_b((nt, LANES)),
            per_b((A_WIDTH, page)), per_b((A_WIDTH, page)),
            pl.BlockSpec((rows, A_WIDTH), lambda b, c, pt: (0, 0)),
        ] + _page_specs(n_past_steps, layer, A_WIDTH, page) * 2,
        out_specs=per_b((nt, A_WIDTH)),
        scratch_shapes=[pltpu.VMEM((rows, 1), F32), pltpu.VMEM((rows, 1), F32), pltpu.VMEM((rows, A_WIDTH), F32)],
    )
    kern = functools.partial(_sample_attn_kernel, n_past_steps=n_past_steps, page=page)
    return pl.pallas_call(
        kern, grid_spec=grid_spec,
        out_shape=jax.ShapeDtypeStruct((DB, nt, A_WIDTH), q_rows.dtype),
        compiler_params=_cparams(2), name="sample_attn",
    )(page_table, q_rows, sc, thr, cut, k_new, v_new, head_mask,
      *([cache_k] * PAGES_PER_STEP), *([cache_v] * PAGES_PER_STEP))


def _shift_rows(e, k):
    return pltpu.roll(e, k, 0)


def _merge_kernel(x_ref, att_ref, u_ref, vv_ref, xc_ref, halo_ref, cnt_ref,
                  g1_ref, wgt_ref, wa_ref, wb_ref, wc_ref, wout_ref, ws_ref, btab_ref, wpool_ref, pscale_ref,
                  o_ref, *, row_stride, tiles_per_seq, zero_first_halo):
    mxu = wgt_ref.dtype
    tm = x_ref.shape[0]
    x = x_ref[...]
    xn = _rms(x, g1_ref[...]).astype(mxu)
    d = x.shape[1]

    def gate(n):
        return jax.nn.sigmoid(jnp.dot(xn, wgt_ref[:, n * d:(n + 1) * d], preferred_element_type=F32))

    merged = gate(0) * jnp.dot(att_ref[...], wa_ref[...], preferred_element_type=F32)

    lane = lax.broadcasted_iota(I32, (SGU_CHUNK, LANES), 1)
    lo = lane < (SGU_WIDTH // SGU_GROUPS)
    sgo = []
    for c in range(tm // SGU_CHUNK):
        rs = slice(c * SGU_CHUNK, (c + 1) * SGU_CHUNK)
        vvb = vv_ref[rs, :].astype(mxu)
        mix = []
        for p in range(SGU_GROUPS // 2):
            pair = vvb[:, LANES * p:LANES * (p + 1)]
            r0 = jnp.dot(ws_ref[2 * p], pair, preferred_element_type=F32)
            r1 = jnp.dot(ws_ref[2 * p + 1], pair, preferred_element_type=F32)
            mix.append(jnp.where(lo, r0, r1))
        sgo.append(u_ref[rs, :] * (jnp.concatenate(mix, axis=1) + btab_ref[...]))
    sgo = jnp.concatenate(sgo, axis=0).astype(mxu)
    merged = merged + gate(1) * jnp.dot(sgo, wb_ref[...], preferred_element_type=F32)

    xc = xc_ref[...]
    halo = halo_ref[...]
    if zero_first_halo:
        first = (pl.program_id(0) % tiles_per_seq) == 0
        halo = jnp.where(first, jnp.zeros_like(halo), halo)
    hp = halo.shape[0]
    e = jnp.concatenate([halo, xc], axis=0)
    s1 = e + _shift_rows(e, row_stride)
    s2 = s1 + _shift_rows(s1, 2 * row_stride)
    s3 = s2 + _shift_rows(s2, 4 * row_stride)
    s4 = s3 + _shift_rows(s3, 8 * row_stride)
    gd = POOL_WIDTH // POOL_GROUPS
    wsum = jnp.concatenate([s[hp:, g * gd:(g + 1) * gd] for g, s in enumerate((s1, s2, s3, s4))], axis=1)
    pooled = (wsum / cnt_ref[...] - xc).astype(mxu)
    po = (jnp.dot(pooled, wpool_ref[...], preferred_element_type=F32) * pscale_ref[...]).astype(mxu)
    merged = merged + gate(2) * jnp.dot(po, wc_ref[...], preferred_element_type=F32)

    o_ref[...] = x + jnp.dot(merged.astype(mxu), wout_ref[...], preferred_element_type=F32)


def _merge(x, att, u, vv, xc, halo, cnt, g1, wgt, wa, wb, wc, wout, ws, btab, wpool, pscale,
           *, tm, row_stride, tiles_per_seq, halo_rows, halo_from_xc, cnt_tiles):
    T, D = x.shape
    n = T // tm
    row = lambda width: pl.BlockSpec((tm, width), lambda i: (i, 0))
    if halo_from_xc:
        per = tm // halo_rows
        halo_spec = pl.BlockSpec((halo_rows, xc.shape[1]), lambda i: (jnp.maximum(i * per - 1, 0), 0))
    else:
        halo_spec = _const_spec(halo.shape)
    consts = [g1, wgt, wa, wb, wc, wout, ws, btab, wpool, pscale]
    kern = functools.partial(_merge_kernel, row_stride=row_stride, tiles_per_seq=tiles_per_seq,
                             zero_first_halo=halo_from_xc)
    return pl.pallas_call(
        kern, grid=(n,),
        in_specs=[row(D), row(att.shape[1]), row(u.shape[1]), row(vv.shape[1]), row(xc.shape[1]), halo_spec,
                  pl.BlockSpec((tm, cnt.shape[1]), lambda i: (i % cnt_tiles, 0))]
        + [_const_spec(c.shape) for c in consts],
        out_specs=row(D), out_shape=jax.ShapeDtypeStruct((T, D), F32),
        compiler_params=_cparams(1), name="merge",
    )(x, att, u, vv, xc, halo, cnt, *consts)


FFN_CHUNK = 256


def _ffn_kernel(x_ref, g2_ref, wa_ref, wu_ref, cw_ref, cb_ref, wd_ref, halo_ref, o_ref, tail_ref, carry_ref,
                *, row_stride, tiles_per_seq, use_carry):
    mxu = wa_ref.dtype
    x = x_ref[...]
    tm = x.shape[0]
    xn = _rms(x, g2_ref[...]).astype(mxu)
    dff = wa_ref.shape[1]
    tail = tail_ref.shape[1]
    acc = jnp.zeros(x.shape, F32)
    if use_carry:
        first = (pl.program_id(0) % tiles_per_seq) == 0
    for c0 in range(0, dff, FFN_CHUNK):
        cs = slice(c0, c0 + FFN_CHUNK)
        a = jnp.dot(xn, wa_ref[:, cs], preferred_element_type=F32)
        up = jnp.dot(xn, wu_ref[:, cs], preferred_element_type=F32)
        if use_carry:
            halo = jnp.where(first, 0.0, carry_ref[:, cs])
            carry_ref[:, cs] = a[tm - carry_ref.shape[0]:, :]
        else:
            halo = halo_ref[:, cs]
        hp = halo.shape[0]
        e = jnp.concatenate([halo, a], axis=0)
        p1 = _shift_rows(e, row_stride)[hp:]
        p2 = _shift_rows(e, 2 * row_stride)[hp:]
        conv = p2 * cw_ref[0:1, cs] + p1 * cw_ref[1:2, cs] + a * cw_ref[2:3, cs] + cb_ref[:, cs]
        act = (jax.nn.silu(conv) * up).astype(mxu)
        acc = acc + jnp.dot(act, wd_ref[cs, :], preferred_element_type=F32)
        tail_ref[0, :, cs] = a[tm - tail:, :]
    o_ref[...] = x + acc


def _ffn(x, g2, wa, wu, cw, cb, wd, halo, *, tm, row_stride, tiles_per_seq, use_carry, tail):
    T, D = x.shape
    n = T // tm
    dff = wa.shape[1]
    row = pl.BlockSpec((tm, D), lambda i: (i, 0))
    consts = [g2, wa, wu, cw, cb, wd, halo]
    kern = functools.partial(_ffn_kernel, row_stride=row_stride, tiles_per_seq=tiles_per_seq, use_carry=use_carry)
    return pl.pallas_call(
        kern, grid=(n,),
        in_specs=[row] + [_const_spec(c.shape) for c in consts],
        out_specs=[row, pl.BlockSpec((1, tail, dff), lambda i: (i, 0, 0))],
        out_shape=[jax.ShapeDtypeStruct((T, D), F32), jax.ShapeDtypeStruct((n, tail, dff), F32)],
        scratch_shapes=[pltpu.VMEM((SUBLANES, dff), F32)],
        compiler_params=_cparams(1), name="ffn",
    )(x, *consts)


def _pack_w_in(w_in):
    d = w_in.shape[0]
    o = 0
    q, k, v = (w_in[:, o + i * A_WIDTH:o + (i + 1) * A_WIDTH] for i in range(3))
    o += 3 * A_WIDTH
    iq = w_in[:, o:o + IDX_HEADS * IDX_DIM]
    o += IDX_HEADS * IDX_DIM
    ik = w_in[:, o:o + IDX_DIM]
    o += IDX_DIM
    iw = w_in[:, o:o + IDX_HEADS]
    o += IDX_HEADS
    sg = w_in[:, o:o + 2 * SGU_WIDTH]
    o += 2 * SGU_WIDTH
    xc = w_in[:, o:o + POOL_WIDTH]
    o += POOL_WIDTH
    gt = w_in[:, o:]
    pad = jnp.zeros((d, LANES - IDX_DIM - IDX_HEADS), w_in.dtype)
    packed = jnp.concatenate([q, k, v, iq, ik, iw, pad, ik, ik, sg, xc], axis=1)
    assert packed.shape[1] == PROJ_COLS
    return packed, gt


def _rope_tables(pos):
    posf = np.asarray(pos, np.float64)[:, None]

    def cs(half):
        inv = ROPE_THETA ** (-np.arange(half, dtype=np.float64) / half)
        ang = posf * inv[None, :]
        return np.cos(ang), np.sin(ang)

    t = posf.shape[0]
    c32, s32 = cs(HEAD_DIM // 2)
    cqk = np.tile(np.concatenate([c32, c32], axis=1), (1, 2))
    sqk = np.tile(np.concatenate([-s32, s32], axis=1), (1, 2))
    c16, s16 = cs(IDX_ROPE // 2)
    rest = IDX_DIM - IDX_ROPE
    ci = np.concatenate([c16, c16, np.ones((t, rest))], axis=1)
    si = np.concatenate([-s16, s16, np.zeros((t, rest))], axis=1)
    ciq, siq = np.tile(ci, (1, 2)), np.tile(si, (1, 2))
    wpad = LANES - IDX_DIM - IDX_HEADS
    cikw = np.concatenate([ci, np.full((t, IDX_HEADS), IDX_HEADS ** -0.5), np.zeros((t, wpad))], axis=1)
    sikw = np.concatenate([si, np.zeros((t, LANES - IDX_DIM))], axis=1)
    return jnp.asarray(np.concatenate([cqk, sqk, ciq, siq, cikw, sikw], axis=1).astype(np.float32))


def _block_diag(blocks):
    n = len(blocks)
    r, c = blocks[0].shape
    out = jnp.zeros((n * r, n * c), blocks[0].dtype)
    for i, b in enumerate(blocks):
        out = out.at[i * r:(i + 1) * r, i * c:(i + 1) * c].set(b)
    return out


def _pool_counts(pos):
    gd = POOL_WIDTH // POOL_GROUPS
    pos = np.asarray(pos)
    cols = [np.broadcast_to(np.minimum(pos + 1, w).astype(np.float32)[:, None], (pos.shape[0], gd))
            for w in POOL_WINDOWS]
    return jnp.asarray(np.concatenate(cols, axis=1))


def _head_segments(dtype):
    return jnp.asarray(np.kron(np.eye(N_HEADS, dtype=np.float32), np.ones((HEAD_DIM, HEAD_DIM), np.float32)), dtype)


def _layer_weights(l, norm1_g, w_in, q_norm_g, k_norm_g, sgu_w, sgu_b, sgu_norm_g, pool_w, pool_scale,
                   w_br_a, w_br_b, w_br_c, w_out, norm2_g, w_ff_in, ff_conv_w, ff_conv_b, w_ff_down):
    mxu = MXU_DTYPE
    packed, gt = _pack_w_in(w_in[l])
    dff = w_ff_in.shape[2] // 2
    tril = jnp.tril(jnp.ones((SGU_CHUNK, SGU_CHUNK), bool))
    return dict(
        g1=norm1_g[l][None, :], w_proj=packed.astype(mxu), w_gt=gt.astype(mxu),
        qg=jnp.tile(q_norm_g[l], N_HEADS)[None, :], kg=jnp.tile(k_norm_g[l], N_HEADS)[None, :],
        sgg=sgu_norm_g[l][None, :],
        ws=jnp.where(tril[None], sgu_w[l], 0.0),
        sgu_b=sgu_b[l],
        wpool=_block_diag([pool_w[l, g] for g in range(POOL_GROUPS)]).astype(mxu),
        pscale=pool_scale[l][None, :],
        wa=w_br_a[l].astype(mxu), wb=w_br_b[l].astype(mxu), wc=w_br_c[l].astype(mxu), wout=w_out[l].astype(mxu),
        g2=norm2_g[l][None, :],
        w_ffa=w_ff_in[l, :, :dff].astype(mxu), w_ffu=w_ff_in[l, :, dff:].astype(mxu),
        cw=jnp.pad(ff_conv_w[l], ((0, SUBLANES - CONV_W), (0, 0))), cb=ff_conv_b[l][None, :],
        w_ffd=w_ff_down[l].astype(mxu),
    )


def _sgu_bias_table(sgu_b, t_of_row):
    gd = SGU_WIDTH // SGU_GROUPS
    return jnp.repeat(jnp.transpose(sgu_b)[t_of_row], gd, axis=1)


def _prompt_layer(x, lw, tabs, cnt, l, depth, state, *, B, S):
    mxu = MXU_DTYPE
    T = B * S
    tm_proj, tm_merge, tm_ffn, qb = 512, 256, 512, 256
    pr = _proj(x, lw["g1"], lw["w_proj"], lw["qg"], lw["kg"], lw["sgg"], tabs, _head_segments(mxu),
               tm=tm_proj, tab_tiles=S // tm_proj, stack=(l, depth, B, S, state))
    u, vv, xc = pr["u"], pr["vv"], pr["xc"]
    r3 = lambda a: a.reshape(B, S, a.shape[1])
    topk = min(TOPK_MAX, S // 4)
    att = _dsa_prompt(r3(pr["qb"]), r3(pr["iqb"]), r3(pr["ikw"]), r3(pr["kb"]), pr["vt"], r3(pr["ikd"]),
                      qb=qb, topk=topk).reshape(T, A_WIDTH)
    btab = _sgu_bias_table(lw["sgu_b"], jnp.arange(SGU_CHUNK))
    x1 = _merge(x, att, u, vv, xc, xc, cnt, lw["g1"], lw["w_gt"], lw["wa"], lw["wb"], lw["wc"], lw["wout"],
                lw["ws"].astype(mxu), btab, lw["wpool"], lw["pscale"],
                tm=tm_merge, row_stride=1, tiles_per_seq=S // tm_merge, halo_rows=16, halo_from_xc=True,
                cnt_tiles=S // tm_merge)
    dff = lw["w_ffa"].shape[1]
    x2, tails = _ffn(x1, lw["g2"], lw["w_ffa"], lw["w_ffu"], lw["cw"], lw["cb"], lw["w_ffd"],
                     jnp.zeros((SUBLANES, dff), F32),
                     tm=tm_ffn, row_stride=1, tiles_per_seq=S // tm_ffn, use_carry=True, tail=SUBLANES)
    n_t = S // tm_ffn
    ff_state = tails.reshape(B, n_t, SUBLANES, dff)[:, -1, SUBLANES - (CONV_W - 1):, :]
    pool_state = xc.reshape(B, S, POOL_WIDTH)[:, S - POOL_STATE:, :]
    return x2, (pr["k"], pr["v"], pr["ik"]), (pool_state, ff_state)


def _sample_layer(x, lw, tabs, cnt, l, cache_k, cache_v, cache_ik, state_pool, state_ffn, page_table, *, DB, TS):
    mxu = MXU_DTYPE
    T = TS * DB
    page = cache_ik.shape[3]
    past = page_table.shape[1] * page
    pr = _proj(x, lw["g1"], lw["w_proj"], lw["qg"], lw["kg"], lw["sgg"], tabs, _head_segments(mxu),
               tm=T, tab_tiles=1)
    qb_, k, kb, v, vt, iqb, ikw, ikd, u, vv, xc = (
        pr[n] for n in ("qb", "k", "kb", "v", "vt", "iqb", "ikw", "ikd", "u", "vv", "xc"))
    bm = lambda a: jnp.transpose(a.reshape(TS, DB, a.shape[1]), (1, 0, 2))

    iq_rows = bm(iqb).reshape(DB, TS * IDX_HEADS, IDX_DIM)
    w_rows = bm(ikw)[:, :, IDX_DIM:IDX_DIM + IDX_HEADS].reshape(DB, TS * IDX_HEADS, 1)
    new_t = lambda a: jnp.pad(jnp.transpose(a, (0, 2, 1)), ((0, 0), (0, 0), (0, page - TS)))
    ik_new = new_t(bm(ikd)[:, :, :IDX_DIM])
    sc = _sample_scores(page_table, iq_rows, w_rows, ik_new, cache_ik, layer=l)
    topk = min(TOPK_MAX, (past + TS) // 4)
    thr, cut = _sample_threshold(sc.reshape(DB * TS, sc.shape[2]), topk=topk)
    hm = jnp.asarray(np.repeat(np.eye(N_HEADS, dtype=np.float32), HEAD_DIM, axis=1))
    q_rows = (bm(qb_)[:, :, None, :] * hm[None, None].astype(mxu)).reshape(DB, TS * N_HEADS, A_WIDTH)
    att = _sample_attn(page_table, q_rows, sc, thr.reshape(DB, TS, LANES), cut.reshape(DB, TS, LANES),
                       new_t(bm(kb)), new_t(bm(jnp.transpose(vt))), jnp.tile(hm, (TS, 1)), cache_k, cache_v, layer=l)
    att = jnp.transpose(att, (1, 0, 2)).reshape(T, A_WIDTH)

    eye = jnp.eye(DB, dtype=F32)
    ws = jnp.stack([jnp.kron(lw["ws"][g, :TS, :TS], eye) for g in range(SGU_GROUPS)])
    pc = SGU_CHUNK - T
    ws = jnp.pad(ws, ((0, 0), (0, pc), (0, pc)))
    btab = _sgu_bias_table(lw["sgu_b"], jnp.minimum(jnp.arange(SGU_CHUNK) // DB, SGU_CHUNK - 1))
    halo = jnp.concatenate([jnp.zeros((DB, POOL_WIDTH), F32),
                            jnp.transpose(state_pool[l], (1, 0, 2)).reshape(POOL_STATE * DB, POOL_WIDTH)], axis=0)
    x1 = _merge(x, att, u, vv, xc, halo, cnt, lw["g1"], lw["w_gt"], lw["wa"], lw["wb"], lw["wc"], lw["wout"],
                ws.astype(mxu), btab, lw["wpool"], lw["pscale"],
                tm=T, row_stride=DB, tiles_per_seq=1, halo_rows=halo.shape[0], halo_from_xc=False, cnt_tiles=1)
    dff = lw["w_ffa"].shape[1]
    ff_halo = jnp.transpose(state_ffn[l], (1, 0, 2)).reshape((CONV_W - 1) * DB, dff)
    tail = (CONV_W - 1) * DB
    x2, tails = _ffn(x1, lw["g2"], lw["w_ffa"], lw["w_ffu"], lw["cw"], lw["cb"], lw["w_ffd"], ff_halo,
                     tm=T, row_stride=DB, tiles_per_seq=1, use_carry=False, tail=tail)
    ff_state = jnp.transpose(tails.reshape(CONV_W - 1, DB, dff), (1, 0, 2))
    pool_state = jnp.concatenate([state_pool[l], bm(xc)], axis=1)[:, -POOL_STATE:, :]
    hd = lambda a: bm(a).reshape(DB, TS, N_HEADS, HEAD_DIM)
    return x2, (hd(k), hd(v), bm(ikw)[:, :, :IDX_DIM], bm(vv), pool_state, ff_state)


def kernel(x_prompt, x_sample, cache_k, cache_v, cache_idx_k, state_pool, state_ffn_conv, page_table, norm1_g, w_in, q_norm_g, k_norm_g, sgu_w, sgu_b, sgu_norm_g, pool_w, pool_scale, w_br_a, w_br_b, w_br_c, w_out, norm2_g, w_ff_in, ff_conv_w, ff_conv_b, w_ff_down):
    weights = (norm1_g, w_in, q_norm_g, k_norm_g, sgu_w, sgu_b, sgu_norm_g, pool_w, pool_scale,
               w_br_a, w_br_b, w_br_c, w_out, norm2_g, w_ff_in, ff_conv_w, ff_conv_b, w_ff_down)
    B, S, D = x_prompt.shape
    DB, TS, _ = x_sample.shape
    depth = w_in.shape[0]
    page = cache_idx_k.shape[2]
    past = page_table.shape[1] * page
    pool = cache_k.shape[1]
    cache_kt = jnp.transpose(cache_k, (0, 1, 3, 4, 2)).reshape(depth, pool, A_WIDTH, page)
    cache_vt = jnp.transpose(cache_v, (0, 1, 3, 4, 2)).reshape(depth, pool, A_WIDTH, page)
    cache_ikt = jnp.transpose(cache_idx_k, (0, 1, 3, 2))
    assert S % 512 == 0 and S >= POOL_STATE and TS * DB == SGU_CHUNK and TS >= CONV_W - 1 and TS <= page
    assert past % SGU_CHUNK == 0 and page_table.shape[1] % PAGES_PER_STEP == 0

    pos_p = np.arange(S)
    pos_s = past + np.repeat(np.arange(TS), DB)
    tabs_p, tabs_s = _rope_tables(pos_p), _rope_tables(pos_s)
    cnt_p, cnt_s = _pool_counts(pos_p), _pool_counts(pos_s)

    xp = x_prompt.reshape(B * S, D)
    xs = jnp.transpose(x_sample, (1, 0, 2)).reshape(TS * DB, D)
    outs_p, outs_s = [], []
    kv_state = None
    for l in range(depth):
        lw = _layer_weights(l, *weights)
        xp, kv_state, st_p = _prompt_layer(xp, lw, tabs_p, cnt_p, l, depth, kv_state, B=B, S=S)
        xs, st_s = _sample_layer(xs, lw, tabs_s, cnt_s, l, cache_kt, cache_vt, cache_ikt, state_pool,
                                 state_ffn_conv, page_table, DB=DB, TS=TS)
        outs_p.append(st_p)
        outs_s.append(st_s)
    stack = lambda outs, i: jnp.stack([o[i] for o in outs])
    y_p = xp.reshape(B, S, D)
    y_s = jnp.transpose(xs.reshape(TS, DB, D), (1, 0, 2))
    k_all, v_all, ik_all = kv_state
    heads = lambda a: jnp.transpose(a.reshape(depth, B, N_HEADS, HEAD_DIM, S), (0, 1, 4, 2, 3))
    return (y_p, y_s,
            heads(k_all), heads(v_all), jnp.transpose(ik_all, (0, 1, 3, 2)), stack(outs_p, 0), stack(outs_p, 1),
            stack(outs_s, 0), stack(outs_s, 1), stack(outs_s, 2), stack(outs_s, 3), stack(outs_s, 4), stack(outs_s, 5))
```

```python
import functools

import jax
import jax.numpy as jnp
import numpy as np
from jax import lax
from jax.experimental import pallas as pl
from jax.experimental.pallas import tpu as pltpu

MXU_DTYPE = jnp.bfloat16
F32 = jnp.float32
I32 = jnp.int32

N_HEADS = 8
HEAD_DIM = 64
A_WIDTH = N_HEADS * HEAD_DIM
IDX_HEADS = 8
IDX_DIM = 64
IDX_ROPE = 32
TOPK_MAX = 256
SGU_GROUPS = 4
SGU_WIDTH = 256
SGU_CHUNK = 128
POOL_GROUPS = 4
POOL_WIDTH = 256
POOL_WINDOWS = (2, 4, 8, 16)
POOL_STATE = 15
CONV_W = 3
ROPE_THETA = 10000.0
EPS = 1e-6

LANES = 128
SUBLANES = 8
INT_MIN = -2 ** 31
NEG_BIG = -1e30
Q_SCALE = HEAD_DIM ** -0.5 * float(np.log2(np.e))
PROJ_COLS = 3072
VMEM_LIMIT = 56 * 1024 * 1024

_NT = (((1,), (1,)), ((), ()))


def _cparams(n_axes):
    return pltpu.CompilerParams(dimension_semantics=("arbitrary",) * n_axes, vmem_limit_bytes=VMEM_LIMIT)


def _const_spec(shape):
    nd = len(shape)
    return pl.BlockSpec(shape, lambda *_: (0,) * nd)


def _rope128(x, cos, sin, half):
    lane = lax.broadcasted_iota(I32, x.shape, 1)
    lo = (lane & (HEAD_DIM - 1)) < half
    rot = jnp.where(lo, pltpu.roll(x, LANES - half, 1), pltpu.roll(x, half, 1))
    return x * cos + rot * sin


def _head_rms(x, seg, g):
    ss = jnp.dot((x * x).astype(seg.dtype), seg, preferred_element_type=F32)
    return x * lax.rsqrt(ss * (1.0 / HEAD_DIM) + EPS) * g


def _rms(x, g):
    return x * lax.rsqrt(jnp.mean(x * x, axis=-1, keepdims=True) + EPS) * g


def _proj_kernel(x_ref, g1_ref, w_ref, qg_ref, kg_ref, sgg_ref, tab_ref, seg_ref, *rest, stacked, n_alias):
    rest = rest[n_alias:]
    if stacked:
        qb_ref, k_ref, kb_ref, v_ref, vt_ref, iqb_ref, ikw_ref, ik_ref, ikd_ref, u_ref, vv_ref, xc_ref = rest
    else:
        qb_ref, k_ref, kb_ref, v_ref, vt_ref, iqb_ref, ikw_ref, ikd_ref, u_ref, vv_ref, xc_ref = rest
    mxu = w_ref.dtype
    xn = _rms(x_ref[...], g1_ref[...]).astype(mxu)

    def mm(c0, c1):
        return jnp.dot(xn, w_ref[:, c0:c1], preferred_element_type=F32)

    cqk, sqk = tab_ref[:, 0:128], tab_ref[:, 128:256]
    ciq, siq = tab_ref[:, 256:384], tab_ref[:, 384:512]
    cikw, sikw = tab_ref[:, 512:640], tab_ref[:, 640:768]
    seg = seg_ref[...]
    half_qk = HEAD_DIM // 2
    half_idx = IDX_ROPE // 2

    q = _head_rms(mm(0, 512), seg, qg_ref[...])
    for c in range(4):
        sl = slice(LANES * c, LANES * (c + 1))
        qb_ref[:, sl] = (_rope128(q[:, sl], cqk, sqk, half_qk) * Q_SCALE).astype(mxu)
    k = _head_rms(mm(512, 1024), seg, kg_ref[...])
    for c in range(4):
        sl = slice(LANES * c, LANES * (c + 1))
        kr = _rope128(k[:, sl], cqk, sqk, half_qk)
        if stacked:
            k_ref[sl, :] = jnp.transpose(kr)
        else:
            k_ref[:, sl] = kr
        kb_ref[:, sl] = kr.astype(mxu)
    v = mm(1024, 1536)
    vt = jnp.transpose(v)
    v_ref[...] = vt if stacked else v
    vt_ref[...] = vt.astype(mxu)
    iq = mm(1536, 2048)
    for c in range(4):
        sl = slice(LANES * c, LANES * (c + 1))
        iqb_ref[:, sl] = (_rope128(iq[:, sl], ciq, siq, half_idx) * (IDX_DIM ** -0.5)).astype(mxu)
    ikw = _rope128(mm(2048, 2176), cikw, sikw, half_idx)
    ikw_ref[...] = ikw
    if stacked:
        ik_ref[...] = jnp.transpose(ikw)[:IDX_DIM, :]
    ikd_ref[...] = _rope128(mm(2176, 2304), ciq, siq, half_idx).astype(mxu)
    sg = jax.nn.gelu(mm(2304, 2816))
    u_ref[...] = sg[:, :SGU_WIDTH]
    vv_ref[...] = _rms(sg[:, SGU_WIDTH:], sgg_ref[...])
    xc_ref[...] = mm(2816, 3072)


def _proj(x, g1, w, qg, kg, sgg, tab, seg, *, tm, tab_tiles, stack=None):
    T, D = x.shape
    mxu = w.dtype
    n = T // tm
    row = lambda width: pl.BlockSpec((tm, width), lambda i: (i, 0))
    natural = lambda width, dt: (jax.ShapeDtypeStruct((T, width), dt), row(width))
    outs = dict(
        qb=natural(A_WIDTH, mxu),
        k=natural(A_WIDTH, F32), kb=natural(A_WIDTH, mxu), v=natural(A_WIDTH, F32),
        vt=(jax.ShapeDtypeStruct((A_WIDTH, T), mxu), pl.BlockSpec((A_WIDTH, tm), lambda i: (0, i))),
        iqb=natural(IDX_HEADS * IDX_DIM, mxu),
        ikw=natural(LANES, F32),
        ik=None,
        ikd=natural(LANES, mxu),
        u=natural(SGU_WIDTH, F32), vv=natural(SGU_WIDTH, F32), xc=natural(POOL_WIDTH, F32),
    )
    in_specs = [
        row(D), _const_spec(g1.shape), _const_spec(w.shape), _const_spec(qg.shape), _const_spec(kg.shape),
        _const_spec(sgg.shape),
        pl.BlockSpec((tm, tab.shape[1]), lambda i: (i % tab_tiles, 0)),
        _const_spec(seg.shape),
    ]
    args = [x, g1, w, qg, kg, sgg, tab, seg]
    aliases = {}
    if stack is not None:
        layer, depth, B, S, state = stack
        tps = S // tm
        state_out = lambda feat: (jax.ShapeDtypeStruct((depth, B, feat, S), F32),
                                  pl.BlockSpec((None, None, feat, tm), lambda i: (layer, i // tps, 0, i % tps)))
        outs.update(k=state_out(A_WIDTH), v=state_out(A_WIDTH), ik=state_out(IDX_DIM))
        if state is not None:
            names = list(k for k, o in outs.items() if o is not None)
            for buf, name in zip(state, ("k", "v", "ik")):
                aliases[len(args)] = names.index(name)
                args.append(buf)
                in_specs.append(pl.BlockSpec(memory_space=pl.ANY))
    outs = {k: o for k, o in outs.items() if o is not None}
    kern = functools.partial(_proj_kernel, stacked=stack is not None, n_alias=len(aliases))
    res = pl.pallas_call(
        kern, grid=(n,), in_specs=in_specs, out_specs=[o[1] for o in outs.values()],
        out_shape=[o[0] for o in outs.values()], input_output_aliases=aliases,
        compiler_params=_cparams(1), name="proj",
    )(*args)
    return dict(zip(outs.keys(), res))


def _sortable_key(s):
    b = lax.bitcast_convert_type(s, I32)
    key = jnp.where(b < 0, b ^ jnp.int32(0x7FFFFFFF), b)
    return jnp.where(key == -1, 0, key)


def _fold_lanes(m):
    out = m[:, 0:LANES]
    for c in range(1, m.shape[1] // LANES):
        out = out + m[:, c * LANES:(c + 1) * LANES]
    return out


def _fold_sublanes(m):
    out = m[0:SUBLANES]
    for r in range(1, m.shape[0] // SUBLANES):
        out = out + m[r * SUBLANES:(r + 1) * SUBLANES]
    return out


I16 = jnp.int16
PACK16 = 2 * SUBLANES


def _kth_largest_by_halves(khi_ref, klo_ref, low_ref, n_tiles, tile, rows, kf):
    def count_ge(ref, cand):
        c16 = cand.astype(I16)

        def body(j, c):
            off = pl.multiple_of(j * tile, tile)
            m = jnp.where(ref[pl.ds(off, tile), :] >= c16, jnp.int16(1), jnp.int16(0))
            out = m[0:PACK16]
            for r in range(1, tile // PACK16):
                out = out + m[r * PACK16:(r + 1) * PACK16]
            return c + out
        part = lax.fori_loop(0, n_tiles, body, jnp.zeros((PACK16, rows), I16))
        return jnp.sum(part.astype(F32), axis=0, keepdims=True)

    def bisect(ref):
        c0 = count_ge(ref, jnp.zeros((1, rows), I32))
        ok = c0 >= kf
        state = (jnp.where(ok, jnp.int32(0), jnp.int32(-32768)), jnp.where(ok, c0, 0.0))

        def bit_body(b, state):
            t, c = state
            cand = t + jnp.left_shift(jnp.int32(1), 14 - b)
            cnt = count_ge(ref, cand)
            ok = cnt >= kf
            return jnp.where(ok, cand, t), jnp.where(ok, cnt, c)

        return lax.fori_loop(0, 15, bit_body, state)

    hi, cnt_hi = bisect(khi_ref)
    hi16 = hi.astype(I16)

    def build(j, carry):
        off = pl.multiple_of(j * tile, tile)
        kh = khi_ref[pl.ds(off, tile), :]
        low_ref[pl.ds(off, tile), :] = jnp.where(
            kh > hi16, jnp.int16(32767), jnp.where(kh == hi16, klo_ref[pl.ds(off, tile), :], jnp.int16(-32768)))
        return carry

    lax.fori_loop(0, n_tiles, build, 0)
    lo, cnt = bisect(low_ref)
    return jnp.left_shift(hi, 16) + (lo + 32768), jnp.where(lo == -32768, cnt_hi, cnt)


def _topk_threshold(kint_ref, thr_ref, cut_ref, n_tiles, tile, rows, topk, n_cols, keys_axis=1, halves=None):
    stat_shape = (rows, 1) if keys_axis == 1 else (1, rows)

    def count(pred):
        def body(j, c):
            off = pl.multiple_of(j * tile, tile)
            if keys_axis == 1:
                key = kint_ref[:, pl.ds(off, tile)]
                idx = off + lax.broadcasted_iota(I32, (rows, tile), 1)
                return c + _fold_lanes(pred(key, idx).astype(F32))
            key = kint_ref[pl.ds(off, tile), :]
            idx = off + lax.broadcasted_iota(I32, (tile, rows), 0)
            return c + _fold_sublanes(pred(key, idx).astype(F32))
        init = jnp.zeros((rows, LANES) if keys_axis == 1 else (SUBLANES, rows), F32)
        part = lax.fori_loop(0, n_tiles, body, init)
        return jnp.sum(part, axis=keys_axis, keepdims=True)

    kf = float(topk)
    if halves is None:
        c0 = count(lambda key, col: key >= 0)
        t = jnp.where(c0 >= kf, jnp.int32(0), jnp.int32(INT_MIN))

        def bit_body(b, t):
            cand = t + jnp.left_shift(jnp.int32(1), 30 - b)
            cnt = count(lambda key, col: key >= cand)
            return jnp.where(cnt >= kf, cand, t)

        t = lax.fori_loop(0, 31, bit_body, t)
        t = jnp.maximum(t, jnp.int32(INT_MIN + 1))
        cge = count(lambda key, col: key >= t)
    else:
        t, cge = _kth_largest_by_halves(*halves, n_tiles, tile, rows, kf)
        t = jnp.maximum(t, jnp.int32(INT_MIN + 1))
    thr_ref[...] = t
    cut_ref[...] = jnp.full(stat_shape, n_cols, I32)

    @pl.when(jnp.max(cge) > kf)
    def _():
        need = kf - count(lambda key, col: key > t)
        nbits = max(1, (n_cols - 1).bit_length())

        def idx_body(b, p):
            cand = p + jnp.left_shift(jnp.int32(1), nbits - 1 - b)
            cnt = count(lambda key, col: (key == t) & (col < cand))
            return jnp.where(cnt < need, cand, p)

        p = lax.fori_loop(0, nbits, idx_body, jnp.zeros(stat_shape, I32))
        cut_ref[...] = jnp.where(cge > kf, p, jnp.int32(n_cols))


def _selected(key, col, t, cut):
    return (key > t) | ((key == t) & (col <= cut))


KEY_TILE = 128


def _dsa_prompt_kernel(qb_ref, iqb_ref, ikw_ref, kb_ref, vt_ref, ikd_ref, o_ref,
                       kint_ref, khi_ref, klo_ref, low_ref,
                       qm_ref, iqm_ref, w_ref, s0_ref, s1_ref, p0_ref, p1_ref, a0_ref, a1_ref,
                       acc_ref, m_ref, l_ref, thr_ref, cut_ref, *, topk, qb, seq):
    i = pl.program_id(1)
    kt = KEY_TILE
    assert qb == 2 * kt
    n_tiles = 2 * (i + 1)
    last = n_tiles - 1
    s_slots, p_slots, a_slots = (s0_ref, s1_ref), (p0_ref, p1_ref), (a0_ref, a1_ref)
    lane = lax.broadcasted_iota(I32, (qb, LANES), 1)
    lo = lane < HEAD_DIM
    for h in range(N_HEADS):
        sl = slice(LANES * (h // 2), LANES * (h // 2 + 1))
        msk = lo if h % 2 == 0 else jnp.logical_not(lo)
        qm_ref[h] = jnp.where(msk, qb_ref[0, :, sl], jnp.zeros((), qb_ref.dtype))
        iqm_ref[h] = jnp.where(msk, iqb_ref[0, :, sl], jnp.zeros((), iqb_ref.dtype))
    w_ref[...] = jnp.transpose(ikw_ref[0])[IDX_DIM:IDX_DIM + IDX_HEADS, :]
    qpos = i * qb + lax.broadcasted_iota(I32, (1, qb), 1)

    def tile_off(tile):
        return pl.multiple_of(jnp.clip(tile, 0, last) * kt, kt)

    def idx_dots(tile, s_out):
        ik = ikd_ref[0, pl.ds(tile_off(tile), kt), :]
        for h in range(IDX_HEADS):
            s_out[h] = lax.dot_general(ik, iqm_ref[h], _NT, preferred_element_type=F32)

    def combine(tile, s_in):
        sc = jnp.zeros((kt, qb), F32)
        for h in range(IDX_HEADS):
            sc = sc + jnp.maximum(s_in[h], 0.0) * w_ref[h:h + 1, :]
        off = tile_off(tile)
        kpos = off + lax.broadcasted_iota(I32, (kt, qb), 0)
        key = jnp.where(kpos <= qpos, _sortable_key(sc), jnp.int32(INT_MIN))
        kint_ref[pl.ds(off, kt), :] = key
        khi_ref[pl.ds(off, kt), :] = jnp.right_shift(key, 16).astype(I16)
        klo_ref[pl.ds(off, kt), :] = ((key & 0xFFFF) - 32768).astype(I16)

    idx_dots(0, s0_ref)

    def score_pair(jj, carry):
        for step in range(2):
            tile = 2 * jj + step
            combine(tile, s_slots[step])
            idx_dots(tile + 1, s_slots[1 - step])
        return carry

    lax.fori_loop(0, i + 1, score_pair, 0)
    pad = pl.ds(pl.multiple_of((i + 1) * qb, qb), qb)
    kint_ref[pad, :] = jnp.full((qb, qb), INT_MIN, I32)
    khi_ref[pad, :] = jnp.full((qb, qb), -32768, I16)
    klo_ref[pad, :] = jnp.full((qb, qb), -32768, I16)
    _topk_threshold(kint_ref, thr_ref, cut_ref, (i + 2) // 2, 2 * qb, qb, topk, seq, keys_axis=0,
                    halves=(khi_ref, klo_ref, low_ref))
    thr = thr_ref[...]
    cut = cut_ref[...]

    acc_ref[...] = jnp.zeros_like(acc_ref)
    m_ref[...] = jnp.full(m_ref.shape, NEG_BIG, F32)
    l_ref[...] = jnp.zeros_like(l_ref)
    p1_ref[...] = jnp.zeros_like(p1_ref)
    a1_ref[...] = jnp.ones_like(a1_ref)

    def qk_dots(tile, s_out):
        off = tile_off(tile)
        for h in range(N_HEADS):
            kk = kb_ref[0, pl.ds(off, kt), LANES * (h // 2):LANES * (h // 2 + 1)]
            s_out[h] = lax.dot_general(kk, qm_ref[h], _NT, preferred_element_type=F32)

    def softmax(tile, s_in, p_out, a_out):
        off = tile_off(tile)
        live = tile <= last
        kpos = off + lax.broadcasted_iota(I32, (kt, qb), 0)
        sel = _selected(kint_ref[pl.ds(off, kt), :], kpos,
                        jnp.where(live, thr, jnp.int32(2 ** 31 - 1)), jnp.where(live, cut, jnp.int32(-1)))
        bias = jnp.where(sel, 0.0, NEG_BIG)
        for h in range(N_HEADS):
            s = s_in[h] + bias
            m_old = m_ref[h]
            m_new = jnp.maximum(m_old, jnp.max(s, axis=0, keepdims=True))
            alpha = jnp.exp2(m_old - m_new)
            pm = jnp.exp2(s - m_new)
            m_ref[h] = m_new
            p_out[h] = pm.astype(p_out.dtype)
            a_out[h] = alpha

    ones_rows = jnp.ones((PACK16, kt), vt_ref.dtype)

    def pv_dots(tile, p_in, a_in):
        off = tile_off(tile)
        for h in range(N_HEADS):
            dr = slice(HEAD_DIM * h, HEAD_DIM * (h + 1))
            lhs = jnp.concatenate([vt_ref[dr, pl.ds(off, kt)], ones_rows], axis=0)
            pv = jnp.dot(lhs, p_in[h], preferred_element_type=F32)
            acc_ref[dr, :] = a_in[h] * acc_ref[dr, :] + pv[:HEAD_DIM]
            l_ref[h] = a_in[h] * l_ref[h] + pv[HEAD_DIM:HEAD_DIM + 1]

    qk_dots(0, s0_ref)

    def att_pair(jj, carry):
        for step in range(2):
            tile = 2 * jj + step
            softmax(tile, s_slots[step], p_slots[step], a_slots[step])
            qk_dots(tile + 1, s_slots[1 - step])
            pv_dots(tile - 1, p_slots[1 - step], a_slots[1 - step])
        return carry

    lax.fori_loop(0, i + 2, att_pair, 0)
    for h in range(N_HEADS):
        dr = slice(HEAD_DIM * h, HEAD_DIM * (h + 1))
        acc_ref[dr, :] = acc_ref[dr, :] / l_ref[h]
    o_ref[0] = jnp.transpose(acc_ref[...]).astype(o_ref.dtype)


def _dsa_prompt(qb_, iqb, ikw, kb, vt, ikd, *, qb, topk):
    B, S, _ = qb_.shape
    mxu = qb_.dtype
    blk = lambda width: pl.BlockSpec((1, qb, width), lambda b, i: (b, i, 0))
    full = lambda width: pl.BlockSpec((1, S, width), lambda b, i: (b, 0, 0))
    kern = functools.partial(_dsa_prompt_kernel, topk=topk, qb=qb, seq=S)
    return pl.pallas_call(
        kern, grid=(B, S // qb),
        in_specs=[blk(A_WIDTH), blk(A_WIDTH), blk(LANES), full(A_WIDTH),
                  pl.BlockSpec((A_WIDTH, S), lambda b, i: (0, b)), full(LANES)],
        out_specs=blk(A_WIDTH),
        out_shape=jax.ShapeDtypeStruct((B, S, A_WIDTH), mxu),
        scratch_shapes=[
            pltpu.VMEM((S + qb, qb), I32),
            pltpu.VMEM((S + qb, qb), I16), pltpu.VMEM((S + qb, qb), I16), pltpu.VMEM((S + qb, qb), I16),
            pltpu.VMEM((N_HEADS, qb, LANES), mxu),
            pltpu.VMEM((N_HEADS, qb, LANES), mxu),
            pltpu.VMEM((IDX_HEADS, qb), F32),
            pltpu.VMEM((N_HEADS, KEY_TILE, qb), F32),
            pltpu.VMEM((N_HEADS, KEY_TILE, qb), F32),
            pltpu.VMEM((N_HEADS, KEY_TILE, qb), mxu),
            pltpu.VMEM((N_HEADS, KEY_TILE, qb), mxu),
            pltpu.VMEM((N_HEADS, 1, qb), F32),
            pltpu.VMEM((N_HEADS, 1, qb), F32),
            pltpu.VMEM((A_WIDTH, qb), F32),
            pltpu.VMEM((N_HEADS, 1, qb), F32),
            pltpu.VMEM((N_HEADS, 1, qb), F32),
            pltpu.VMEM((1, qb), I32),
            pltpu.VMEM((1, qb), I32),
        ],
        compiler_params=_cparams(2), name="dsa_prompt",
    )(qb_, iqb, ikw, kb, vt, ikd)


PAGES_PER_STEP = 16


def _head_sum(s):
    return jnp.concatenate(
        [jnp.sum(s[IDX_HEADS * t:IDX_HEADS * (t + 1)], axis=0, keepdims=True) for t in range(s.shape[0] // IDX_HEADS)],
        axis=0)


def _sample_scores_kernel(pt_ref, iq_ref, w_ref, ikn_ref, *rest, n_steps):
    pages = rest[:PAGES_PER_STEP]
    sc_ref, scn_ref = rest[PAGES_PER_STEP:]
    c = pl.program_id(1)
    iq = iq_ref[0]
    w = w_ref[0]

    def scores(ik):
        s = jnp.dot(iq, ik.astype(iq.dtype), preferred_element_type=F32)
        return _head_sum(jnp.maximum(s, 0.0) * w)

    sc_ref[0] = scores(jnp.concatenate([r[...].astype(iq.dtype) for r in pages], axis=1))

    @pl.when(c == n_steps - 1)
    def _():
        s = scores(ikn_ref[0])
        col = lax.broadcasted_iota(I32, s.shape, 1)
        row = lax.broadcasted_iota(I32, s.shape, 0)
        scn_ref[0] = jnp.where(col <= row, s, -jnp.inf)


def _page_specs(layer, rows, page):
    def make(r):
        def imap(b, c, pt):
            return (layer, pt[b, c * PAGES_PER_STEP + r], 0, 0)
        return pl.BlockSpec((None, None, rows, page), imap)
    return [make(r) for r in range(PAGES_PER_STEP)]


def _sample_scores(page_table, iq_rows, w_rows, ik_new, cache_ik, *, layer):
    DB, n_pages = page_table.shape
    page = cache_ik.shape[3]
    n_steps = n_pages // PAGES_PER_STEP
    rows = iq_rows.shape[1]
    nt = rows // IDX_HEADS
    step_cols = PAGES_PER_STEP * page
    grid_spec = pltpu.PrefetchScalarGridSpec(
        num_scalar_prefetch=1, grid=(DB, n_steps),
        in_specs=[
            pl.BlockSpec((1, rows, IDX_DIM), lambda b, c, pt: (b, 0, 0)),
            pl.BlockSpec((1, rows, 1), lambda b, c, pt: (b, 0, 0)),
            pl.BlockSpec((1, IDX_DIM, page), lambda b, c, pt: (b, 0, 0)),
        ] + _page_specs(layer, IDX_DIM, page),
        out_specs=[pl.BlockSpec((1, nt, step_cols), lambda b, c, pt: (b, 0, c)),
                   pl.BlockSpec((1, nt, page), lambda b, c, pt: (b, 0, 0))],
    )
    kern = functools.partial(_sample_scores_kernel, n_steps=n_steps)
    return pl.pallas_call(
        kern, grid_spec=grid_spec,
        out_shape=[jax.ShapeDtypeStruct((DB, nt, n_pages * page), F32), jax.ShapeDtypeStruct((DB, nt, page), F32)],
        compiler_params=_cparams(2), name="sample_scores",
    )(page_table, iq_rows, w_rows, ik_new, *([cache_ik] * PAGES_PER_STEP))


def _score_key(sc):
    return jnp.where(sc == -jnp.inf, jnp.int32(INT_MIN), _sortable_key(sc))


def _sample_threshold_kernel(sc_ref, scn_ref, thr_ref, cut_ref, kint_ref, t_ref, c_ref, *, topk, tile):
    rows, past = sc_ref.shape
    page = scn_ref.shape[1]
    cols = kint_ref.shape[1]
    kint_ref[:, 0:past] = _score_key(sc_ref[...])
    kint_ref[:, past:past + page] = _score_key(scn_ref[...])
    kint_ref[:, past + page:] = jnp.full((rows, cols - past - page), INT_MIN, I32)
    _topk_threshold(kint_ref, t_ref, c_ref, cols // tile, tile, rows, topk, cols)
    thr_ref[...] = jnp.broadcast_to(t_ref[...], thr_ref.shape)
    cut_ref[...] = jnp.broadcast_to(c_ref[...], cut_ref.shape)


def _sample_threshold(sc, sc_new, *, topk):
    rows, past = sc.shape
    page = sc_new.shape[1]
    tile = 2 * LANES
    cols = pl.cdiv(past + page, tile) * tile
    kern = functools.partial(_sample_threshold_kernel, topk=topk, tile=tile)
    return pl.pallas_call(
        kern, grid=(1,),
        in_specs=[_const_spec(sc.shape), _const_spec(sc_new.shape)],
        out_specs=[_const_spec((rows, LANES)), _const_spec((rows, LANES))],
        out_shape=[jax.ShapeDtypeStruct((rows, LANES), I32)] * 2,
        scratch_shapes=[pltpu.VMEM((rows, cols), I32), pltpu.VMEM((rows, 1), I32), pltpu.VMEM((rows, 1), I32)],
        compiler_params=_cparams(1), name="sample_threshold",
    )(sc, sc_new)


def _sample_attn_kernel(pt_ref, q_ref, sc_ref, scn_ref, thr_ref, cut_ref, kn_ref, vn_ref, hm_ref, *rest, n_steps):
    kp = rest[:PAGES_PER_STEP]
    vp = rest[PAGES_PER_STEP:2 * PAGES_PER_STEP]
    o_ref = rest[2 * PAGES_PER_STEP]
    m_ref, l_ref, acc_ref = rest[2 * PAGES_PER_STEP + 1:]
    c = pl.program_id(1)
    q = q_ref[0]
    mxu = q.dtype
    nt = sc_ref.shape[1]
    step_cols = sc_ref.shape[2]

    @pl.when(c == 0)
    def _():
        m_ref[...] = jnp.full(m_ref.shape, NEG_BIG, F32)
        l_ref[...] = jnp.zeros_like(l_ref)
        acc_ref[...] = jnp.zeros_like(acc_ref)

    def rows_th(x):
        return jnp.concatenate([jnp.broadcast_to(x[t:t + 1], (N_HEADS, x.shape[1])) for t in range(nt)], axis=0)

    def update(keys, vals, sc, col0):
        key = _score_key(sc)
        col = col0 + lax.broadcasted_iota(I32, key.shape, 1)
        sel = rows_th(_selected(key, col, thr_ref[0, :, 0:1], cut_ref[0, :, 0:1]).astype(F32)) > 0.5
        kcat = jnp.concatenate([kk.astype(mxu) for kk in keys], axis=1)
        s = jnp.dot(q, kcat, preferred_element_type=F32)
        s = jnp.where(sel, s, NEG_BIG)
        m_old = m_ref[...]
        m_new = jnp.maximum(m_old, jnp.max(s, axis=1, keepdims=True))
        alpha = jnp.exp2(m_old - m_new)
        pm = jnp.exp2(s - m_new)
        l_ref[...] = alpha * l_ref[...] + jnp.sum(pm, axis=1, keepdims=True)
        m_ref[...] = m_new
        vcat = jnp.concatenate([vv.astype(mxu) for vv in vals], axis=1)
        pv = lax.dot_general(pm.astype(mxu), vcat, _NT, preferred_element_type=F32)
        acc_ref[...] = alpha * acc_ref[...] + pv

    update([r[...] for r in kp], [r[...] for r in vp], sc_ref[0], c * step_cols)

    @pl.when(c == n_steps - 1)
    def _():
        update([kn_ref[0]], [vn_ref[0]], scn_ref[0], n_steps * step_cols)
        out = acc_ref[...] / l_ref[...] * hm_ref[...]
        o_ref[0] = _head_sum(out).astype(o_ref.dtype)


def _sample_attn(page_table, q_rows, sc, sc_new, thr, cut, k_new, v_new, head_mask, cache_k, cache_v, *, layer):
    DB, n_pages = page_table.shape
    page = cache_k.shape[3]
    n_steps = n_pages // PAGES_PER_STEP
    rows = q_rows.shape[1]
    nt = rows // N_HEADS
    step_cols = PAGES_PER_STEP * page
    per_b = lambda shape: pl.BlockSpec((1,) + shape, lambda b, c, pt: (b, 0, 0))
    grid_spec = pltpu.PrefetchScalarGridSpec(
        num_scalar_prefetch=1, grid=(DB, n_steps),
        in_specs=[
            per_b((rows, A_WIDTH)),
            pl.BlockSpec((1, nt, step_cols), lambda b, c, pt: (b, 0, c)),
            per_b((nt, page)), per_b((nt, LANES)), per_b((nt, LANES)),
            per_b((A_WIDTH, page)), per_b((A_WIDTH, page)),
            pl.BlockSpec((rows, A_WIDTH), lambda b, c, pt: (0, 0)),
        ] + _page_specs(layer, A_WIDTH, page) * 2,
        out_specs=per_b((nt, A_WIDTH)),
        scratch_shapes=[pltpu.VMEM((rows, 1), F32), pltpu.VMEM((rows, 1), F32), pltpu.VMEM((rows, A_WIDTH), F32)],
    )
    kern = functools.partial(_sample_attn_kernel, n_steps=n_steps)
    return pl.pallas_call(
        kern, grid_spec=grid_spec,
        out_shape=jax.ShapeDtypeStruct((DB, nt, A_WIDTH), q_rows.dtype),
        compiler_params=_cparams(2), name="sample_attn",
    )(page_table, q_rows, sc, sc_new, thr, cut, k_new, v_new, head_mask,
      *([cache_k] * PAGES_PER_STEP), *([cache_v] * PAGES_PER_STEP))


def _shift_rows(e, k):
    return pltpu.roll(e, k, 0)


def _merge_kernel(x_ref, att_ref, u_ref, vv_ref, xc_ref, halo_ref, cnt_ref,
                  g1_ref, wgt_ref, wa_ref, wb_ref, wc_ref, wout_ref, ws_ref, btab_ref, wpool_ref, pscale_ref,
                  o_ref, *, row_stride, tiles_per_seq, zero_first_halo):
    mxu = wgt_ref.dtype
    tm = x_ref.shape[0]
    x = x_ref[...]
    xn = _rms(x, g1_ref[...]).astype(mxu)
    d = x.shape[1]

    def gate(n):
        return jax.nn.sigmoid(jnp.dot(xn, wgt_ref[:, n * d:(n + 1) * d], preferred_element_type=F32))

    merged = gate(0) * jnp.dot(att_ref[...], wa_ref[...], preferred_element_type=F32)

    lane = lax.broadcasted_iota(I32, (SGU_CHUNK, LANES), 1)
    lo = lane < (SGU_WIDTH // SGU_GROUPS)
    sgo = []
    for c in range(tm // SGU_CHUNK):
        rs = slice(c * SGU_CHUNK, (c + 1) * SGU_CHUNK)
        vvb = vv_ref[rs, :].astype(mxu)
        mix = []
        for p in range(SGU_GROUPS // 2):
            pair = vvb[:, LANES * p:LANES * (p + 1)]
            r0 = jnp.dot(ws_ref[2 * p], pair, preferred_element_type=F32)
            r1 = jnp.dot(ws_ref[2 * p + 1], pair, preferred_element_type=F32)
            mix.append(jnp.where(lo, r0, r1))
        sgo.append(u_ref[rs, :] * (jnp.concatenate(mix, axis=1) + btab_ref[...]))
    sgo = jnp.concatenate(sgo, axis=0).astype(mxu)
    merged = merged + gate(1) * jnp.dot(sgo, wb_ref[...], preferred_element_type=F32)

    xc = xc_ref[...]
    halo = halo_ref[...]
    if zero_first_halo:
        first = (pl.program_id(0) % tiles_per_seq) == 0
        halo = jnp.where(first, jnp.zeros_like(halo), halo)
    hp = halo.shape[0]
    e = jnp.concatenate([halo, xc], axis=0)
    s1 = e + _shift_rows(e, row_stride)
    s2 = s1 + _shift_rows(s1, 2 * row_stride)
    s3 = s2 + _shift_rows(s2, 4 * row_stride)
    s4 = s3 + _shift_rows(s3, 8 * row_stride)
    gd = POOL_WIDTH // POOL_GROUPS
    wsum = jnp.concatenate([s[hp:, g * gd:(g + 1) * gd] for g, s in enumerate((s1, s2, s3, s4))], axis=1)
    pooled = (wsum / cnt_ref[...] - xc).astype(mxu)
    po = (jnp.dot(pooled, wpool_ref[...], preferred_element_type=F32) * pscale_ref[...]).astype(mxu)
    merged = merged + gate(2) * jnp.dot(po, wc_ref[...], preferred_element_type=F32)

    o_ref[...] = x + jnp.dot(merged.astype(mxu), wout_ref[...], preferred_element_type=F32)


def _merge(x, att, u, vv, xc, halo, cnt, g1, wgt, wa, wb, wc, wout, ws, btab, wpool, pscale,
           *, tm, row_stride, tiles_per_seq, halo_rows, halo_from_xc, cnt_tiles):
    T, D = x.shape
    n = T // tm
    row = lambda width: pl.BlockSpec((tm, width), lambda i: (i, 0))
    if halo_from_xc:
        per = tm // halo_rows
        halo_spec = pl.BlockSpec((halo_rows, xc.shape[1]), lambda i: (jnp.maximum(i * per - 1, 0), 0))
    else:
        halo_spec = _const_spec(halo.shape)
    consts = [g1, wgt, wa, wb, wc, wout, ws, btab, wpool, pscale]
    kern = functools.partial(_merge_kernel, row_stride=row_stride, tiles_per_seq=tiles_per_seq,
                             zero_first_halo=halo_from_xc)
    return pl.pallas_call(
        kern, grid=(n,),
        in_specs=[row(D), row(att.shape[1]), row(u.shape[1]), row(vv.shape[1]), row(xc.shape[1]), halo_spec,
                  pl.BlockSpec((tm, cnt.shape[1]), lambda i: (i % cnt_tiles, 0))]
        + [_const_spec(c.shape) for c in consts],
        out_specs=row(D), out_shape=jax.ShapeDtypeStruct((T, D), F32),
        compiler_params=_cparams(1), name="merge",
    )(x, att, u, vv, xc, halo, cnt, *consts)


FFN_CHUNK = 256


def _ffn_kernel(x_ref, g2_ref, wa_ref, wu_ref, cw_ref, cb_ref, wd_ref, halo_ref, o_ref, tail_ref, carry_ref,
                *, row_stride, tiles_per_seq, use_carry):
    mxu = wa_ref.dtype
    x = x_ref[...]
    tm = x.shape[0]
    xn = _rms(x, g2_ref[...]).astype(mxu)
    dff = wa_ref.shape[1]
    tail = tail_ref.shape[1]
    acc = jnp.zeros(x.shape, F32)
    if use_carry:
        first = (pl.program_id(0) % tiles_per_seq) == 0
    for c0 in range(0, dff, FFN_CHUNK):
        cs = slice(c0, c0 + FFN_CHUNK)
        a = jnp.dot(xn, wa_ref[:, cs], preferred_element_type=F32)
        up = jnp.dot(xn, wu_ref[:, cs], preferred_element_type=F32)
        if use_carry:
            halo = jnp.where(first, 0.0, carry_ref[:, cs])
            carry_ref[:, cs] = a[tm - carry_ref.shape[0]:, :]
        else:
            halo = halo_ref[:, cs]
        hp = halo.shape[0]
        e = jnp.concatenate([halo, a], axis=0)
        p1 = _shift_rows(e, row_stride)[hp:]
        p2 = _shift_rows(e, 2 * row_stride)[hp:]
        conv = p2 * cw_ref[0:1, cs] + p1 * cw_ref[1:2, cs] + a * cw_ref[2:3, cs] + cb_ref[:, cs]
        act = (jax.nn.silu(conv) * up).astype(mxu)
        acc = acc + jnp.dot(act, wd_ref[cs, :], preferred_element_type=F32)
        tail_ref[0, :, cs] = a[tm - tail:, :]
    o_ref[...] = x + acc


def _ffn(x, g2, wa, wu, cw, cb, wd, halo, *, tm, row_stride, tiles_per_seq, use_carry, tail):
    T, D = x.shape
    n = T // tm
    dff = wa.shape[1]
    row = pl.BlockSpec((tm, D), lambda i: (i, 0))
    consts = [g2, wa, wu, cw, cb, wd, halo]
    kern = functools.partial(_ffn_kernel, row_stride=row_stride, tiles_per_seq=tiles_per_seq, use_carry=use_carry)
    return pl.pallas_call(
        kern, grid=(n,),
        in_specs=[row] + [_const_spec(c.shape) for c in consts],
        out_specs=[row, pl.BlockSpec((1, tail, dff), lambda i: (i, 0, 0))],
        out_shape=[jax.ShapeDtypeStruct((T, D), F32), jax.ShapeDtypeStruct((n, tail, dff), F32)],
        scratch_shapes=[pltpu.VMEM((SUBLANES, dff), F32)],
        compiler_params=_cparams(1), name="ffn",
    )(x, *consts)


def _pack_w_in(w_in):
    d = w_in.shape[0]
    o = 0
    q, k, v = (w_in[:, o + i * A_WIDTH:o + (i + 1) * A_WIDTH] for i in range(3))
    o += 3 * A_WIDTH
    iq = w_in[:, o:o + IDX_HEADS * IDX_DIM]
    o += IDX_HEADS * IDX_DIM
    ik = w_in[:, o:o + IDX_DIM]
    o += IDX_DIM
    iw = w_in[:, o:o + IDX_HEADS]
    o += IDX_HEADS
    sg = w_in[:, o:o + 2 * SGU_WIDTH]
    o += 2 * SGU_WIDTH
    xc = w_in[:, o:o + POOL_WIDTH]
    o += POOL_WIDTH
    gt = w_in[:, o:]
    pad = jnp.zeros((d, LANES - IDX_DIM - IDX_HEADS), w_in.dtype)
    packed = jnp.concatenate([q, k, v, iq, ik, iw, pad, ik, ik, sg, xc], axis=1)
    assert packed.shape[1] == PROJ_COLS
    return packed, gt


def _rope_tables(pos):
    posf = np.asarray(pos, np.float64)[:, None]

    def cs(half):
        inv = ROPE_THETA ** (-np.arange(half, dtype=np.float64) / half)
        ang = posf * inv[None, :]
        return np.cos(ang), np.sin(ang)

    t = posf.shape[0]
    c32, s32 = cs(HEAD_DIM // 2)
    cqk = np.tile(np.concatenate([c32, c32], axis=1), (1, 2))
    sqk = np.tile(np.concatenate([-s32, s32], axis=1), (1, 2))
    c16, s16 = cs(IDX_ROPE // 2)
    rest = IDX_DIM - IDX_ROPE
    ci = np.concatenate([c16, c16, np.ones((t, rest))], axis=1)
    si = np.concatenate([-s16, s16, np.zeros((t, rest))], axis=1)
    ciq, siq = np.tile(ci, (1, 2)), np.tile(si, (1, 2))
    wpad = LANES - IDX_DIM - IDX_HEADS
    cikw = np.concatenate([ci, np.full((t, IDX_HEADS), IDX_HEADS ** -0.5), np.zeros((t, wpad))], axis=1)
    sikw = np.concatenate([si, np.zeros((t, LANES - IDX_DIM))], axis=1)
    return jnp.asarray(np.concatenate([cqk, sqk, ciq, siq, cikw, sikw], axis=1).astype(np.float32))


def _block_diag(blocks):
    n = len(blocks)
    r, c = blocks[0].shape
    out = jnp.zeros((n * r, n * c), blocks[0].dtype)
    for i, b in enumerate(blocks):
        out = out.at[i * r:(i + 1) * r, i * c:(i + 1) * c].set(b)
    return out


def _pool_counts(pos):
    gd = POOL_WIDTH // POOL_GROUPS
    pos = np.asarray(pos)
    cols = [np.broadcast_to(np.minimum(pos + 1, w).astype(np.float32)[:, None], (pos.shape[0], gd))
            for w in POOL_WINDOWS]
    return jnp.asarray(np.concatenate(cols, axis=1))


def _head_segments(dtype):
    return jnp.asarray(np.kron(np.eye(N_HEADS, dtype=np.float32), np.ones((HEAD_DIM, HEAD_DIM), np.float32)), dtype)


def _layer_weights(l, norm1_g, w_in, q_norm_g, k_norm_g, sgu_w, sgu_b, sgu_norm_g, pool_w, pool_scale,
                   w_br_a, w_br_b, w_br_c, w_out, norm2_g, w_ff_in, ff_conv_w, ff_conv_b, w_ff_down):
    mxu = MXU_DTYPE
    packed, gt = _pack_w_in(w_in[l])
    dff = w_ff_in.shape[2] // 2
    tril = jnp.tril(jnp.ones((SGU_CHUNK, SGU_CHUNK), bool))
    return dict(
        g1=norm1_g[l][None, :], w_proj=packed.astype(mxu), w_gt=gt.astype(mxu),
        qg=jnp.tile(q_norm_g[l], N_HEADS)[None, :], kg=jnp.tile(k_norm_g[l], N_HEADS)[None, :],
        sgg=sgu_norm_g[l][None, :],
        ws=jnp.where(tril[None], sgu_w[l], 0.0),
        sgu_b=sgu_b[l],
        wpool=_block_diag([pool_w[l, g] for g in range(POOL_GROUPS)]).astype(mxu),
        pscale=pool_scale[l][None, :],
        wa=w_br_a[l].astype(mxu), wb=w_br_b[l].astype(mxu), wc=w_br_c[l].astype(mxu), wout=w_out[l].astype(mxu),
        g2=norm2_g[l][None, :],
        w_ffa=w_ff_in[l, :, :dff].astype(mxu), w_ffu=w_ff_in[l, :, dff:].astype(mxu),
        cw=jnp.pad(ff_conv_w[l], ((0, SUBLANES - CONV_W), (0, 0))), cb=ff_conv_b[l][None, :],
        w_ffd=w_ff_down[l].astype(mxu),
    )


def _sgu_bias_table(sgu_b, t_of_row):
    gd = SGU_WIDTH // SGU_GROUPS
    return jnp.repeat(jnp.transpose(sgu_b)[t_of_row], gd, axis=1)


def _prompt_layer(x, lw, tabs, cnt, l, depth, state, *, B, S):
    mxu = MXU_DTYPE
    T = B * S
    tm_proj, tm_merge, tm_ffn, qb = 512, 256, 512, 256
    pr = _proj(x, lw["g1"], lw["w_proj"], lw["qg"], lw["kg"], lw["sgg"], tabs, _head_segments(mxu),
               tm=tm_proj, tab_tiles=S // tm_proj, stack=(l, depth, B, S, state))
    u, vv, xc = pr["u"], pr["vv"], pr["xc"]
    r3 = lambda a: a.reshape(B, S, a.shape[1])
    topk = min(TOPK_MAX, S // 4)
    att = _dsa_prompt(r3(pr["qb"]), r3(pr["iqb"]), r3(pr["ikw"]), r3(pr["kb"]), pr["vt"], r3(pr["ikd"]),
                      qb=qb, topk=topk).reshape(T, A_WIDTH)
    btab = _sgu_bias_table(lw["sgu_b"], jnp.arange(SGU_CHUNK))
    x1 = _merge(x, att, u, vv, xc, xc, cnt, lw["g1"], lw["w_gt"], lw["wa"], lw["wb"], lw["wc"], lw["wout"],
                lw["ws"].astype(mxu), btab, lw["wpool"], lw["pscale"],
                tm=tm_merge, row_stride=1, tiles_per_seq=S // tm_merge, halo_rows=16, halo_from_xc=True,
                cnt_tiles=S // tm_merge)
    dff = lw["w_ffa"].shape[1]
    x2, tails = _ffn(x1, lw["g2"], lw["w_ffa"], lw["w_ffu"], lw["cw"], lw["cb"], lw["w_ffd"],
                     jnp.zeros((SUBLANES, dff), F32),
                     tm=tm_ffn, row_stride=1, tiles_per_seq=S // tm_ffn, use_carry=True, tail=SUBLANES)
    n_t = S // tm_ffn
    ff_state = tails.reshape(B, n_t, SUBLANES, dff)[:, -1, SUBLANES - (CONV_W - 1):, :]
    pool_state = xc.reshape(B, S, POOL_WIDTH)[:, S - POOL_STATE:, :]
    return x2, (pr["k"], pr["v"], pr["ik"]), (pool_state, ff_state)


def _sample_layer(x, lw, tabs, cnt, l, cache_k, cache_v, cache_ik, state_pool, state_ffn, page_table, *, DB, TS):
    mxu = MXU_DTYPE
    T = TS * DB
    page = cache_ik.shape[3]
    past = page_table.shape[1] * page
    pr = _proj(x, lw["g1"], lw["w_proj"], lw["qg"], lw["kg"], lw["sgg"], tabs, _head_segments(mxu),
               tm=T, tab_tiles=1)
    qb_, k, kb, v, vt, iqb, ikw, ikd, u, vv, xc = (
        pr[n] for n in ("qb", "k", "kb", "v", "vt", "iqb", "ikw", "ikd", "u", "vv", "xc"))
    bm = lambda a: jnp.transpose(a.reshape(TS, DB, a.shape[1]), (1, 0, 2))

    iq_rows = bm(iqb).reshape(DB, TS * IDX_HEADS, IDX_DIM)
    w_rows = bm(ikw)[:, :, IDX_DIM:IDX_DIM + IDX_HEADS].reshape(DB, TS * IDX_HEADS, 1)
    new_t = lambda a: jnp.pad(jnp.transpose(a, (0, 2, 1)), ((0, 0), (0, 0), (0, page - TS)))
    ik_new = new_t(bm(ikd)[:, :, :IDX_DIM])
    sc, sc_new = _sample_scores(page_table, iq_rows, w_rows, ik_new, cache_ik, layer=l)
    topk = min(TOPK_MAX, (past + TS) // 4)
    thr, cut = _sample_threshold(sc.reshape(DB * TS, past), sc_new.reshape(DB * TS, page), topk=topk)
    hm = jnp.asarray(np.repeat(np.eye(N_HEADS, dtype=np.float32), HEAD_DIM, axis=1))
    q_rows = (bm(qb_)[:, :, None, :] * hm[None, None].astype(mxu)).reshape(DB, TS * N_HEADS, A_WIDTH)
    att = _sample_attn(page_table, q_rows, sc, sc_new, thr.reshape(DB, TS, LANES), cut.reshape(DB, TS, LANES),
                       new_t(bm(kb)), new_t(bm(jnp.transpose(vt))), jnp.tile(hm, (TS, 1)), cache_k, cache_v, layer=l)
    att = jnp.transpose(att, (1, 0, 2)).reshape(T, A_WIDTH)

    eye = jnp.eye(DB, dtype=F32)
    ws = jnp.stack([jnp.kron(lw["ws"][g, :TS, :TS], eye) for g in range(SGU_GROUPS)])
    pc = SGU_CHUNK - T
    ws = jnp.pad(ws, ((0, 0), (0, pc), (0, pc)))
    btab = _sgu_bias_table(lw["sgu_b"], jnp.minimum(jnp.arange(SGU_CHUNK) // DB, SGU_CHUNK - 1))
    halo = jnp.concatenate([jnp.zeros((DB, POOL_WIDTH), F32),
                            jnp.transpose(state_pool[l], (1, 0, 2)).reshape(POOL_STATE * DB, POOL_WIDTH)], axis=0)
    x1 = _merge(x, att, u, vv, xc, halo, cnt, lw["g1"], lw["w_gt"], lw["wa"], lw["wb"], lw["wc"], lw["wout"],
                ws.astype(mxu), btab, lw["wpool"], lw["pscale"],
                tm=T, row_stride=DB, tiles_per_seq=1, halo_rows=halo.shape[0], halo_from_xc=False, cnt_tiles=1)
    dff = lw["w_ffa"].shape[1]
    ff_halo = jnp.transpose(state_ffn[l], (1, 0, 2)).reshape((CONV_W - 1) * DB, dff)
    tail = (CONV_W - 1) * DB
    x2, tails = _ffn(x1, lw["g2"], lw["w_ffa"], lw["w_ffu"], lw["cw"], lw["cb"], lw["w_ffd"], ff_halo,
                     tm=T, row_stride=DB, tiles_per_seq=1, use_carry=False, tail=tail)
    ff_state = jnp.transpose(tails.reshape(CONV_W - 1, DB, dff), (1, 0, 2))
    pool_state = jnp.concatenate([state_pool[l], bm(xc)], axis=1)[:, -POOL_STATE:, :]
    hd = lambda a: bm(a).reshape(DB, TS, N_HEADS, HEAD_DIM)
    return x2, (hd(k), hd(v), bm(ikw)[:, :, :IDX_DIM], bm(vv), pool_state, ff_state)


def kernel(x_prompt, x_sample, cache_k, cache_v, cache_idx_k, state_pool, state_ffn_conv, page_table, norm1_g, w_in, q_norm_g, k_norm_g, sgu_w, sgu_b, sgu_norm_g, pool_w, pool_scale, w_br_a, w_br_b, w_br_c, w_out, norm2_g, w_ff_in, ff_conv_w, ff_conv_b, w_ff_down):
    weights = (norm1_g, w_in, q_norm_g, k_norm_g, sgu_w, sgu_b, sgu_norm_g, pool_w, pool_scale,
               w_br_a, w_br_b, w_br_c, w_out, norm2_g, w_ff_in, ff_conv_w, ff_conv_b, w_ff_down)
    B, S, D = x_prompt.shape
    DB, TS, _ = x_sample.shape
    depth = w_in.shape[0]
    page = cache_idx_k.shape[2]
    past = page_table.shape[1] * page
    pool = cache_k.shape[1]
    cache_kt = jnp.transpose(cache_k, (0, 1, 3, 4, 2)).reshape(depth, pool, A_WIDTH, page)
    cache_vt = jnp.transpose(cache_v, (0, 1, 3, 4, 2)).reshape(depth, pool, A_WIDTH, page)
    cache_ikt = jnp.transpose(cache_idx_k, (0, 1, 3, 2))
    assert S % 512 == 0 and S >= POOL_STATE and TS * DB == SGU_CHUNK and TS >= CONV_W - 1 and TS <= page
    assert past % SGU_CHUNK == 0 and page_table.shape[1] % PAGES_PER_STEP == 0

    pos_p = np.arange(S)
    pos_s = past + np.repeat(np.arange(TS), DB)
    tabs_p, tabs_s = _rope_tables(pos_p), _rope_tables(pos_s)
    cnt_p, cnt_s = _pool_counts(pos_p), _pool_counts(pos_s)

    xp = x_prompt.reshape(B * S, D)
    xs = jnp.transpose(x_sample, (1, 0, 2)).reshape(TS * DB, D)
    outs_p, outs_s = [], []
    kv_state = None
    for l in range(depth):
        lw = _layer_weights(l, *weights)
        xp, kv_state, st_p = _prompt_layer(xp, lw, tabs_p, cnt_p, l, depth, kv_state, B=B, S=S)
        xs, st_s = _sample_layer(xs, lw, tabs_s, cnt_s, l, cache_kt, cache_vt, cache_ikt, state_pool,
                                 state_ffn_conv, page_table, DB=DB, TS=TS)
        outs_p.append(st_p)
        outs_s.append(st_s)
    stack = lambda outs, i: jnp.stack([o[i] for o in outs])
    y_p = xp.reshape(B, S, D)
    y_s = jnp.transpose(xs.reshape(TS, DB, D), (1, 0, 2))
    k_all, v_all, ik_all = kv_state
    heads = lambda a: jnp.transpose(a.reshape(depth, B, N_HEADS, HEAD_DIM, S), (0, 1, 4, 2, 3))
    return (y_p, y_s,
            heads(k_all), heads(v_all), jnp.transpose(ik_all, (0, 1, 3, 2)), stack(outs_p, 0), stack(outs_p, 1),
            stack(outs_s, 0), stack(outs_s, 1), stack(outs_s, 2), stack(outs_s, 3), stack(outs_s, 4), stack(outs_s, 5))
```

```python
import functools

import jax
import jax.numpy as jnp
import numpy as np
from jax import lax
from jax.experimental import pallas as pl
from jax.experimental.pallas import tpu as pltpu

MXU_DTYPE = jnp.bfloat16
F32 = jnp.float32
I32 = jnp.int32

N_HEADS = 8
HEAD_DIM = 64
A_WIDTH = N_HEADS * HEAD_DIM
IDX_HEADS = 8
IDX_DIM = 64
IDX_ROPE = 32
TOPK_MAX = 256
SGU_GROUPS = 4
SGU_WIDTH = 256
SGU_CHUNK = 128
POOL_GROUPS = 4
POOL_WIDTH = 256
POOL_WINDOWS = (2, 4, 8, 16)
POOL_STATE = 15
CONV_W = 3
ROPE_THETA = 10000.0
EPS = 1e-6

LANES = 128
SUBLANES = 8
INT_MIN = -2 ** 31
NEG_BIG = -1e30
Q_SCALE = HEAD_DIM ** -0.5 * float(np.log2(np.e))
PROJ_COLS = 3072
VMEM_LIMIT = 56 * 1024 * 1024

_NT = (((1,), (1,)), ((), ()))


def _cparams(n_axes):
    return pltpu.CompilerParams(dimension_semantics=("arbitrary",) * n_axes, vmem_limit_bytes=VMEM_LIMIT)


def _const_spec(shape):
    nd = len(shape)
    return pl.BlockSpec(shape, lambda *_: (0,) * nd)


def _rope128(x, cos, sin, half):
    lane = lax.broadcasted_iota(I32, x.shape, 1)
    lo = (lane & (HEAD_DIM - 1)) < half
    rot = jnp.where(lo, pltpu.roll(x, LANES - half, 1), pltpu.roll(x, half, 1))
    return x * cos + rot * sin


def _head_rms(x, seg, g):
    ss = jnp.dot((x * x).astype(seg.dtype), seg, preferred_element_type=F32)
    return x * lax.rsqrt(ss * (1.0 / HEAD_DIM) + EPS) * g


def _rms(x, g):
    return x * lax.rsqrt(jnp.mean(x * x, axis=-1, keepdims=True) + EPS) * g


def _proj_kernel(x_ref, g1_ref, w_ref, qg_ref, kg_ref, sgg_ref, tab_ref, seg_ref, *rest, stacked, n_alias):
    rest = rest[n_alias:]
    if stacked:
        qb_ref, k_ref, kb_ref, v_ref, vt_ref, iqb_ref, ikw_ref, ik_ref, ikd_ref, u_ref, vv_ref, xc_ref = rest
    else:
        qb_ref, k_ref, kb_ref, v_ref, vt_ref, iqb_ref, ikw_ref, ikd_ref, u_ref, vv_ref, xc_ref = rest
    mxu = w_ref.dtype
    xn = _rms(x_ref[...], g1_ref[...]).astype(mxu)

    def mm(c0, c1):
        return jnp.dot(xn, w_ref[:, c0:c1], preferred_element_type=F32)

    cqk, sqk = tab_ref[:, 0:128], tab_ref[:, 128:256]
    ciq, siq = tab_ref[:, 256:384], tab_ref[:, 384:512]
    cikw, sikw = tab_ref[:, 512:640], tab_ref[:, 640:768]
    seg = seg_ref[...]
    half_qk = HEAD_DIM // 2
    half_idx = IDX_ROPE // 2

    q = _head_rms(mm(0, 512), seg, qg_ref[...])
    for c in range(4):
        sl = slice(LANES * c, LANES * (c + 1))
        qb_ref[:, sl] = (_rope128(q[:, sl], cqk, sqk, half_qk) * Q_SCALE).astype(mxu)
    k = _head_rms(mm(512, 1024), seg, kg_ref[...])
    for c in range(4):
        sl = slice(LANES * c, LANES * (c + 1))
        kr = _rope128(k[:, sl], cqk, sqk, half_qk)
        if stacked:
            k_ref[sl, :] = jnp.transpose(kr)
        else:
            k_ref[:, sl] = kr
        kb_ref[:, sl] = kr.astype(mxu)
    v = mm(1024, 1536)
    vt = jnp.transpose(v)
    v_ref[...] = vt if stacked else v
    vt_ref[...] = vt.astype(mxu)
    iq = mm(1536, 2048)
    for c in range(4):
        sl = slice(LANES * c, LANES * (c + 1))
        iqb_ref[:, sl] = (_rope128(iq[:, sl], ciq, siq, half_idx) * (IDX_DIM ** -0.5)).astype(mxu)
    ikw = _rope128(mm(2048, 2176), cikw, sikw, half_idx)
    ikw_ref[...] = ikw
    if stacked:
        ik_ref[...] = jnp.transpose(ikw)[:IDX_DIM, :]
    ikd_ref[...] = _rope128(mm(2176, 2304), ciq, siq, half_idx).astype(mxu)
    sg = jax.nn.gelu(mm(2304, 2816))
    u_ref[...] = sg[:, :SGU_WIDTH]
    vv_ref[...] = _rms(sg[:, SGU_WIDTH:], sgg_ref[...])
    xc_ref[...] = mm(2816, 3072)


def _proj(x, g1, w, qg, kg, sgg, tab, seg, *, tm, tab_tiles, stack=None):
    T, D = x.shape
    mxu = w.dtype
    n = T // tm
    row = lambda width: pl.BlockSpec((tm, width), lambda i: (i, 0))
    natural = lambda width, dt: (jax.ShapeDtypeStruct((T, width), dt), row(width))
    outs = dict(
        qb=natural(A_WIDTH, mxu),
        k=natural(A_WIDTH, F32), kb=natural(A_WIDTH, mxu), v=natural(A_WIDTH, F32),
        vt=(jax.ShapeDtypeStruct((A_WIDTH, T), mxu), pl.BlockSpec((A_WIDTH, tm), lambda i: (0, i))),
        iqb=natural(IDX_HEADS * IDX_DIM, mxu),
        ikw=natural(LANES, F32),
        ik=None,
        ikd=natural(LANES, mxu),
        u=natural(SGU_WIDTH, F32), vv=natural(SGU_WIDTH, F32), xc=natural(POOL_WIDTH, F32),
    )
    in_specs = [
        row(D), _const_spec(g1.shape), _const_spec(w.shape), _const_spec(qg.shape), _const_spec(kg.shape),
        _const_spec(sgg.shape),
        pl.BlockSpec((tm, tab.shape[1]), lambda i: (i % tab_tiles, 0)),
        _const_spec(seg.shape),
    ]
    args = [x, g1, w, qg, kg, sgg, tab, seg]
    aliases = {}
    if stack is not None:
        layer, depth, B, S, state = stack
        tps = S // tm
        state_out = lambda feat: (jax.ShapeDtypeStruct((depth, B, feat, S), F32),
                                  pl.BlockSpec((None, None, feat, tm), lambda i: (layer, i // tps, 0, i % tps)))
        outs.update(k=state_out(A_WIDTH), v=state_out(A_WIDTH), ik=state_out(IDX_DIM))
        if state is not None:
            names = list(k for k, o in outs.items() if o is not None)
            for buf, name in zip(state, ("k", "v", "ik")):
                aliases[len(args)] = names.index(name)
                args.append(buf)
                in_specs.append(pl.BlockSpec(memory_space=pl.ANY))
    outs = {k: o for k, o in outs.items() if o is not None}
    kern = functools.partial(_proj_kernel, stacked=stack is not None, n_alias=len(aliases))
    res = pl.pallas_call(
        kern, grid=(n,), in_specs=in_specs, out_specs=[o[1] for o in outs.values()],
        out_shape=[o[0] for o in outs.values()], input_output_aliases=aliases,
        compiler_params=_cparams(1), name="proj",
    )(*args)
    return dict(zip(outs.keys(), res))


def _sortable_key(s):
    b = lax.bitcast_convert_type(s, I32)
    key = jnp.where(b < 0, b ^ jnp.int32(0x7FFFFFFF), b)
    return jnp.where(key == -1, 0, key)


def _fold_lanes(m):
    out = m[:, 0:LANES]
    for c in range(1, m.shape[1] // LANES):
        out = out + m[:, c * LANES:(c + 1) * LANES]
    return out


def _fold_sublanes(m):
    out = m[0:SUBLANES]
    for r in range(1, m.shape[0] // SUBLANES):
        out = out + m[r * SUBLANES:(r + 1) * SUBLANES]
    return out


I16 = jnp.int16
PACK16 = 2 * SUBLANES


def _fold_pack16(m):
    out = m[0:PACK16]
    for r in range(1, m.shape[0] // PACK16):
        out = out + m[r * PACK16:(r + 1) * PACK16]
    return out


def _tied_below_counter(kint_ref, iota_ref, eq_ref, t, n_tiles, tile, rows):
    iota_ref[0:tile, :] = lax.broadcasted_iota(I32, (tile, rows), 0).astype(I16)

    def build(j, carry):
        off = pl.multiple_of(j * tile, tile)
        eq_ref[pl.ds(off, tile), :] = jnp.where(kint_ref[pl.ds(off, tile), :] == t, 1, 0).astype(I16)
        return carry

    lax.fori_loop(0, n_tiles, build, 0)

    def counter(cand):
        def body(j, c):
            off = pl.multiple_of(j * tile, tile)
            local = jnp.clip(cand - off, 0, tile).astype(I16)
            m = jnp.where(iota_ref[0:tile, :] < local, eq_ref[pl.ds(off, tile), :], jnp.int16(0))
            return c + _fold_pack16(m)
        part = lax.fori_loop(0, n_tiles, body, jnp.zeros((PACK16, rows), I16))
        return jnp.sum(part.astype(F32), axis=0, keepdims=True)

    return counter


def _kth_largest_by_halves(khi_ref, klo_ref, low_ref, n_tiles, tile, rows, kf):
    def count_ge(ref, cand):
        c16 = cand.astype(I16)

        def body(j, c):
            off = pl.multiple_of(j * tile, tile)
            m = jnp.where(ref[pl.ds(off, tile), :] >= c16, jnp.int16(1), jnp.int16(0))
            return c + _fold_pack16(m)
        part = lax.fori_loop(0, n_tiles, body, jnp.zeros((PACK16, rows), I16))
        return jnp.sum(part.astype(F32), axis=0, keepdims=True)

    def bisect(ref):
        c0 = count_ge(ref, jnp.zeros((1, rows), I32))
        ok = c0 >= kf
        state = (jnp.where(ok, jnp.int32(0), jnp.int32(-32768)), jnp.where(ok, c0, 0.0))

        def bit_body(b, state):
            t, c = state
            cand = t + jnp.left_shift(jnp.int32(1), 14 - b)
            cnt = count_ge(ref, cand)
            ok = cnt >= kf
            return jnp.where(ok, cand, t), jnp.where(ok, cnt, c)

        return lax.fori_loop(0, 15, bit_body, state)

    hi, cnt_hi = bisect(khi_ref)
    hi16 = hi.astype(I16)

    def build(j, carry):
        off = pl.multiple_of(j * tile, tile)
        kh = khi_ref[pl.ds(off, tile), :]
        low_ref[pl.ds(off, tile), :] = jnp.where(
            kh > hi16, jnp.int16(32767), jnp.where(kh == hi16, klo_ref[pl.ds(off, tile), :], jnp.int16(-32768)))
        return carry

    lax.fori_loop(0, n_tiles, build, 0)
    lo, cnt = bisect(low_ref)
    return jnp.left_shift(hi, 16) + (lo + 32768), jnp.where(lo == -32768, cnt_hi, cnt)


def _topk_threshold(kint_ref, thr_ref, cut_ref, n_tiles, tile, rows, topk, n_cols, keys_axis=1, halves=None):
    stat_shape = (rows, 1) if keys_axis == 1 else (1, rows)

    def count(pred):
        def body(j, c):
            off = pl.multiple_of(j * tile, tile)
            if keys_axis == 1:
                key = kint_ref[:, pl.ds(off, tile)]
                idx = off + lax.broadcasted_iota(I32, (rows, tile), 1)
                return c + _fold_lanes(pred(key, idx).astype(F32))
            key = kint_ref[pl.ds(off, tile), :]
            idx = off + lax.broadcasted_iota(I32, (tile, rows), 0)
            return c + _fold_sublanes(pred(key, idx).astype(F32))
        init = jnp.zeros((rows, LANES) if keys_axis == 1 else (SUBLANES, rows), F32)
        part = lax.fori_loop(0, n_tiles, body, init)
        return jnp.sum(part, axis=keys_axis, keepdims=True)

    kf = float(topk)
    if halves is None:
        c0 = count(lambda key, col: key >= 0)
        t = jnp.where(c0 >= kf, jnp.int32(0), jnp.int32(INT_MIN))

        def bit_body(b, t):
            cand = t + jnp.left_shift(jnp.int32(1), 30 - b)
            cnt = count(lambda key, col: key >= cand)
            return jnp.where(cnt >= kf, cand, t)

        t = lax.fori_loop(0, 31, bit_body, t)
        t = jnp.maximum(t, jnp.int32(INT_MIN + 1))
        cge = count(lambda key, col: key >= t)
    else:
        t, cge = _kth_largest_by_halves(*halves, n_tiles, tile, rows, kf)
        t = jnp.maximum(t, jnp.int32(INT_MIN + 1))
    thr_ref[...] = t
    cut_ref[...] = jnp.full(stat_shape, n_cols, I32)

    @pl.when(jnp.max(cge) > kf)
    def _():
        need = kf - count(lambda key, col: key > t)
        nbits = max(1, (n_cols - 1).bit_length())

        if halves is None:
            count_tied_below = lambda cand: count(lambda key, col: (key == t) & (col < cand))
        else:
            count_tied_below = _tied_below_counter(kint_ref, halves[1], halves[2], t, n_tiles, tile, rows)

        def idx_body(b, p):
            cand = p + jnp.left_shift(jnp.int32(1), nbits - 1 - b)
            return jnp.where(count_tied_below(cand) < need, cand, p)

        p = lax.fori_loop(0, nbits, idx_body, jnp.zeros(stat_shape, I32))
        cut_ref[...] = jnp.where(cge > kf, p, jnp.int32(n_cols))


def _selected(key, col, t, cut):
    return (key > t) | ((key == t) & (col <= cut))


KEY_TILE = 128


def _dsa_prompt_kernel(qb_ref, iqb_ref, ikw_ref, kb_ref, vt_ref, ikd_ref, o_ref,
                       kint_ref, khi_ref, klo_ref, low_ref,
                       qm_ref, iqm_ref, w_ref, s0_ref, s1_ref, p0_ref, p1_ref, a0_ref, a1_ref,
                       acc_ref, m_ref, l_ref, thr_ref, cut_ref, *, topk, qb, seq):
    i = pl.program_id(1)
    kt = KEY_TILE
    assert qb == 2 * kt
    n_tiles = 2 * (i + 1)
    last = n_tiles - 1
    s_slots, p_slots, a_slots = (s0_ref, s1_ref), (p0_ref, p1_ref), (a0_ref, a1_ref)
    lane = lax.broadcasted_iota(I32, (qb, LANES), 1)
    lo = lane < HEAD_DIM
    for h in range(N_HEADS):
        sl = slice(LANES * (h // 2), LANES * (h // 2 + 1))
        msk = lo if h % 2 == 0 else jnp.logical_not(lo)
        qm_ref[h] = jnp.where(msk, qb_ref[0, :, sl], jnp.zeros((), qb_ref.dtype))
        iqm_ref[h] = jnp.where(msk, iqb_ref[0, :, sl], jnp.zeros((), iqb_ref.dtype))
    w_ref[...] = jnp.transpose(ikw_ref[0])[IDX_DIM:IDX_DIM + IDX_HEADS, :]
    qpos = i * qb + lax.broadcasted_iota(I32, (1, qb), 1)

    def tile_off(tile):
        return pl.multiple_of(jnp.clip(tile, 0, last) * kt, kt)

    def idx_dots(tile, s_out):
        ik = ikd_ref[0, pl.ds(tile_off(tile), kt), :]
        for h in range(IDX_HEADS):
            s_out[h] = lax.dot_general(ik, iqm_ref[h], _NT, preferred_element_type=F32)

    def combine(tile, s_in):
        sc = jnp.zeros((kt, qb), F32)
        for h in range(IDX_HEADS):
            sc = sc + jnp.maximum(s_in[h], 0.0) * w_ref[h:h + 1, :]
        off = tile_off(tile)
        kpos = off + lax.broadcasted_iota(I32, (kt, qb), 0)
        key = jnp.where(kpos <= qpos, _sortable_key(sc), jnp.int32(INT_MIN))
        kint_ref[pl.ds(off, kt), :] = key
        khi_ref[pl.ds(off, kt), :] = jnp.right_shift(key, 16).astype(I16)
        klo_ref[pl.ds(off, kt), :] = ((key & 0xFFFF) - 32768).astype(I16)

    idx_dots(0, s0_ref)

    def score_pair(jj, carry):
        for step in range(2):
            tile = 2 * jj + step
            combine(tile, s_slots[step])
            idx_dots(tile + 1, s_slots[1 - step])
        return carry

    lax.fori_loop(0, i + 1, score_pair, 0)
    pad = pl.ds(pl.multiple_of((i + 1) * qb, qb), qb)
    kint_ref[pad, :] = jnp.full((qb, qb), INT_MIN, I32)
    khi_ref[pad, :] = jnp.full((qb, qb), -32768, I16)
    klo_ref[pad, :] = jnp.full((qb, qb), -32768, I16)
    _topk_threshold(kint_ref, thr_ref, cut_ref, (i + 2) // 2, 2 * qb, qb, topk, seq, keys_axis=0,
                    halves=(khi_ref, klo_ref, low_ref))
    thr = thr_ref[...]
    cut = cut_ref[...]

    acc_ref[...] = jnp.zeros_like(acc_ref)
    m_ref[...] = jnp.full(m_ref.shape, NEG_BIG, F32)
    l_ref[...] = jnp.zeros_like(l_ref)
    p1_ref[...] = jnp.zeros_like(p1_ref)
    a1_ref[...] = jnp.ones_like(a1_ref)

    def qk_dots(tile, s_out):
        off = tile_off(tile)
        for h in range(N_HEADS):
            kk = kb_ref[0, pl.ds(off, kt), LANES * (h // 2):LANES * (h // 2 + 1)]
            s_out[h] = lax.dot_general(kk, qm_ref[h], _NT, preferred_element_type=F32)

    def softmax(tile, s_in, p_out, a_out):
        off = tile_off(tile)
        live = tile <= last
        kpos = off + lax.broadcasted_iota(I32, (kt, qb), 0)
        sel = _selected(kint_ref[pl.ds(off, kt), :], kpos,
                        jnp.where(live, thr, jnp.int32(2 ** 31 - 1)), jnp.where(live, cut, jnp.int32(-1)))
        bias = jnp.where(sel, 0.0, NEG_BIG)
        for h in range(N_HEADS):
            s = s_in[h] + bias
            m_old = m_ref[h]
            m_new = jnp.maximum(m_old, jnp.max(s, axis=0, keepdims=True))
            alpha = jnp.exp2(m_old - m_new)
            pm = jnp.exp2(s - m_new)
            m_ref[h] = m_new
            p_out[h] = pm.astype(p_out.dtype)
            a_out[h] = alpha

    ones_rows = jnp.ones((PACK16, kt), vt_ref.dtype)

    def pv_dots(tile, p_in, a_in):
        off = tile_off(tile)
        for h in range(N_HEADS):
            dr = slice(HEAD_DIM * h, HEAD_DIM * (h + 1))
            lhs = jnp.concatenate([vt_ref[dr, pl.ds(off, kt)], ones_rows], axis=0)
            pv = jnp.dot(lhs, p_in[h], preferred_element_type=F32)
            acc_ref[dr, :] = a_in[h] * acc_ref[dr, :] + pv[:HEAD_DIM]
            l_ref[h] = a_in[h] * l_ref[h] + pv[HEAD_DIM:HEAD_DIM + 1]

    qk_dots(0, s0_ref)

    def att_pair(jj, carry):
        for step in range(2):
            tile = 2 * jj + step
            softmax(tile, s_slots[step], p_slots[step], a_slots[step])
            qk_dots(tile + 1, s_slots[1 - step])
            pv_dots(tile - 1, p_slots[1 - step], a_slots[1 - step])
        return carry

    lax.fori_loop(0, i + 2, att_pair, 0)
    for h in range(N_HEADS):
        dr = slice(HEAD_DIM * h, HEAD_DIM * (h + 1))
        acc_ref[dr, :] = acc_ref[dr, :] / l_ref[h]
    o_ref[0] = jnp.transpose(acc_ref[...]).astype(o_ref.dtype)


def _dsa_prompt(qb_, iqb, ikw, kb, vt, ikd, *, qb, topk):
    B, S, _ = qb_.shape
    mxu = qb_.dtype
    blk = lambda width: pl.BlockSpec((1, qb, width), lambda b, i: (b, i, 0))
    full = lambda width: pl.BlockSpec((1, S, width), lambda b, i: (b, 0, 0))
    kern = functools.partial(_dsa_prompt_kernel, topk=topk, qb=qb, seq=S)
    return pl.pallas_call(
        kern, grid=(B, S // qb),
        in_specs=[blk(A_WIDTH), blk(A_WIDTH), blk(LANES), full(A_WIDTH),
                  pl.BlockSpec((A_WIDTH, S), lambda b, i: (0, b)), full(LANES)],
        out_specs=blk(A_WIDTH),
        out_shape=jax.ShapeDtypeStruct((B, S, A_WIDTH), mxu),
        scratch_shapes=[
            pltpu.VMEM((S + qb, qb), I32),
            pltpu.VMEM((S + qb, qb), I16), pltpu.VMEM((S + qb, qb), I16), pltpu.VMEM((S + qb, qb), I16),
            pltpu.VMEM((N_HEADS, qb, LANES), mxu),
            pltpu.VMEM((N_HEADS, qb, LANES), mxu),
            pltpu.VMEM((IDX_HEADS, qb), F32),
            pltpu.VMEM((N_HEADS, KEY_TILE, qb), F32),
            pltpu.VMEM((N_HEADS, KEY_TILE, qb), F32),
            pltpu.VMEM((N_HEADS, KEY_TILE, qb), mxu),
            pltpu.VMEM((N_HEADS, KEY_TILE, qb), mxu),
            pltpu.VMEM((N_HEADS, 1, qb), F32),
            pltpu.VMEM((N_HEADS, 1, qb), F32),
            pltpu.VMEM((A_WIDTH, qb), F32),
            pltpu.VMEM((N_HEADS, 1, qb), F32),
            pltpu.VMEM((N_HEADS, 1, qb), F32),
            pltpu.VMEM((1, qb), I32),
            pltpu.VMEM((1, qb), I32),
        ],
        compiler_params=_cparams(2), name="dsa_prompt",
    )(qb_, iqb, ikw, kb, vt, ikd)


PAGES_PER_STEP = 16


def _head_sum(s):
    return jnp.concatenate(
        [jnp.sum(s[IDX_HEADS * t:IDX_HEADS * (t + 1)], axis=0, keepdims=True) for t in range(s.shape[0] // IDX_HEADS)],
        axis=0)


def _sample_scores_kernel(pt_ref, iq_ref, w_ref, ikn_ref, *rest, n_steps):
    pages = rest[:PAGES_PER_STEP]
    sc_ref, scn_ref = rest[PAGES_PER_STEP:]
    c = pl.program_id(1)
    iq = iq_ref[0]
    w = w_ref[0]

    def scores(ik):
        s = jnp.dot(iq, ik.astype(iq.dtype), preferred_element_type=F32)
        return _head_sum(jnp.maximum(s, 0.0) * w)

    sc_ref[0] = scores(jnp.concatenate([r[...].astype(iq.dtype) for r in pages], axis=1))

    @pl.when(c == n_steps - 1)
    def _():
        s = scores(ikn_ref[0])
        col = lax.broadcasted_iota(I32, s.shape, 1)
        row = lax.broadcasted_iota(I32, s.shape, 0)
        scn_ref[0] = jnp.where(col <= row, s, -jnp.inf)


def _page_specs(layer, rows, page):
    def make(r):
        def imap(b, c, pt):
            return (layer, pt[b, c * PAGES_PER_STEP + r], 0, 0)
        return pl.BlockSpec((None, None, rows, page), imap)
    return [make(r) for r in range(PAGES_PER_STEP)]


def _sample_scores(page_table, iq_rows, w_rows, ik_new, cache_ik, *, layer):
    DB, n_pages = page_table.shape
    page = cache_ik.shape[3]
    n_steps = n_pages // PAGES_PER_STEP
    rows = iq_rows.shape[1]
    nt = rows // IDX_HEADS
    step_cols = PAGES_PER_STEP * page
    grid_spec = pltpu.PrefetchScalarGridSpec(
        num_scalar_prefetch=1, grid=(DB, n_steps),
        in_specs=[
            pl.BlockSpec((1, rows, IDX_DIM), lambda b, c, pt: (b, 0, 0)),
            pl.BlockSpec((1, rows, 1), lambda b, c, pt: (b, 0, 0)),
            pl.BlockSpec((1, IDX_DIM, page), lambda b, c, pt: (b, 0, 0)),
        ] + _page_specs(layer, IDX_DIM, page),
        out_specs=[pl.BlockSpec((1, nt, step_cols), lambda b, c, pt: (b, 0, c)),
                   pl.BlockSpec((1, nt, page), lambda b, c, pt: (b, 0, 0))],
    )
    kern = functools.partial(_sample_scores_kernel, n_steps=n_steps)
    return pl.pallas_call(
        kern, grid_spec=grid_spec,
        out_shape=[jax.ShapeDtypeStruct((DB, nt, n_pages * page), F32), jax.ShapeDtypeStruct((DB, nt, page), F32)],
        compiler_params=_cparams(2), name="sample_scores",
    )(page_table, iq_rows, w_rows, ik_new, *([cache_ik] * PAGES_PER_STEP))


def _score_key(sc):
    return jnp.where(sc == -jnp.inf, jnp.int32(INT_MIN), _sortable_key(sc))


def _sample_threshold_kernel(sc_ref, scn_ref, thr_ref, cut_ref, kint_ref, t_ref, c_ref, *, topk, tile):
    rows, past = sc_ref.shape
    page = scn_ref.shape[1]
    cols = kint_ref.shape[1]
    kint_ref[:, 0:past] = _score_key(sc_ref[...])
    kint_ref[:, past:past + page] = _score_key(scn_ref[...])
    kint_ref[:, past + page:] = jnp.full((rows, cols - past - page), INT_MIN, I32)
    _topk_threshold(kint_ref, t_ref, c_ref, cols // tile, tile, rows, topk, cols)
    thr_ref[...] = jnp.broadcast_to(t_ref[...], thr_ref.shape)
    cut_ref[...] = jnp.broadcast_to(c_ref[...], cut_ref.shape)


def _sample_threshold(sc, sc_new, *, topk):
    rows, past = sc.shape
    page = sc_new.shape[1]
    tile = 2 * LANES
    cols = pl.cdiv(past + page, tile) * tile
    kern = functools.partial(_sample_threshold_kernel, topk=topk, tile=tile)
    return pl.pallas_call(
        kern, grid=(1,),
        in_specs=[_const_spec(sc.shape), _const_spec(sc_new.shape)],
        out_specs=[_const_spec((rows, LANES)), _const_spec((rows, LANES))],
        out_shape=[jax.ShapeDtypeStruct((rows, LANES), I32)] * 2,
        scratch_shapes=[pltpu.VMEM((rows, cols), I32), pltpu.VMEM((rows, 1), I32), pltpu.VMEM((rows, 1), I32)],
        compiler_params=_cparams(1), name="sample_threshold",
    )(sc, sc_new)


def _sample_attn_kernel(pt_ref, q_ref, sc_ref, scn_ref, thr_ref, cut_ref, kn_ref, vn_ref, hm_ref, *rest, n_steps):
    kp = rest[:PAGES_PER_STEP]
    vp = rest[PAGES_PER_STEP:2 * PAGES_PER_STEP]
    o_ref = rest[2 * PAGES_PER_STEP]
    m_ref, l_ref, acc_ref = rest[2 * PAGES_PER_STEP + 1:]
    c = pl.program_id(1)
    q = q_ref[0]
    mxu = q.dtype
    nt = sc_ref.shape[1]
    step_cols = sc_ref.shape[2]

    @pl.when(c == 0)
    def _():
        m_ref[...] = jnp.full(m_ref.shape, NEG_BIG, F32)
        l_ref[...] = jnp.zeros_like(l_ref)
        acc_ref[...] = jnp.zeros_like(acc_ref)

    def rows_th(x):
        return jnp.concatenate([jnp.broadcast_to(x[t:t + 1], (N_HEADS, x.shape[1])) for t in range(nt)], axis=0)

    def update(keys, vals, sc, col0):
        key = _score_key(sc)
        col = col0 + lax.broadcasted_iota(I32, key.shape, 1)
        sel = rows_th(_selected(key, col, thr_ref[0, :, 0:1], cut_ref[0, :, 0:1]).astype(F32)) > 0.5
        kcat = jnp.concatenate([kk.astype(mxu) for kk in keys], axis=1)
        s = jnp.dot(q, kcat, preferred_element_type=F32)
        s = jnp.where(sel, s, NEG_BIG)
        m_old = m_ref[...]
        m_new = jnp.maximum(m_old, jnp.max(s, axis=1, keepdims=True))
        alpha = jnp.exp2(m_old - m_new)
        pm = jnp.exp2(s - m_new)
        l_ref[...] = alpha * l_ref[...] + jnp.sum(pm, axis=1, keepdims=True)
        m_ref[...] = m_new
        vcat = jnp.concatenate([vv.astype(mxu) for vv in vals], axis=1)
        pv = lax.dot_general(pm.astype(mxu), vcat, _NT, preferred_element_type=F32)
        acc_ref[...] = alpha * acc_ref[...] + pv

    update([r[...] for r in kp], [r[...] for r in vp], sc_ref[0], c * step_cols)

    @pl.when(c == n_steps - 1)
    def _():
        update([kn_ref[0]], [vn_ref[0]], scn_ref[0], n_steps * step_cols)
        out = acc_ref[...] / l_ref[...] * hm_ref[...]
        o_ref[0] = _head_sum(out).astype(o_ref.dtype)


def _sample_attn(page_table, q_rows, sc, sc_new, thr, cut, k_new, v_new, head_mask, cache_k, cache_v, *, layer):
    DB, n_pages = page_table.shape
    page = cache_k.shape[3]
    n_steps = n_pages // PAGES_PER_STEP
    rows = q_rows.shape[1]
    nt = rows // N_HEADS
    step_cols = PAGES_PER_STEP * page
    per_b = lambda shape: pl.BlockSpec((1,) + shape, lambda b, c, pt: (b, 0, 0))
    grid_spec = pltpu.PrefetchScalarGridSpec(
        num_scalar_prefetch=1, grid=(DB, n_steps),
        in_specs=[
            per_b((rows, A_WIDTH)),
            pl.BlockSpec((1, nt, step_cols), lambda b, c, pt: (b, 0, c)),
            per_b((nt, page)), per_b((nt, LANES)), per_b((nt, LANES)),
            per_b((A_WIDTH, page)), per_b((A_WIDTH, page)),
            pl.BlockSpec((rows, A_WIDTH), lambda b, c, pt: (0, 0)),
        ] + _page_specs(layer, A_WIDTH, page) * 2,
        out_specs=per_b((nt, A_WIDTH)),
        scratch_shapes=[pltpu.VMEM((rows, 1), F32), pltpu.VMEM((rows, 1), F32), pltpu.VMEM((rows, A_WIDTH), F32)],
    )
    kern = functools.partial(_sample_attn_kernel, n_steps=n_steps)
    return pl.pallas_call(
        kern, grid_spec=grid_spec,
        out_shape=jax.ShapeDtypeStruct((DB, nt, A_WIDTH), q_rows.dtype),
        compiler_params=_cparams(2), name="sample_attn",
    )(page_table, q_rows, sc, sc_new, thr, cut, k_new, v_new, head_mask,
      *([cache_k] * PAGES_PER_STEP), *([cache_v] * PAGES_PER_STEP))


def _shift_rows(e, k):
    return pltpu.roll(e, k, 0)


def _merge_kernel(x_ref, att_ref, u_ref, vv_ref, xc_ref, halo_ref, cnt_ref,
                  g1_ref, wgt_ref, wa_ref, wb_ref, wc_ref, wout_ref, ws_ref, btab_ref, wpool_ref, pscale_ref,
                  o_ref, *, row_stride, tiles_per_seq, zero_first_halo):
    mxu = wgt_ref.dtype
    tm = x_ref.shape[0]
    x = x_ref[...]
    xn = _rms(x, g1_ref[...]).astype(mxu)
    d = x.shape[1]

    def gate(n):
        return jax.nn.sigmoid(jnp.dot(xn, wgt_ref[:, n * d:(n + 1) * d], preferred_element_type=F32))

    merged = gate(0) * jnp.dot(att_ref[...], wa_ref[...], preferred_element_type=F32)

    lane = lax.broadcasted_iota(I32, (SGU_CHUNK, LANES), 1)
    lo = lane < (SGU_WIDTH // SGU_GROUPS)
    sgo = []
    for c in range(tm // SGU_CHUNK):
        rs = slice(c * SGU_CHUNK, (c + 1) * SGU_CHUNK)
        vvb = vv_ref[rs, :].astype(mxu)
        mix = []
        for p in range(SGU_GROUPS // 2):
            pair = vvb[:, LANES * p:LANES * (p + 1)]
            r0 = jnp.dot(ws_ref[2 * p], pair, preferred_element_type=F32)
            r1 = jnp.dot(ws_ref[2 * p + 1], pair, preferred_element_type=F32)
            mix.append(jnp.where(lo, r0, r1))
        sgo.append(u_ref[rs, :] * (jnp.concatenate(mix, axis=1) + btab_ref[...]))
    sgo = jnp.concatenate(sgo, axis=0).astype(mxu)
    merged = merged + gate(1) * jnp.dot(sgo, wb_ref[...], preferred_element_type=F32)

    xc = xc_ref[...]
    halo = halo_ref[...]
    if zero_first_halo:
        first = (pl.program_id(0) % tiles_per_seq) == 0
        halo = jnp.where(first, jnp.zeros_like(halo), halo)
    hp = halo.shape[0]
    e = jnp.concatenate([halo, xc], axis=0)
    s1 = e + _shift_rows(e, row_stride)
    s2 = s1 + _shift_rows(s1, 2 * row_stride)
    s3 = s2 + _shift_rows(s2, 4 * row_stride)
    s4 = s3 + _shift_rows(s3, 8 * row_stride)
    gd = POOL_WIDTH // POOL_GROUPS
    wsum = jnp.concatenate([s[hp:, g * gd:(g + 1) * gd] for g, s in enumerate((s1, s2, s3, s4))], axis=1)
    pooled = (wsum / cnt_ref[...] - xc).astype(mxu)
    po = (jnp.dot(pooled, wpool_ref[...], preferred_element_type=F32) * pscale_ref[...]).astype(mxu)
    merged = merged + gate(2) * jnp.dot(po, wc_ref[...], preferred_element_type=F32)

    o_ref[...] = x + jnp.dot(merged.astype(mxu), wout_ref[...], preferred_element_type=F32)


def _merge(x, att, u, vv, xc, halo, cnt, g1, wgt, wa, wb, wc, wout, ws, btab, wpool, pscale,
           *, tm, row_stride, tiles_per_seq, halo_rows, halo_from_xc, cnt_tiles):
    T, D = x.shape
    n = T // tm
    row = lambda width: pl.BlockSpec((tm, width), lambda i: (i, 0))
    if halo_from_xc:
        per = tm // halo_rows
        halo_spec = pl.BlockSpec((halo_rows, xc.shape[1]), lambda i: (jnp.maximum(i * per - 1, 0), 0))
    else:
        halo_spec = _const_spec(halo.shape)
    consts = [g1, wgt, wa, wb, wc, wout, ws, btab, wpool, pscale]
    kern = functools.partial(_merge_kernel, row_stride=row_stride, tiles_per_seq=tiles_per_seq,
                             zero_first_halo=halo_from_xc)
    return pl.pallas_call(
        kern, grid=(n,),
        in_specs=[row(D), row(att.shape[1]), row(u.shape[1]), row(vv.shape[1]), row(xc.shape[1]), halo_spec,
                  pl.BlockSpec((tm, cnt.shape[1]), lambda i: (i % cnt_tiles, 0))]
        + [_const_spec(c.shape) for c in consts],
        out_specs=row(D), out_shape=jax.ShapeDtypeStruct((T, D), F32),
        compiler_params=_cparams(1), name="merge",
    )(x, att, u, vv, xc, halo, cnt, *consts)


FFN_CHUNK = 256


def _ffn_kernel(x_ref, g2_ref, wa_ref, wu_ref, cw_ref, cb_ref, wd_ref, halo_ref, o_ref, tail_ref, carry_ref,
                *, row_stride, tiles_per_seq, use_carry):
    mxu = wa_ref.dtype
    x = x_ref[...]
    tm = x.shape[0]
    xn = _rms(x, g2_ref[...]).astype(mxu)
    dff = wa_ref.shape[1]
    tail = tail_ref.shape[1]
    acc = jnp.zeros(x.shape, F32)
    if use_carry:
        first = (pl.program_id(0) % tiles_per_seq) == 0
    for c0 in range(0, dff, FFN_CHUNK):
        cs = slice(c0, c0 + FFN_CHUNK)
        a = jnp.dot(xn, wa_ref[:, cs], preferred_element_type=F32)
        up = jnp.dot(xn, wu_ref[:, cs], preferred_element_type=F32)
        if use_carry:
            halo = jnp.where(first, 0.0, carry_ref[:, cs])
            carry_ref[:, cs] = a[tm - carry_ref.shape[0]:, :]
        else:
            halo = halo_ref[:, cs]
        hp = halo.shape[0]
        e = jnp.concatenate([halo, a], axis=0)
        p1 = _shift_rows(e, row_stride)[hp:]
        p2 = _shift_rows(e, 2 * row_stride)[hp:]
        conv = p2 * cw_ref[0:1, cs] + p1 * cw_ref[1:2, cs] + a * cw_ref[2:3, cs] + cb_ref[:, cs]
        act = (jax.nn.silu(conv) * up).astype(mxu)
        acc = acc + jnp.dot(act, wd_ref[cs, :], preferred_element_type=F32)
        tail_ref[0, :, cs] = a[tm - tail:, :]
    o_ref[...] = x + acc


def _ffn(x, g2, wa, wu, cw, cb, wd, halo, *, tm, row_stride, tiles_per_seq, use_carry, tail):
    T, D = x.shape
    n = T // tm
    dff = wa.shape[1]
    row = pl.BlockSpec((tm, D), lambda i: (i, 0))
    consts = [g2, wa, wu, cw, cb, wd, halo]
    kern = functools.partial(_ffn_kernel, row_stride=row_stride, tiles_per_seq=tiles_per_seq, use_carry=use_carry)
    return pl.pallas_call(
        kern, grid=(n,),
        in_specs=[row] + [_const_spec(c.shape) for c in consts],
        out_specs=[row, pl.BlockSpec((1, tail, dff), lambda i: (i, 0, 0))],
        out_shape=[jax.ShapeDtypeStruct((T, D), F32), jax.ShapeDtypeStruct((n, tail, dff), F32)],
        scratch_shapes=[pltpu.VMEM((SUBLANES, dff), F32)],
        compiler_params=_cparams(1), name="ffn",
    )(x, *consts)


def _pack_w_in(w_in):
    d = w_in.shape[0]
    o = 0
    q, k, v = (w_in[:, o + i * A_WIDTH:o + (i + 1) * A_WIDTH] for i in range(3))
    o += 3 * A_WIDTH
    iq = w_in[:, o:o + IDX_HEADS * IDX_DIM]
    o += IDX_HEADS * IDX_DIM
    ik = w_in[:, o:o + IDX_DIM]
    o += IDX_DIM
    iw = w_in[:, o:o + IDX_HEADS]
    o += IDX_HEADS
    sg = w_in[:, o:o + 2 * SGU_WIDTH]
    o += 2 * SGU_WIDTH
    xc = w_in[:, o:o + POOL_WIDTH]
    o += POOL_WIDTH
    gt = w_in[:, o:]
    pad = jnp.zeros((d, LANES - IDX_DIM - IDX_HEADS), w_in.dtype)
    packed = jnp.concatenate([q, k, v, iq, ik, iw, pad, ik, ik, sg, xc], axis=1)
    assert packed.shape[1] == PROJ_COLS
    return packed, gt


def _rope_tables(pos):
    posf = np.asarray(pos, np.float64)[:, None]

    def cs(half):
        inv = ROPE_THETA ** (-np.arange(half, dtype=np.float64) / half)
        ang = posf * inv[None, :]
        return np.cos(ang), np.sin(ang)

    t = posf.shape[0]
    c32, s32 = cs(HEAD_DIM // 2)
    cqk = np.tile(np.concatenate([c32, c32], axis=1), (1, 2))
    sqk = np.tile(np.concatenate([-s32, s32], axis=1), (1, 2))
    c16, s16 = cs(IDX_ROPE // 2)
    rest = IDX_DIM - IDX_ROPE
    ci = np.concatenate([c16, c16, np.ones((t, rest))], axis=1)
    si = np.concatenate([-s16, s16, np.zeros((t, rest))], axis=1)
    ciq, siq = np.tile(ci, (1, 2)), np.tile(si, (1, 2))
    wpad = LANES - IDX_DIM - IDX_HEADS
    cikw = np.concatenate([ci, np.full((t, IDX_HEADS), IDX_HEADS ** -0.5), np.zeros((t, wpad))], axis=1)
    sikw = np.concatenate([si, np.zeros((t, LANES - IDX_DIM))], axis=1)
    return jnp.asarray(np.concatenate([cqk, sqk, ciq, siq, cikw, sikw], axis=1).astype(np.float32))


def _block_diag(blocks):
    n = len(blocks)
    r, c = blocks[0].shape
    out = jnp.zeros((n * r, n * c), blocks[0].dtype)
    for i, b in enumerate(blocks):
        out = out.at[i * r:(i + 1) * r, i * c:(i + 1) * c].set(b)
    return out


def _pool_counts(pos):
    gd = POOL_WIDTH // POOL_GROUPS
    pos = np.asarray(pos)
    cols = [np.broadcast_to(np.minimum(pos + 1, w).astype(np.float32)[:, None], (pos.shape[0], gd))
            for w in POOL_WINDOWS]
    return jnp.asarray(np.concatenate(cols, axis=1))


def _head_segments(dtype):
    return jnp.asarray(np.kron(np.eye(N_HEADS, dtype=np.float32), np.ones((HEAD_DIM, HEAD_DIM), np.float32)), dtype)


def _layer_weights(l, norm1_g, w_in, q_norm_g, k_norm_g, sgu_w, sgu_b, sgu_norm_g, pool_w, pool_scale,
                   w_br_a, w_br_b, w_br_c, w_out, norm2_g, w_ff_in, ff_conv_w, ff_conv_b, w_ff_down):
    mxu = MXU_DTYPE
    packed, gt = _pack_w_in(w_in[l])
    dff = w_ff_in.shape[2] // 2
    tril = jnp.tril(jnp.ones((SGU_CHUNK, SGU_CHUNK), bool))
    return dict(
        g1=norm1_g[l][None, :], w_proj=packed.astype(mxu), w_gt=gt.astype(mxu),
        qg=jnp.tile(q_norm_g[l], N_HEADS)[None, :], kg=jnp.tile(k_norm_g[l], N_HEADS)[None, :],
        sgg=sgu_norm_g[l][None, :],
        ws=jnp.where(tril[None], sgu_w[l], 0.0),
        sgu_b=sgu_b[l],
        wpool=_block_diag([pool_w[l, g] for g in range(POOL_GROUPS)]).astype(mxu),
        pscale=pool_scale[l][None, :],
        wa=w_br_a[l].astype(mxu), wb=w_br_b[l].astype(mxu), wc=w_br_c[l].astype(mxu), wout=w_out[l].astype(mxu),
        g2=norm2_g[l][None, :],
        w_ffa=w_ff_in[l, :, :dff].astype(mxu), w_ffu=w_ff_in[l, :, dff:].astype(mxu),
        cw=jnp.pad(ff_conv_w[l], ((0, SUBLANES - CONV_W), (0, 0))), cb=ff_conv_b[l][None, :],
        w_ffd=w_ff_down[l].astype(mxu),
    )


def _sgu_bias_table(sgu_b, t_of_row):
    gd = SGU_WIDTH // SGU_GROUPS
    return jnp.repeat(jnp.transpose(sgu_b)[t_of_row], gd, axis=1)


def _prompt_layer(x, lw, tabs, cnt, l, depth, state, *, B, S):
    mxu = MXU_DTYPE
    T = B * S
    tm_proj, tm_merge, tm_ffn, qb = 512, 256, 512, 256
    pr = _proj(x, lw["g1"], lw["w_proj"], lw["qg"], lw["kg"], lw["sgg"], tabs, _head_segments(mxu),
               tm=tm_proj, tab_tiles=S // tm_proj, stack=(l, depth, B, S, state))
    u, vv, xc = pr["u"], pr["vv"], pr["xc"]
    r3 = lambda a: a.reshape(B, S, a.shape[1])
    topk = min(TOPK_MAX, S // 4)
    att = _dsa_prompt(r3(pr["qb"]), r3(pr["iqb"]), r3(pr["ikw"]), r3(pr["kb"]), pr["vt"], r3(pr["ikd"]),
                      qb=qb, topk=topk).reshape(T, A_WIDTH)
    btab = _sgu_bias_table(lw["sgu_b"], jnp.arange(SGU_CHUNK))
    x1 = _merge(x, att, u, vv, xc, xc, cnt, lw["g1"], lw["w_gt"], lw["wa"], lw["wb"], lw["wc"], lw["wout"],
                lw["ws"].astype(mxu), btab, lw["wpool"], lw["pscale"],
                tm=tm_merge, row_stride=1, tiles_per_seq=S // tm_merge, halo_rows=16, halo_from_xc=True,
                cnt_tiles=S // tm_merge)
    dff = lw["w_ffa"].shape[1]
    x2, tails = _ffn(x1, lw["g2"], lw["w_ffa"], lw["w_ffu"], lw["cw"], lw["cb"], lw["w_ffd"],
                     jnp.zeros((SUBLANES, dff), F32),
                     tm=tm_ffn, row_stride=1, tiles_per_seq=S // tm_ffn, use_carry=True, tail=SUBLANES)
    n_t = S // tm_ffn
    ff_state = tails.reshape(B, n_t, SUBLANES, dff)[:, -1, SUBLANES - (CONV_W - 1):, :]
    pool_state = xc.reshape(B, S, POOL_WIDTH)[:, S - POOL_STATE:, :]
    return x2, (pr["k"], pr["v"], pr["ik"]), (pool_state, ff_state)


def _sample_layer(x, lw, tabs, cnt, l, cache_k, cache_v, cache_ik, state_pool, state_ffn, page_table, *, DB, TS):
    mxu = MXU_DTYPE
    T = TS * DB
    page = cache_ik.shape[3]
    past = page_table.shape[1] * page
    pr = _proj(x, lw["g1"], lw["w_proj"], lw["qg"], lw["kg"], lw["sgg"], tabs, _head_segments(mxu),
               tm=T, tab_tiles=1)
    qb_, k, kb, v, vt, iqb, ikw, ikd, u, vv, xc = (
        pr[n] for n in ("qb", "k", "kb", "v", "vt", "iqb", "ikw", "ikd", "u", "vv", "xc"))
    bm = lambda a: jnp.transpose(a.reshape(TS, DB, a.shape[1]), (1, 0, 2))

    iq_rows = bm(iqb).reshape(DB, TS * IDX_HEADS, IDX_DIM)
    w_rows = bm(ikw)[:, :, IDX_DIM:IDX_DIM + IDX_HEADS].reshape(DB, TS * IDX_HEADS, 1)
    new_t = lambda a: jnp.pad(jnp.transpose(a, (0, 2, 1)), ((0, 0), (0, 0), (0, page - TS)))
    ik_new = new_t(bm(ikd)[:, :, :IDX_DIM])
    sc, sc_new = _sample_scores(page_table, iq_rows, w_rows, ik_new, cache_ik, layer=l)
    topk = min(TOPK_MAX, (past + TS) // 4)
    thr, cut = _sample_threshold(sc.reshape(DB * TS, past), sc_new.reshape(DB * TS, page), topk=topk)
    hm = jnp.asarray(np.repeat(np.eye(N_HEADS, dtype=np.float32), HEAD_DIM, axis=1))
    q_rows = (bm(qb_)[:, :, None, :] * hm[None, None].astype(mxu)).reshape(DB, TS * N_HEADS, A_WIDTH)
    att = _sample_attn(page_table, q_rows, sc, sc_new, thr.reshape(DB, TS, LANES), cut.reshape(DB, TS, LANES),
                       new_t(bm(kb)), new_t(bm(jnp.transpose(vt))), jnp.tile(hm, (TS, 1)), cache_k, cache_v, layer=l)
    att = jnp.transpose(att, (1, 0, 2)).reshape(T, A_WIDTH)

    eye = jnp.eye(DB, dtype=F32)
    ws = jnp.stack([jnp.kron(lw["ws"][g, :TS, :TS], eye) for g in range(SGU_GROUPS)])
    pc = SGU_CHUNK - T
    ws = jnp.pad(ws, ((0, 0), (0, pc), (0, pc)))
    btab = _sgu_bias_table(lw["sgu_b"], jnp.minimum(jnp.arange(SGU_CHUNK) // DB, SGU_CHUNK - 1))
    halo = jnp.concatenate([jnp.zeros((DB, POOL_WIDTH), F32),
                            jnp.transpose(state_pool[l], (1, 0, 2)).reshape(POOL_STATE * DB, POOL_WIDTH)], axis=0)
    x1 = _merge(x, att, u, vv, xc, halo, cnt, lw["g1"], lw["w_gt"], lw["wa"], lw["wb"], lw["wc"], lw["wout"],
                ws.astype(mxu), btab, lw["wpool"], lw["pscale"],
                tm=T, row_stride=DB, tiles_per_seq=1, halo_rows=halo.shape[0], halo_from_xc=False, cnt_tiles=1)
    dff = lw["w_ffa"].shape[1]
    ff_halo = jnp.transpose(state_ffn[l], (1, 0, 2)).reshape((CONV_W - 1) * DB, dff)
    tail = (CONV_W - 1) * DB
    x2, tails = _ffn(x1, lw["g2"], lw["w_ffa"], lw["w_ffu"], lw["cw"], lw["cb"], lw["w_ffd"], ff_halo,
                     tm=T, row_stride=DB, tiles_per_seq=1, use_carry=False, tail=tail)
    ff_state = jnp.transpose(tails.reshape(CONV_W - 1, DB, dff), (1, 0, 2))
    pool_state = jnp.concatenate([state_pool[l], bm(xc)], axis=1)[:, -POOL_STATE:, :]
    hd = lambda a: bm(a).reshape(DB, TS, N_HEADS, HEAD_DIM)
    return x2, (hd(k), hd(v), bm(ikw)[:, :, :IDX_DIM], bm(vv), pool_state, ff_state)


def kernel(x_prompt, x_sample, cache_k, cache_v, cache_idx_k, state_pool, state_ffn_conv, page_table, norm1_g, w_in, q_norm_g, k_norm_g, sgu_w, sgu_b, sgu_norm_g, pool_w, pool_scale, w_br_a, w_br_b, w_br_c, w_out, norm2_g, w_ff_in, ff_conv_w, ff_conv_b, w_ff_down):
    weights = (norm1_g, w_in, q_norm_g, k_norm_g, sgu_w, sgu_b, sgu_norm_g, pool_w, pool_scale,
               w_br_a, w_br_b, w_br_c, w_out, norm2_g, w_ff_in, ff_conv_w, ff_conv_b, w_ff_down)
    B, S, D = x_prompt.shape
    DB, TS, _ = x_sample.shape
    depth = w_in.shape[0]
    page = cache_idx_k.shape[2]
    past = page_table.shape[1] * page
    pool = cache_k.shape[1]
    cache_kt = jnp.transpose(cache_k, (0, 1, 3, 4, 2)).reshape(depth, pool, A_WIDTH, page)
    cache_vt = jnp.transpose(cache_v, (0, 1, 3, 4, 2)).reshape(depth, pool, A_WIDTH, page)
    cache_ikt = jnp.transpose(cache_idx_k, (0, 1, 3, 2))
    assert S % 512 == 0 and S >= POOL_STATE and TS * DB == SGU_CHUNK and TS >= CONV_W - 1 and TS <= page
    assert past % SGU_CHUNK == 0 and page_table.shape[1] % PAGES_PER_STEP == 0

    pos_p = np.arange(S)
    pos_s = past + np.repeat(np.arange(TS), DB)
    tabs_p, tabs_s = _rope_tables(pos_p), _rope_tables(pos_s)
    cnt_p, cnt_s = _pool_counts(pos_p), _pool_counts(pos_s)

    xp = x_prompt.reshape(B * S, D)
    xs = jnp.transpose(x_sample, (1, 0, 2)).reshape(TS * DB, D)
    outs_p, outs_s = [], []
    kv_state = None
    for l in range(depth):
        lw = _layer_weights(l, *weights)
        xp, kv_state, st_p = _prompt_layer(xp, lw, tabs_p, cnt_p, l, depth, kv_state, B=B, S=S)
        xs, st_s = _sample_layer(xs, lw, tabs_s, cnt_s, l, cache_kt, cache_vt, cache_ikt, state_pool,
                                 state_ffn_conv, page_table, DB=DB, TS=TS)
        outs_p.append(st_p)
        outs_s.append(st_s)
    stack = lambda outs, i: jnp.stack([o[i] for o in outs])
    y_p = xp.reshape(B, S, D)
    y_s = jnp.transpose(xs.reshape(TS, DB, D), (1, 0, 2))
    k_all, v_all, ik_all = kv_state
    heads = lambda a: jnp.transpose(a.reshape(depth, B, N_HEADS, HEAD_DIM, S), (0, 1, 4, 2, 3))
    return (y_p, y_s,
            heads(k_all), heads(v_all), jnp.transpose(ik_all, (0, 1, 3, 2)), stack(outs_p, 0), stack(outs_p, 1),
            stack(outs_s, 0), stack(outs_s, 1), stack(outs_s, 2), stack(outs_s, 3), stack(outs_s, 4), stack(outs_s, 5))
```

```python
import functools

import jax
import jax.numpy as jnp
import numpy as np
from jax import lax
from jax.experimental import pallas as pl
from jax.experimental.pallas import tpu as pltpu

MXU_DTYPE = jnp.bfloat16
F32 = jnp.float32
I32 = jnp.int32

N_HEADS = 8
HEAD_DIM = 64
A_WIDTH = N_HEADS * HEAD_DIM
IDX_HEADS = 8
IDX_DIM = 64
IDX_ROPE = 32
TOPK_MAX = 256
SGU_GROUPS = 4
SGU_WIDTH = 256
SGU_CHUNK = 128
POOL_GROUPS = 4
POOL_WIDTH = 256
POOL_WINDOWS = (2, 4, 8, 16)
POOL_STATE = 15
CONV_W = 3
ROPE_THETA = 10000.0
EPS = 1e-6

LANES = 128
SUBLANES = 8
INT_MIN = -2 ** 31
NEG_BIG = -1e30
Q_SCALE = HEAD_DIM ** -0.5 * float(np.log2(np.e))
PROJ_COLS = 3072
VMEM_LIMIT = 56 * 1024 * 1024

_NT = (((1,), (1,)), ((), ()))


def _cparams(n_axes):
    return pltpu.CompilerParams(dimension_semantics=("arbitrary",) * n_axes, vmem_limit_bytes=VMEM_LIMIT)


def _const_spec(shape):
    nd = len(shape)
    return pl.BlockSpec(shape, lambda *_: (0,) * nd)


def _rope128(x, cos, sin, half):
    lane = lax.broadcasted_iota(I32, x.shape, 1)
    lo = (lane & (HEAD_DIM - 1)) < half
    rot = jnp.where(lo, pltpu.roll(x, LANES - half, 1), pltpu.roll(x, half, 1))
    return x * cos + rot * sin


def _head_rms(x, seg, g):
    ss = jnp.dot((x * x).astype(seg.dtype), seg, preferred_element_type=F32)
    return x * lax.rsqrt(ss * (1.0 / HEAD_DIM) + EPS) * g


def _rms(x, g):
    return x * lax.rsqrt(jnp.mean(x * x, axis=-1, keepdims=True) + EPS) * g


def _proj_kernel(x_ref, g1_ref, w_ref, qg_ref, kg_ref, sgg_ref, tab_ref, seg_ref, *rest, stacked, n_alias):
    rest = rest[n_alias:]
    if stacked:
        qb_ref, k_ref, kb_ref, v_ref, vt_ref, iqb_ref, ikw_ref, ik_ref, ikd_ref, u_ref, vv_ref, xc_ref = rest
    else:
        qb_ref, k_ref, kb_ref, v_ref, vt_ref, iqb_ref, ikw_ref, ikd_ref, u_ref, vv_ref, xc_ref = rest
    mxu = w_ref.dtype
    xn = _rms(x_ref[...], g1_ref[...]).astype(mxu)

    def mm(c0, c1):
        return jnp.dot(xn, w_ref[:, c0:c1], preferred_element_type=F32)

    cqk, sqk = tab_ref[:, 0:128], tab_ref[:, 128:256]
    ciq, siq = tab_ref[:, 256:384], tab_ref[:, 384:512]
    cikw, sikw = tab_ref[:, 512:640], tab_ref[:, 640:768]
    seg = seg_ref[...]
    half_qk = HEAD_DIM // 2
    half_idx = IDX_ROPE // 2

    q = _head_rms(mm(0, 512), seg, qg_ref[...])
    for c in range(4):
        sl = slice(LANES * c, LANES * (c + 1))
        qb_ref[:, sl] = (_rope128(q[:, sl], cqk, sqk, half_qk) * Q_SCALE).astype(mxu)
    k = _head_rms(mm(512, 1024), seg, kg_ref[...])
    for c in range(4):
        sl = slice(LANES * c, LANES * (c + 1))
        kr = _rope128(k[:, sl], cqk, sqk, half_qk)
        if stacked:
            k_ref[sl, :] = jnp.transpose(kr)
        else:
            k_ref[:, sl] = kr
        kb_ref[:, sl] = kr.astype(mxu)
    v = mm(1024, 1536)
    vt = jnp.transpose(v)
    v_ref[...] = vt if stacked else v
    vt_ref[...] = vt.astype(mxu)
    iq = mm(1536, 2048)
    for c in range(4):
        sl = slice(LANES * c, LANES * (c + 1))
        iqb_ref[:, sl] = (_rope128(iq[:, sl], ciq, siq, half_idx) * (IDX_DIM ** -0.5)).astype(mxu)
    ikw = _rope128(mm(2048, 2176), cikw, sikw, half_idx)
    ikw_ref[...] = ikw
    if stacked:
        ik_ref[...] = jnp.transpose(ikw)[:IDX_DIM, :]
    ikd_ref[...] = _rope128(mm(2176, 2304), ciq, siq, half_idx).astype(mxu)
    sg = jax.nn.gelu(mm(2304, 2816))
    u_ref[...] = sg[:, :SGU_WIDTH]
    vv_ref[...] = _rms(sg[:, SGU_WIDTH:], sgg_ref[...])
    xc_ref[...] = mm(2816, 3072)


def _proj(x, g1, w, qg, kg, sgg, tab, seg, *, tm, tab_tiles, stack=None):
    T, D = x.shape
    mxu = w.dtype
    n = T // tm
    row = lambda width: pl.BlockSpec((tm, width), lambda i: (i, 0))
    natural = lambda width, dt: (jax.ShapeDtypeStruct((T, width), dt), row(width))
    outs = dict(
        qb=natural(A_WIDTH, mxu),
        k=natural(A_WIDTH, F32), kb=natural(A_WIDTH, mxu), v=natural(A_WIDTH, F32),
        vt=(jax.ShapeDtypeStruct((A_WIDTH, T), mxu), pl.BlockSpec((A_WIDTH, tm), lambda i: (0, i))),
        iqb=natural(IDX_HEADS * IDX_DIM, mxu),
        ikw=natural(LANES, F32),
        ik=None,
        ikd=natural(LANES, mxu),
        u=natural(SGU_WIDTH, F32), vv=natural(SGU_WIDTH, F32), xc=natural(POOL_WIDTH, F32),
    )
    in_specs = [
        row(D), _const_spec(g1.shape), _const_spec(w.shape), _const_spec(qg.shape), _const_spec(kg.shape),
        _const_spec(sgg.shape),
        pl.BlockSpec((tm, tab.shape[1]), lambda i: (i % tab_tiles, 0)),
        _const_spec(seg.shape),
    ]
    args = [x, g1, w, qg, kg, sgg, tab, seg]
    aliases = {}
    if stack is not None:
        layer, depth, B, S, state = stack
        tps = S // tm
        state_out = lambda feat: (jax.ShapeDtypeStruct((depth, B, feat, S), F32),
                                  pl.BlockSpec((None, None, feat, tm), lambda i: (layer, i // tps, 0, i % tps)))
        outs.update(k=state_out(A_WIDTH), v=state_out(A_WIDTH), ik=state_out(IDX_DIM))
        if state is not None:
            names = list(k for k, o in outs.items() if o is not None)
            for buf, name in zip(state, ("k", "v", "ik")):
                aliases[len(args)] = names.index(name)
                args.append(buf)
                in_specs.append(pl.BlockSpec(memory_space=pl.ANY))
    outs = {k: o for k, o in outs.items() if o is not None}
    kern = functools.partial(_proj_kernel, stacked=stack is not None, n_alias=len(aliases))
    res = pl.pallas_call(
        kern, grid=(n,), in_specs=in_specs, out_specs=[o[1] for o in outs.values()],
        out_shape=[o[0] for o in outs.values()], input_output_aliases=aliases,
        compiler_params=_cparams(1), name="proj",
    )(*args)
    return dict(zip(outs.keys(), res))


def _sortable_key(s):
    b = lax.bitcast_convert_type(s, I32)
    key = jnp.where(b < 0, b ^ jnp.int32(0x7FFFFFFF), b)
    return jnp.where(key == -1, 0, key)


def _fold_lanes(m):
    out = m[:, 0:LANES]
    for c in range(1, m.shape[1] // LANES):
        out = out + m[:, c * LANES:(c + 1) * LANES]
    return out


def _fold_sublanes(m):
    out = m[0:SUBLANES]
    for r in range(1, m.shape[0] // SUBLANES):
        out = out + m[r * SUBLANES:(r + 1) * SUBLANES]
    return out


I16 = jnp.int16
PACK16 = 2 * SUBLANES


def _fold_pack16(m):
    out = m[0:PACK16]
    for r in range(1, m.shape[0] // PACK16):
        out = out + m[r * PACK16:(r + 1) * PACK16]
    return out


def _tied_below_counter(kint_ref, iota_ref, eq_ref, t, n_tiles, tile, rows):
    iota_ref[0:tile, :] = lax.broadcasted_iota(I32, (tile, rows), 0).astype(I16)

    def build(j, carry):
        off = pl.multiple_of(j * tile, tile)
        eq_ref[pl.ds(off, tile), :] = jnp.where(kint_ref[pl.ds(off, tile), :] == t, 1, 0).astype(I16)
        return carry

    lax.fori_loop(0, n_tiles, build, 0)

    def counter(cand):
        def body(j, c):
            off = pl.multiple_of(j * tile, tile)
            local = jnp.clip(cand - off, 0, tile).astype(I16)
            m = jnp.where(iota_ref[0:tile, :] < local, eq_ref[pl.ds(off, tile), :], jnp.int16(0))
            return c + _fold_pack16(m)
        part = lax.fori_loop(0, n_tiles, body, jnp.zeros((PACK16, rows), I16))
        return jnp.sum(part.astype(F32), axis=0, keepdims=True)

    return counter


def _kth_largest_by_halves(khi_ref, klo_ref, low_ref, n_tiles, tile, rows, kf):
    def count_ge(ref, cand):
        c16 = cand.astype(I16)

        def body(j, c):
            off = pl.multiple_of(j * tile, tile)
            m = jnp.where(ref[pl.ds(off, tile), :] >= c16, jnp.int16(1), jnp.int16(0))
            return c + _fold_pack16(m)
        part = lax.fori_loop(0, n_tiles, body, jnp.zeros((PACK16, rows), I16))
        return jnp.sum(part.astype(F32), axis=0, keepdims=True)

    def bisect(ref):
        c0 = count_ge(ref, jnp.zeros((1, rows), I32))
        ok = c0 >= kf
        state = (jnp.where(ok, jnp.int32(0), jnp.int32(-32768)), jnp.where(ok, c0, 0.0))

        def bit_body(b, state):
            t, c = state
            cand = t + jnp.left_shift(jnp.int32(1), 14 - b)
            cnt = count_ge(ref, cand)
            ok = cnt >= kf
            return jnp.where(ok, cand, t), jnp.where(ok, cnt, c)

        return lax.fori_loop(0, 15, bit_body, state)

    hi, cnt_hi = bisect(khi_ref)
    hi16 = hi.astype(I16)

    def build(j, carry):
        off = pl.multiple_of(j * tile, tile)
        kh = khi_ref[pl.ds(off, tile), :]
        low_ref[pl.ds(off, tile), :] = jnp.where(
            kh > hi16, jnp.int16(32767), jnp.where(kh == hi16, klo_ref[pl.ds(off, tile), :], jnp.int16(-32768)))
        return carry

    lax.fori_loop(0, n_tiles, build, 0)
    lo, cnt = bisect(low_ref)
    return jnp.left_shift(hi, 16) + (lo + 32768), jnp.where(lo == -32768, cnt_hi, cnt)


def _topk_threshold(kint_ref, thr_ref, cut_ref, n_tiles, tile, rows, topk, n_cols, keys_axis=1, halves=None):
    stat_shape = (rows, 1) if keys_axis == 1 else (1, rows)

    def count(pred):
        def body(j, c):
            off = pl.multiple_of(j * tile, tile)
            if keys_axis == 1:
                key = kint_ref[:, pl.ds(off, tile)]
                idx = off + lax.broadcasted_iota(I32, (rows, tile), 1)
                return c + _fold_lanes(pred(key, idx).astype(F32))
            key = kint_ref[pl.ds(off, tile), :]
            idx = off + lax.broadcasted_iota(I32, (tile, rows), 0)
            return c + _fold_sublanes(pred(key, idx).astype(F32))
        init = jnp.zeros((rows, LANES) if keys_axis == 1 else (SUBLANES, rows), F32)
        part = lax.fori_loop(0, n_tiles, body, init)
        return jnp.sum(part, axis=keys_axis, keepdims=True)

    kf = float(topk)
    if halves is None:
        c0 = count(lambda key, col: key >= 0)
        t = jnp.where(c0 >= kf, jnp.int32(0), jnp.int32(INT_MIN))

        def bit_body(b, t):
            cand = t + jnp.left_shift(jnp.int32(1), 30 - b)
            cnt = count(lambda key, col: key >= cand)
            return jnp.where(cnt >= kf, cand, t)

        t = lax.fori_loop(0, 31, bit_body, t)
        t = jnp.maximum(t, jnp.int32(INT_MIN + 1))
        cge = count(lambda key, col: key >= t)
    else:
        t, cge = _kth_largest_by_halves(*halves, n_tiles, tile, rows, kf)
        t = jnp.maximum(t, jnp.int32(INT_MIN + 1))
    thr_ref[...] = t
    cut_ref[...] = jnp.full(stat_shape, n_cols, I32)

    @pl.when(jnp.max(cge) > kf)
    def _():
        need = kf - count(lambda key, col: key > t)
        nbits = max(1, (n_cols - 1).bit_length())

        if halves is None:
            count_tied_below = lambda cand: count(lambda key, col: (key == t) & (col < cand))
        else:
            count_tied_below = _tied_below_counter(kint_ref, halves[1], halves[2], t, n_tiles, tile, rows)

        def idx_body(b, p):
            cand = p + jnp.left_shift(jnp.int32(1), nbits - 1 - b)
            return jnp.where(count_tied_below(cand) < need, cand, p)

        p = lax.fori_loop(0, nbits, idx_body, jnp.zeros(stat_shape, I32))
        cut_ref[...] = jnp.where(cge > kf, p, jnp.int32(n_cols))


def _selected(key, col, t, cut):
    return (key > t) | ((key == t) & (col <= cut))


KEY_TILE = 128


def _dsa_prompt_kernel(qb_ref, iqb_ref, ikw_ref, kb_ref, vt_ref, ikd_ref, o_ref,
                       kint_ref, khi_ref, klo_ref, low_ref,
                       qm_ref, iqm_ref, w_ref, s0_ref, s1_ref, p0_ref, p1_ref, a0_ref, a1_ref,
                       acc_ref, m_ref, l_ref, thr_ref, cut_ref, *, topk, qb, seq):
    i = pl.program_id(1)
    kt = KEY_TILE
    assert qb == 2 * kt
    n_tiles = 2 * (i + 1)
    last = n_tiles - 1
    s_slots, p_slots, a_slots = (s0_ref, s1_ref), (p0_ref, p1_ref), (a0_ref, a1_ref)
    lane = lax.broadcasted_iota(I32, (qb, LANES), 1)
    lo = lane < HEAD_DIM
    for h in range(N_HEADS):
        sl = slice(LANES * (h // 2), LANES * (h // 2 + 1))
        msk = lo if h % 2 == 0 else jnp.logical_not(lo)
        qm_ref[h] = jnp.where(msk, qb_ref[0, :, sl], jnp.zeros((), qb_ref.dtype))
        iqm_ref[h] = jnp.where(msk, iqb_ref[0, :, sl], jnp.zeros((), iqb_ref.dtype))
    w_ref[...] = jnp.transpose(ikw_ref[0])[IDX_DIM:IDX_DIM + IDX_HEADS, :]
    qpos = i * qb + lax.broadcasted_iota(I32, (1, qb), 1)

    def tile_off(tile):
        return pl.multiple_of(jnp.clip(tile, 0, last) * kt, kt)

    def idx_dots(tile, s_out):
        ik = ikd_ref[0, pl.ds(tile_off(tile), kt), :]
        for h in range(IDX_HEADS):
            s_out[h] = lax.dot_general(ik, iqm_ref[h], _NT, preferred_element_type=F32)

    def combine(tile, s_in):
        sc = jnp.zeros((kt, qb), F32)
        for h in range(IDX_HEADS):
            sc = sc + jnp.maximum(s_in[h], 0.0) * w_ref[h:h + 1, :]
        off = tile_off(tile)
        kpos = off + lax.broadcasted_iota(I32, (kt, qb), 0)
        key = jnp.where(kpos <= qpos, _sortable_key(sc), jnp.int32(INT_MIN))
        kint_ref[pl.ds(off, kt), :] = key
        khi_ref[pl.ds(off, kt), :] = jnp.right_shift(key, 16).astype(I16)
        klo_ref[pl.ds(off, kt), :] = ((key & 0xFFFF) - 32768).astype(I16)

    idx_dots(0, s0_ref)

    def score_pair(jj, carry):
        for step in range(2):
            tile = 2 * jj + step
            combine(tile, s_slots[step])
            idx_dots(tile + 1, s_slots[1 - step])
        return carry

    lax.fori_loop(0, i + 1, score_pair, 0)
    pad = pl.ds(pl.multiple_of((i + 1) * qb, qb), qb)
    kint_ref[pad, :] = jnp.full((qb, qb), INT_MIN, I32)
    khi_ref[pad, :] = jnp.full((qb, qb), -32768, I16)
    klo_ref[pad, :] = jnp.full((qb, qb), -32768, I16)
    _topk_threshold(kint_ref, thr_ref, cut_ref, (i + 2) // 2, 2 * qb, qb, topk, seq, keys_axis=0,
                    halves=(khi_ref, klo_ref, low_ref))
    thr = thr_ref[...]
    cut = cut_ref[...]

    acc_ref[...] = jnp.zeros_like(acc_ref)
    m_ref[...] = jnp.full(m_ref.shape, NEG_BIG, F32)
    l_ref[...] = jnp.zeros_like(l_ref)
    p1_ref[...] = jnp.zeros_like(p1_ref)
    a1_ref[...] = jnp.ones_like(a1_ref)

    def qk_dots(tile, s_out):
        off = tile_off(tile)
        for h in range(N_HEADS):
            kk = kb_ref[0, pl.ds(off, kt), LANES * (h // 2):LANES * (h // 2 + 1)]
            s_out[h] = lax.dot_general(kk, qm_ref[h], _NT, preferred_element_type=F32)

    def softmax(tile, s_in, p_out, a_out):
        off = tile_off(tile)
        live = tile <= last
        kpos = off + lax.broadcasted_iota(I32, (kt, qb), 0)
        sel = _selected(kint_ref[pl.ds(off, kt), :], kpos,
                        jnp.where(live, thr, jnp.int32(2 ** 31 - 1)), jnp.where(live, cut, jnp.int32(-1)))
        bias = jnp.where(sel, 0.0, NEG_BIG)
        for h in range(N_HEADS):
            s = s_in[h] + bias
            m_old = m_ref[h]
            m_new = jnp.maximum(m_old, jnp.max(s, axis=0, keepdims=True))
            alpha = jnp.exp2(m_old - m_new)
            pm = jnp.exp2(s - m_new)
            m_ref[h] = m_new
            p_out[h] = pm.astype(p_out.dtype)
            a_out[h] = alpha

    ones_rows = jnp.ones((PACK16, kt), vt_ref.dtype)

    def pv_dots(tile, p_in, a_in):
        off = tile_off(tile)
        for h in range(N_HEADS):
            dr = slice(HEAD_DIM * h, HEAD_DIM * (h + 1))
            lhs = jnp.concatenate([vt_ref[dr, pl.ds(off, kt)], ones_rows], axis=0)
            pv = jnp.dot(lhs, p_in[h], preferred_element_type=F32)
            acc_ref[dr, :] = a_in[h] * acc_ref[dr, :] + pv[:HEAD_DIM]
            l_ref[h] = a_in[h] * l_ref[h] + pv[HEAD_DIM:HEAD_DIM + 1]

    qk_dots(0, s0_ref)

    def att_pair(jj, carry):
        for step in range(2):
            tile = 2 * jj + step
            softmax(tile, s_slots[step], p_slots[step], a_slots[step])
            qk_dots(tile + 1, s_slots[1 - step])
            pv_dots(tile - 1, p_slots[1 - step], a_slots[1 - step])
        return carry

    lax.fori_loop(0, i + 2, att_pair, 0)
    for h in range(N_HEADS):
        dr = slice(HEAD_DIM * h, HEAD_DIM * (h + 1))
        acc_ref[dr, :] = acc_ref[dr, :] / l_ref[h]
    o_ref[0] = jnp.transpose(acc_ref[...]).astype(o_ref.dtype)


def _dsa_prompt(qb_, iqb, ikw, kb, vt, ikd, *, qb, topk):
    B, S, _ = qb_.shape
    mxu = qb_.dtype
    blk = lambda width: pl.BlockSpec((1, qb, width), lambda b, i: (b, i, 0))
    full = lambda width: pl.BlockSpec((1, S, width), lambda b, i: (b, 0, 0))
    kern = functools.partial(_dsa_prompt_kernel, topk=topk, qb=qb, seq=S)
    return pl.pallas_call(
        kern, grid=(B, S // qb),
        in_specs=[blk(A_WIDTH), blk(A_WIDTH), blk(LANES), full(A_WIDTH),
                  pl.BlockSpec((A_WIDTH, S), lambda b, i: (0, b)), full(LANES)],
        out_specs=blk(A_WIDTH),
        out_shape=jax.ShapeDtypeStruct((B, S, A_WIDTH), mxu),
        scratch_shapes=[
            pltpu.VMEM((S + qb, qb), I32),
            pltpu.VMEM((S + qb, qb), I16), pltpu.VMEM((S + qb, qb), I16), pltpu.VMEM((S + qb, qb), I16),
            pltpu.VMEM((N_HEADS, qb, LANES), mxu),
            pltpu.VMEM((N_HEADS, qb, LANES), mxu),
            pltpu.VMEM((IDX_HEADS, qb), F32),
            pltpu.VMEM((N_HEADS, KEY_TILE, qb), F32),
            pltpu.VMEM((N_HEADS, KEY_TILE, qb), F32),
            pltpu.VMEM((N_HEADS, KEY_TILE, qb), mxu),
            pltpu.VMEM((N_HEADS, KEY_TILE, qb), mxu),
            pltpu.VMEM((N_HEADS, 1, qb), F32),
            pltpu.VMEM((N_HEADS, 1, qb), F32),
            pltpu.VMEM((A_WIDTH, qb), F32),
            pltpu.VMEM((N_HEADS, 1, qb), F32),
            pltpu.VMEM((N_HEADS, 1, qb), F32),
            pltpu.VMEM((1, qb), I32),
            pltpu.VMEM((1, qb), I32),
        ],
        compiler_params=_cparams(2), name="dsa_prompt",
    )(qb_, iqb, ikw, kb, vt, ikd)


PAGES_PER_STEP = 32


def _head_sum(s):
    return jnp.concatenate(
        [jnp.sum(s[IDX_HEADS * t:IDX_HEADS * (t + 1)], axis=0, keepdims=True) for t in range(s.shape[0] // IDX_HEADS)],
        axis=0)


def _sample_scores_kernel(pt_ref, iq_ref, w_ref, ikn_ref, *rest, n_steps):
    pages = rest[:PAGES_PER_STEP]
    sc_ref, scn_ref = rest[PAGES_PER_STEP:]
    c = pl.program_id(1)
    iq = iq_ref[0]
    w = w_ref[0]

    def scores(ik):
        s = jnp.dot(iq, ik.astype(iq.dtype), preferred_element_type=F32)
        return _head_sum(jnp.maximum(s, 0.0) * w)

    sc_ref[0] = scores(jnp.concatenate([r[...].astype(iq.dtype) for r in pages], axis=1))

    @pl.when(c == n_steps - 1)
    def _():
        s = scores(ikn_ref[0])
        col = lax.broadcasted_iota(I32, s.shape, 1)
        row = lax.broadcasted_iota(I32, s.shape, 0)
        scn_ref[0] = jnp.where(col <= row, s, -jnp.inf)


def _page_specs(layer, rows, page):
    def make(r):
        def imap(b, c, pt):
            return (layer, pt[b, c * PAGES_PER_STEP + r], 0, 0)
        return pl.BlockSpec((None, None, rows, page), imap)
    return [make(r) for r in range(PAGES_PER_STEP)]


def _sample_scores(page_table, iq_rows, w_rows, ik_new, cache_ik, *, layer):
    DB, n_pages = page_table.shape
    page = cache_ik.shape[3]
    n_steps = n_pages // PAGES_PER_STEP
    rows = iq_rows.shape[1]
    nt = rows // IDX_HEADS
    step_cols = PAGES_PER_STEP * page
    grid_spec = pltpu.PrefetchScalarGridSpec(
        num_scalar_prefetch=1, grid=(DB, n_steps),
        in_specs=[
            pl.BlockSpec((1, rows, IDX_DIM), lambda b, c, pt: (b, 0, 0)),
            pl.BlockSpec((1, rows, 1), lambda b, c, pt: (b, 0, 0)),
            pl.BlockSpec((1, IDX_DIM, page), lambda b, c, pt: (b, 0, 0)),
        ] + _page_specs(layer, IDX_DIM, page),
        out_specs=[pl.BlockSpec((1, nt, step_cols), lambda b, c, pt: (b, 0, c)),
                   pl.BlockSpec((1, nt, page), lambda b, c, pt: (b, 0, 0))],
    )
    kern = functools.partial(_sample_scores_kernel, n_steps=n_steps)
    return pl.pallas_call(
        kern, grid_spec=grid_spec,
        out_shape=[jax.ShapeDtypeStruct((DB, nt, n_pages * page), F32), jax.ShapeDtypeStruct((DB, nt, page), F32)],
        compiler_params=_cparams(2), name="sample_scores",
    )(page_table, iq_rows, w_rows, ik_new, *([cache_ik] * PAGES_PER_STEP))


def _score_key(sc):
    return jnp.where(sc == -jnp.inf, jnp.int32(INT_MIN), _sortable_key(sc))


def _sample_threshold_kernel(sc_ref, scn_ref, thr_ref, cut_ref, kint_ref, t_ref, c_ref, *, topk, tile):
    rows, past = sc_ref.shape
    page = scn_ref.shape[1]
    cols = kint_ref.shape[1]
    kint_ref[:, 0:past] = _score_key(sc_ref[...])
    kint_ref[:, past:past + page] = _score_key(scn_ref[...])
    kint_ref[:, past + page:] = jnp.full((rows, cols - past - page), INT_MIN, I32)
    _topk_threshold(kint_ref, t_ref, c_ref, cols // tile, tile, rows, topk, cols)
    thr_ref[...] = jnp.broadcast_to(t_ref[...], thr_ref.shape)
    cut_ref[...] = jnp.broadcast_to(c_ref[...], cut_ref.shape)


def _sample_threshold(sc, sc_new, *, topk):
    rows, past = sc.shape
    page = sc_new.shape[1]
    tile = 2 * LANES
    cols = pl.cdiv(past + page, tile) * tile
    kern = functools.partial(_sample_threshold_kernel, topk=topk, tile=tile)
    return pl.pallas_call(
        kern, grid=(1,),
        in_specs=[_const_spec(sc.shape), _const_spec(sc_new.shape)],
        out_specs=[_const_spec((rows, LANES)), _const_spec((rows, LANES))],
        out_shape=[jax.ShapeDtypeStruct((rows, LANES), I32)] * 2,
        scratch_shapes=[pltpu.VMEM((rows, cols), I32), pltpu.VMEM((rows, 1), I32), pltpu.VMEM((rows, 1), I32)],
        compiler_params=_cparams(1), name="sample_threshold",
    )(sc, sc_new)


def _sample_attn_kernel(pt_ref, q_ref, sc_ref, scn_ref, thr_ref, cut_ref, kn_ref, vn_ref, hm_ref, *rest, n_steps):
    kp = rest[:PAGES_PER_STEP]
    vp = rest[PAGES_PER_STEP:2 * PAGES_PER_STEP]
    o_ref = rest[2 * PAGES_PER_STEP]
    m_ref, l_ref, acc_ref = rest[2 * PAGES_PER_STEP + 1:]
    c = pl.program_id(1)
    q = q_ref[0]
    mxu = q.dtype
    nt = sc_ref.shape[1]
    step_cols = sc_ref.shape[2]

    @pl.when(c == 0)
    def _():
        m_ref[...] = jnp.full(m_ref.shape, NEG_BIG, F32)
        l_ref[...] = jnp.zeros_like(l_ref)
        acc_ref[...] = jnp.zeros_like(acc_ref)

    def rows_th(x):
        return jnp.concatenate([jnp.broadcast_to(x[t:t + 1], (N_HEADS, x.shape[1])) for t in range(nt)], axis=0)

    def update(keys, vals, sc, col0):
        key = _score_key(sc)
        col = col0 + lax.broadcasted_iota(I32, key.shape, 1)
        sel = rows_th(_selected(key, col, thr_ref[0, :, 0:1], cut_ref[0, :, 0:1]).astype(F32)) > 0.5
        kcat = jnp.concatenate([kk.astype(mxu) for kk in keys], axis=1)
        s = jnp.dot(q, kcat, preferred_element_type=F32)
        s = jnp.where(sel, s, NEG_BIG)
        m_old = m_ref[...]
        m_new = jnp.maximum(m_old, jnp.max(s, axis=1, keepdims=True))
        alpha = jnp.exp2(m_old - m_new)
        pm = jnp.exp2(s - m_new)
        l_ref[...] = alpha * l_ref[...] + jnp.sum(pm, axis=1, keepdims=True)
        m_ref[...] = m_new
        vcat = jnp.concatenate([vv.astype(mxu) for vv in vals], axis=1)
        pv = lax.dot_general(pm.astype(mxu), vcat, _NT, preferred_element_type=F32)
        acc_ref[...] = alpha * acc_ref[...] + pv

    update([r[...] for r in kp], [r[...] for r in vp], sc_ref[0], c * step_cols)

    @pl.when(c == n_steps - 1)
    def _():
        update([kn_ref[0]], [vn_ref[0]], scn_ref[0], n_steps * step_cols)
        out = acc_ref[...] / l_ref[...] * hm_ref[...]
        o_ref[0] = _head_sum(out).astype(o_ref.dtype)


def _sample_attn(page_table, q_rows, sc, sc_new, thr, cut, k_new, v_new, head_mask, cache_k, cache_v, *, layer):
    DB, n_pages = page_table.shape
    page = cache_k.shape[3]
    n_steps = n_pages // PAGES_PER_STEP
    rows = q_rows.shape[1]
    nt = rows // N_HEADS
    step_cols = PAGES_PER_STEP * page
    per_b = lambda shape: pl.BlockSpec((1,) + shape, lambda b, c, pt: (b, 0, 0))
    grid_spec = pltpu.PrefetchScalarGridSpec(
        num_scalar_prefetch=1, grid=(DB, n_steps),
        in_specs=[
            per_b((rows, A_WIDTH)),
            pl.BlockSpec((1, nt, step_cols), lambda b, c, pt: (b, 0, c)),
            per_b((nt, page)), per_b((nt, LANES)), per_b((nt, LANES)),
            per_b((A_WIDTH, page)), per_b((A_WIDTH, page)),
            pl.BlockSpec((rows, A_WIDTH), lambda b, c, pt: (0, 0)),
        ] + _page_specs(layer, A_WIDTH, page) * 2,
        out_specs=per_b((nt, A_WIDTH)),
        scratch_shapes=[pltpu.VMEM((rows, 1), F32), pltpu.VMEM((rows, 1), F32), pltpu.VMEM((rows, A_WIDTH), F32)],
    )
    kern = functools.partial(_sample_attn_kernel, n_steps=n_steps)
    return pl.pallas_call(
        kern, grid_spec=grid_spec,
        out_shape=jax.ShapeDtypeStruct((DB, nt, A_WIDTH), q_rows.dtype),
        compiler_params=_cparams(2), name="sample_attn",
    )(page_table, q_rows, sc, sc_new, thr, cut, k_new, v_new, head_mask,
      *([cache_k] * PAGES_PER_STEP), *([cache_v] * PAGES_PER_STEP))


def _shift_rows(e, k):
    return pltpu.roll(e, k, 0)


def _merge_kernel(x_ref, att_ref, u_ref, vv_ref, xc_ref, halo_ref, cnt_ref,
                  g1_ref, wgt_ref, wa_ref, wb_ref, wc_ref, wout_ref, ws_ref, btab_ref, wpool_ref, pscale_ref,
                  o_ref, *, row_stride, tiles_per_seq, zero_first_halo):
    mxu = wgt_ref.dtype
    tm = x_ref.shape[0]
    x = x_ref[...]
    xn = _rms(x, g1_ref[...]).astype(mxu)
    d = x.shape[1]

    def gate(n):
        return jax.nn.sigmoid(jnp.dot(xn, wgt_ref[:, n * d:(n + 1) * d], preferred_element_type=F32))

    merged = gate(0) * jnp.dot(att_ref[...], wa_ref[...], preferred_element_type=F32)

    lane = lax.broadcasted_iota(I32, (SGU_CHUNK, LANES), 1)
    lo = lane < (SGU_WIDTH // SGU_GROUPS)
    sgo = []
    for c in range(tm // SGU_CHUNK):
        rs = slice(c * SGU_CHUNK, (c + 1) * SGU_CHUNK)
        vvb = vv_ref[rs, :].astype(mxu)
        mix = []
        for p in range(SGU_GROUPS // 2):
            pair = vvb[:, LANES * p:LANES * (p + 1)]
            r0 = jnp.dot(ws_ref[2 * p], pair, preferred_element_type=F32)
            r1 = jnp.dot(ws_ref[2 * p + 1], pair, preferred_element_type=F32)
            mix.append(jnp.where(lo, r0, r1))
        sgo.append(u_ref[rs, :] * (jnp.concatenate(mix, axis=1) + btab_ref[...]))
    sgo = jnp.concatenate(sgo, axis=0).astype(mxu)
    merged = merged + gate(1) * jnp.dot(sgo, wb_ref[...], preferred_element_type=F32)

    xc = xc_ref[...]
    halo = halo_ref[...]
    if zero_first_halo:
        first = (pl.program_id(0) % tiles_per_seq) == 0
        halo = jnp.where(first, jnp.zeros_like(halo), halo)
    hp = halo.shape[0]
    e = jnp.concatenate([halo, xc], axis=0)
    s1 = e + _shift_rows(e, row_stride)
    s2 = s1 + _shift_rows(s1, 2 * row_stride)
    s3 = s2 + _shift_rows(s2, 4 * row_stride)
    s4 = s3 + _shift_rows(s3, 8 * row_stride)
    gd = POOL_WIDTH // POOL_GROUPS
    wsum = jnp.concatenate([s[hp:, g * gd:(g + 1) * gd] for g, s in enumerate((s1, s2, s3, s4))], axis=1)
    pooled = (wsum / cnt_ref[...] - xc).astype(mxu)
    po = (jnp.dot(pooled, wpool_ref[...], preferred_element_type=F32) * pscale_ref[...]).astype(mxu)
    merged = merged + gate(2) * jnp.dot(po, wc_ref[...], preferred_element_type=F32)

    o_ref[...] = x + jnp.dot(merged.astype(mxu), wout_ref[...], preferred_element_type=F32)


def _merge(x, att, u, vv, xc, halo, cnt, g1, wgt, wa, wb, wc, wout, ws, btab, wpool, pscale,
           *, tm, row_stride, tiles_per_seq, halo_rows, halo_from_xc, cnt_tiles):
    T, D = x.shape
    n = T // tm
    row = lambda width: pl.BlockSpec((tm, width), lambda i: (i, 0))
    if halo_from_xc:
        per = tm // halo_rows
        halo_spec = pl.BlockSpec((halo_rows, xc.shape[1]), lambda i: (jnp.maximum(i * per - 1, 0), 0))
    else:
        halo_spec = _const_spec(halo.shape)
    consts = [g1, wgt, wa, wb, wc, wout, ws, btab, wpool, pscale]
    kern = functools.partial(_merge_kernel, row_stride=row_stride, tiles_per_seq=tiles_per_seq,
                             zero_first_halo=halo_from_xc)
    return pl.pallas_call(
        kern, grid=(n,),
        in_specs=[row(D), row(att.shape[1]), row(u.shape[1]), row(vv.shape[1]), row(xc.shape[1]), halo_spec,
                  pl.BlockSpec((tm, cnt.shape[1]), lambda i: (i % cnt_tiles, 0))]
        + [_const_spec(c.shape) for c in consts],
        out_specs=row(D), out_shape=jax.ShapeDtypeStruct((T, D), F32),
        compiler_params=_cparams(1), name="merge",
    )(x, att, u, vv, xc, halo, cnt, *consts)


FFN_CHUNK = 256


def _ffn_kernel(x_ref, g2_ref, wa_ref, wu_ref, cw_ref, cb_ref, wd_ref, halo_ref, o_ref, tail_ref, carry_ref,
                *, row_stride, tiles_per_seq, use_carry):
    mxu = wa_ref.dtype
    x = x_ref[...]
    tm = x.shape[0]
    xn = _rms(x, g2_ref[...]).astype(mxu)
    dff = wa_ref.shape[1]
    tail = tail_ref.shape[1]
    acc = jnp.zeros(x.shape, F32)
    if use_carry:
        first = (pl.program_id(0) % tiles_per_seq) == 0
    for c0 in range(0, dff, FFN_CHUNK):
        cs = slice(c0, c0 + FFN_CHUNK)
        a = jnp.dot(xn, wa_ref[:, cs], preferred_element_type=F32)
        up = jnp.dot(xn, wu_ref[:, cs], preferred_element_type=F32)
        if use_carry:
            halo = jnp.where(first, 0.0, carry_ref[:, cs])
            carry_ref[:, cs] = a[tm - carry_ref.shape[0]:, :]
        else:
            halo = halo_ref[:, cs]
        hp = halo.shape[0]
        e = jnp.concatenate([halo, a], axis=0)
        p1 = _shift_rows(e, row_stride)[hp:]
        p2 = _shift_rows(e, 2 * row_stride)[hp:]
        conv = p2 * cw_ref[0:1, cs] + p1 * cw_ref[1:2, cs] + a * cw_ref[2:3, cs] + cb_ref[:, cs]
        act = (jax.nn.silu(conv) * up).astype(mxu)
        acc = acc + jnp.dot(act, wd_ref[cs, :], preferred_element_type=F32)
        tail_ref[0, :, cs] = a[tm - tail:, :]
    o_ref[...] = x + acc


def _ffn(x, g2, wa, wu, cw, cb, wd, halo, *, tm, row_stride, tiles_per_seq, use_carry, tail):
    T, D = x.shape
    n = T // tm
    dff = wa.shape[1]
    row = pl.BlockSpec((tm, D), lambda i: (i, 0))
    consts = [g2, wa, wu, cw, cb, wd, halo]
    kern = functools.partial(_ffn_kernel, row_stride=row_stride, tiles_per_seq=tiles_per_seq, use_carry=use_carry)
    return pl.pallas_call(
        kern, grid=(n,),
        in_specs=[row] + [_const_spec(c.shape) for c in consts],
        out_specs=[row, pl.BlockSpec((1, tail, dff), lambda i: (i, 0, 0))],
        out_shape=[jax.ShapeDtypeStruct((T, D), F32), jax.ShapeDtypeStruct((n, tail, dff), F32)],
        scratch_shapes=[pltpu.VMEM((SUBLANES, dff), F32)],
        compiler_params=_cparams(1), name="ffn",
    )(x, *consts)


def _pack_w_in(w_in):
    d = w_in.shape[0]
    o = 0
    q, k, v = (w_in[:, o + i * A_WIDTH:o + (i + 1) * A_WIDTH] for i in range(3))
    o += 3 * A_WIDTH
    iq = w_in[:, o:o + IDX_HEADS * IDX_DIM]
    o += IDX_HEADS * IDX_DIM
    ik = w_in[:, o:o + IDX_DIM]
    o += IDX_DIM
    iw = w_in[:, o:o + IDX_HEADS]
    o += IDX_HEADS
    sg = w_in[:, o:o + 2 * SGU_WIDTH]
    o += 2 * SGU_WIDTH
    xc = w_in[:, o:o + POOL_WIDTH]
    o += POOL_WIDTH
    gt = w_in[:, o:]
    pad = jnp.zeros((d, LANES - IDX_DIM - IDX_HEADS), w_in.dtype)
    packed = jnp.concatenate([q, k, v, iq, ik, iw, pad, ik, ik, sg, xc], axis=1)
    assert packed.shape[1] == PROJ_COLS
    return packed, gt


def _rope_tables(pos):
    posf = np.asarray(pos, np.float64)[:, None]

    def cs(half):
        inv = ROPE_THETA ** (-np.arange(half, dtype=np.float64) / half)
        ang = posf * inv[None, :]
        return np.cos(ang), np.sin(ang)

    t = posf.shape[0]
    c32, s32 = cs(HEAD_DIM // 2)
    cqk = np.tile(np.concatenate([c32, c32], axis=1), (1, 2))
    sqk = np.tile(np.concatenate([-s32, s32], axis=1), (1, 2))
    c16, s16 = cs(IDX_ROPE // 2)
    rest = IDX_DIM - IDX_ROPE
    ci = np.concatenate([c16, c16, np.ones((t, rest))], axis=1)
    si = np.concatenate([-s16, s16, np.zeros((t, rest))], axis=1)
    ciq, siq = np.tile(ci, (1, 2)), np.tile(si, (1, 2))
    wpad = LANES - IDX_DIM - IDX_HEADS
    cikw = np.concatenate([ci, np.full((t, IDX_HEADS), IDX_HEADS ** -0.5), np.zeros((t, wpad))], axis=1)
    sikw = np.concatenate([si, np.zeros((t, LANES - IDX_DIM))], axis=1)
    return jnp.asarray(np.concatenate([cqk, sqk, ciq, siq, cikw, sikw], axis=1).astype(np.float32))


def _block_diag(blocks):
    n = len(blocks)
    r, c = blocks[0].shape
    out = jnp.zeros((n * r, n * c), blocks[0].dtype)
    for i, b in enumerate(blocks):
        out = out.at[i * r:(i + 1) * r, i * c:(i + 1) * c].set(b)
    return out


def _pool_counts(pos):
    gd = POOL_WIDTH // POOL_GROUPS
    pos = np.asarray(pos)
    cols = [np.broadcast_to(np.minimum(pos + 1, w).astype(np.float32)[:, None], (pos.shape[0], gd))
            for w in POOL_WINDOWS]
    return jnp.asarray(np.concatenate(cols, axis=1))


def _head_segments(dtype):
    return jnp.asarray(np.kron(np.eye(N_HEADS, dtype=np.float32), np.ones((HEAD_DIM, HEAD_DIM), np.float32)), dtype)


def _layer_weights(l, norm1_g, w_in, q_norm_g, k_norm_g, sgu_w, sgu_b, sgu_norm_g, pool_w, pool_scale,
                   w_br_a, w_br_b, w_br_c, w_out, norm2_g, w_ff_in, ff_conv_w, ff_conv_b, w_ff_down):
    mxu = MXU_DTYPE
    packed, gt = _pack_w_in(w_in[l])
    dff = w_ff_in.shape[2] // 2
    tril = jnp.tril(jnp.ones((SGU_CHUNK, SGU_CHUNK), bool))
    return dict(
        g1=norm1_g[l][None, :], w_proj=packed.astype(mxu), w_gt=gt.astype(mxu),
        qg=jnp.tile(q_norm_g[l], N_HEADS)[None, :], kg=jnp.tile(k_norm_g[l], N_HEADS)[None, :],
        sgg=sgu_norm_g[l][None, :],
        ws=jnp.where(tril[None], sgu_w[l], 0.0),
        sgu_b=sgu_b[l],
        wpool=_block_diag([pool_w[l, g] for g in range(POOL_GROUPS)]).astype(mxu),
        pscale=pool_scale[l][None, :],
        wa=w_br_a[l].astype(mxu), wb=w_br_b[l].astype(mxu), wc=w_br_c[l].astype(mxu), wout=w_out[l].astype(mxu),
        g2=norm2_g[l][None, :],
        w_ffa=w_ff_in[l, :, :dff].astype(mxu), w_ffu=w_ff_in[l, :, dff:].astype(mxu),
        cw=jnp.pad(ff_conv_w[l], ((0, SUBLANES - CONV_W), (0, 0))), cb=ff_conv_b[l][None, :],
        w_ffd=w_ff_down[l].astype(mxu),
    )


def _sgu_bias_table(sgu_b, t_of_row):
    gd = SGU_WIDTH // SGU_GROUPS
    return jnp.repeat(jnp.transpose(sgu_b)[t_of_row], gd, axis=1)


def _prompt_layer(x, lw, tabs, cnt, l, depth, state, *, B, S):
    mxu = MXU_DTYPE
    T = B * S
    tm_proj, tm_merge, tm_ffn, qb = 512, 512, 512, 256
    pr = _proj(x, lw["g1"], lw["w_proj"], lw["qg"], lw["kg"], lw["sgg"], tabs, _head_segments(mxu),
               tm=tm_proj, tab_tiles=S // tm_proj, stack=(l, depth, B, S, state))
    u, vv, xc = pr["u"], pr["vv"], pr["xc"]
    r3 = lambda a: a.reshape(B, S, a.shape[1])
    topk = min(TOPK_MAX, S // 4)
    att = _dsa_prompt(r3(pr["qb"]), r3(pr["iqb"]), r3(pr["ikw"]), r3(pr["kb"]), pr["vt"], r3(pr["ikd"]),
                      qb=qb, topk=topk).reshape(T, A_WIDTH)
    btab = _sgu_bias_table(lw["sgu_b"], jnp.arange(SGU_CHUNK))
    x1 = _merge(x, att, u, vv, xc, xc, cnt, lw["g1"], lw["w_gt"], lw["wa"], lw["wb"], lw["wc"], lw["wout"],
                lw["ws"].astype(mxu), btab, lw["wpool"], lw["pscale"],
                tm=tm_merge, row_stride=1, tiles_per_seq=S // tm_merge, halo_rows=16, halo_from_xc=True,
                cnt_tiles=S // tm_merge)
    dff = lw["w_ffa"].shape[1]
    x2, tails = _ffn(x1, lw["g2"], lw["w_ffa"], lw["w_ffu"], lw["cw"], lw["cb"], lw["w_ffd"],
                     jnp.zeros((SUBLANES, dff), F32),
                     tm=tm_ffn, row_stride=1, tiles_per_seq=S // tm_ffn, use_carry=True, tail=SUBLANES)
    n_t = S // tm_ffn
    ff_state = tails.reshape(B, n_t, SUBLANES, dff)[:, -1, SUBLANES - (CONV_W - 1):, :]
    pool_state = xc.reshape(B, S, POOL_WIDTH)[:, S - POOL_STATE:, :]
    return x2, (pr["k"], pr["v"], pr["ik"]), (pool_state, ff_state)


def _sample_layer(x, lw, tabs, cnt, l, cache_k, cache_v, cache_ik, state_pool, state_ffn, page_table, *, DB, TS):
    mxu = MXU_DTYPE
    T = TS * DB
    page = cache_ik.shape[3]
    past = page_table.shape[1] * page
    pr = _proj(x, lw["g1"], lw["w_proj"], lw["qg"], lw["kg"], lw["sgg"], tabs, _head_segments(mxu),
               tm=T, tab_tiles=1)
    qb_, k, kb, v, vt, iqb, ikw, ikd, u, vv, xc = (
        pr[n] for n in ("qb", "k", "kb", "v", "vt", "iqb", "ikw", "ikd", "u", "vv", "xc"))
    bm = lambda a: jnp.transpose(a.reshape(TS, DB, a.shape[1]), (1, 0, 2))

    iq_rows = bm(iqb).reshape(DB, TS * IDX_HEADS, IDX_DIM)
    w_rows = bm(ikw)[:, :, IDX_DIM:IDX_DIM + IDX_HEADS].reshape(DB, TS * IDX_HEADS, 1)
    new_t = lambda a: jnp.pad(jnp.transpose(a, (0, 2, 1)), ((0, 0), (0, 0), (0, page - TS)))
    ik_new = new_t(bm(ikd)[:, :, :IDX_DIM])
    sc, sc_new = _sample_scores(page_table, iq_rows, w_rows, ik_new, cache_ik, layer=l)
    topk = min(TOPK_MAX, (past + TS) // 4)
    thr, cut = _sample_threshold(sc.reshape(DB * TS, past), sc_new.reshape(DB * TS, page), topk=topk)
    hm = jnp.asarray(np.repeat(np.eye(N_HEADS, dtype=np.float32), HEAD_DIM, axis=1))
    q_rows = (bm(qb_)[:, :, None, :] * hm[None, None].astype(mxu)).reshape(DB, TS * N_HEADS, A_WIDTH)
    att = _sample_attn(page_table, q_rows, sc, sc_new, thr.reshape(DB, TS, LANES), cut.reshape(DB, TS, LANES),
                       new_t(bm(kb)), new_t(bm(jnp.transpose(vt))), jnp.tile(hm, (TS, 1)), cache_k, cache_v, layer=l)
    att = jnp.transpose(att, (1, 0, 2)).reshape(T, A_WIDTH)

    eye = jnp.eye(DB, dtype=F32)
    ws = jnp.stack([jnp.kron(lw["ws"][g, :TS, :TS], eye) for g in range(SGU_GROUPS)])
    pc = SGU_CHUNK - T
    ws = jnp.pad(ws, ((0, 0), (0, pc), (0, pc)))
    btab = _sgu_bias_table(lw["sgu_b"], jnp.minimum(jnp.arange(SGU_CHUNK) // DB, SGU_CHUNK - 1))
    halo = jnp.concatenate([jnp.zeros((DB, POOL_WIDTH), F32),
                            jnp.transpose(state_pool[l], (1, 0, 2)).reshape(POOL_STATE * DB, POOL_WIDTH)], axis=0)
    x1 = _merge(x, att, u, vv, xc, halo, cnt, lw["g1"], lw["w_gt"], lw["wa"], lw["wb"], lw["wc"], lw["wout"],
                ws.astype(mxu), btab, lw["wpool"], lw["pscale"],
                tm=T, row_stride=DB, tiles_per_seq=1, halo_rows=halo.shape[0], halo_from_xc=False, cnt_tiles=1)
    dff = lw["w_ffa"].shape[1]
    ff_halo = jnp.transpose(state_ffn[l], (1, 0, 2)).reshape((CONV_W - 1) * DB, dff)
    tail = (CONV_W - 1) * DB
    x2, tails = _ffn(x1, lw["g2"], lw["w_ffa"], lw["w_ffu"], lw["cw"], lw["cb"], lw["w_ffd"], ff_halo,
                     tm=T, row_stride=DB, tiles_per_seq=1, use_carry=False, tail=tail)
    ff_state = jnp.transpose(tails.reshape(CONV_W - 1, DB, dff), (1, 0, 2))
    pool_state = jnp.concatenate([state_pool[l], bm(xc)], axis=1)[:, -POOL_STATE:, :]
    hd = lambda a: bm(a).reshape(DB, TS, N_HEADS, HEAD_DIM)
    return x2, (hd(k), hd(v), bm(ikw)[:, :, :IDX_DIM], bm(vv), pool_state, ff_state)


def kernel(x_prompt, x_sample, cache_k, cache_v, cache_idx_k, state_pool, state_ffn_conv, page_table, norm1_g, w_in, q_norm_g, k_norm_g, sgu_w, sgu_b, sgu_norm_g, pool_w, pool_scale, w_br_a, w_br_b, w_br_c, w_out, norm2_g, w_ff_in, ff_conv_w, ff_conv_b, w_ff_down):
    weights = (norm1_g, w_in, q_norm_g, k_norm_g, sgu_w, sgu_b, sgu_norm_g, pool_w, pool_scale,
               w_br_a, w_br_b, w_br_c, w_out, norm2_g, w_ff_in, ff_conv_w, ff_conv_b, w_ff_down)
    B, S, D = x_prompt.shape
    DB, TS, _ = x_sample.shape
    depth = w_in.shape[0]
    page = cache_idx_k.shape[2]
    past = page_table.shape[1] * page
    pool = cache_k.shape[1]
    cache_kt = jnp.transpose(cache_k, (0, 1, 3, 4, 2)).reshape(depth, pool, A_WIDTH, page)
    cache_vt = jnp.transpose(cache_v, (0, 1, 3, 4, 2)).reshape(depth, pool, A_WIDTH, page)
    cache_ikt = jnp.transpose(cache_idx_k, (0, 1, 3, 2))
    assert S % 512 == 0 and S >= POOL_STATE and TS * DB == SGU_CHUNK and TS >= CONV_W - 1 and TS <= page
    assert past % SGU_CHUNK == 0 and page_table.shape[1] % PAGES_PER_STEP == 0

    pos_p = np.arange(S)
    pos_s = past + np.repeat(np.arange(TS), DB)
    tabs_p, tabs_s = _rope_tables(pos_p), _rope_tables(pos_s)
    cnt_p, cnt_s = _pool_counts(pos_p), _pool_counts(pos_s)

    xp = x_prompt.reshape(B * S, D)
    xs = jnp.transpose(x_sample, (1, 0, 2)).reshape(TS * DB, D)
    outs_p, outs_s = [], []
    kv_state = None
    for l in range(depth):
        lw = _layer_weights(l, *weights)
        xp, kv_state, st_p = _prompt_layer(xp, lw, tabs_p, cnt_p, l, depth, kv_state, B=B, S=S)
        xs, st_s = _sample_layer(xs, lw, tabs_s, cnt_s, l, cache_kt, cache_vt, cache_ikt, state_pool,
                                 state_ffn_conv, page_table, DB=DB, TS=TS)
        outs_p.append(st_p)
        outs_s.append(st_s)
    stack = lambda outs, i: jnp.stack([o[i] for o in outs])
    y_p = xp.reshape(B, S, D)
    y_s = jnp.transpose(xs.reshape(TS, DB, D), (1, 0, 2))
    k_all, v_all, ik_all = kv_state
    heads = lambda a: jnp.transpose(a.reshape(depth, B, N_HEADS, HEAD_DIM, S), (0, 1, 4, 2, 3))
    return (y_p, y_s,
            heads(k_all), heads(v_all), jnp.transpose(ik_all, (0, 1, 3, 2)), stack(outs_p, 0), stack(outs_p, 1),
            stack(outs_s, 0), stack(outs_s, 1), stack(outs_s, 2), stack(outs_s, 3), stack(outs_s, 4), stack(outs_s, 5))
```

```python
import functools

import jax
import jax.numpy as jnp
import numpy as np
from jax import lax
from jax.experimental import pallas as pl
from jax.experimental.pallas import tpu as pltpu

MXU_DTYPE = jnp.bfloat16
F32 = jnp.float32
I32 = jnp.int32

N_HEADS = 8
HEAD_DIM = 64
A_WIDTH = N_HEADS * HEAD_DIM
IDX_HEADS = 8
IDX_DIM = 64
IDX_ROPE = 32
TOPK_MAX = 256
SGU_GROUPS = 4
SGU_WIDTH = 256
SGU_CHUNK = 128
POOL_GROUPS = 4
POOL_WIDTH = 256
POOL_WINDOWS = (2, 4, 8, 16)
POOL_STATE = 15
CONV_W = 3
ROPE_THETA = 10000.0
EPS = 1e-6

LANES = 128
SUBLANES = 8
INT_MIN = -2 ** 31
NEG_BIG = -1e30
Q_SCALE = HEAD_DIM ** -0.5 * float(np.log2(np.e))
PROJ_COLS = 3072
VMEM_LIMIT = 56 * 1024 * 1024

_NT = (((1,), (1,)), ((), ()))


def _cparams(n_axes):
    return pltpu.CompilerParams(dimension_semantics=("arbitrary",) * n_axes, vmem_limit_bytes=VMEM_LIMIT)


def _const_spec(shape):
    nd = len(shape)
    return pl.BlockSpec(shape, lambda *_: (0,) * nd)


def _rope128(x, cos, sin, half):
    lane = lax.broadcasted_iota(I32, x.shape, 1)
    lo = (lane & (HEAD_DIM - 1)) < half
    rot = jnp.where(lo, pltpu.roll(x, LANES - half, 1), pltpu.roll(x, half, 1))
    return x * cos + rot * sin


def _head_rms(x, seg, g):
    ss = jnp.dot((x * x).astype(seg.dtype), seg, preferred_element_type=F32)
    return x * lax.rsqrt(ss * (1.0 / HEAD_DIM) + EPS) * g


def _rms(x, g):
    return x * lax.rsqrt(jnp.mean(x * x, axis=-1, keepdims=True) + EPS) * g


def _proj_kernel(x_ref, g1_ref, w_ref, qg_ref, kg_ref, sgg_ref, tab_ref, seg_ref, *rest, stacked, n_alias):
    rest = rest[n_alias:]
    if stacked:
        qb_ref, k_ref, kb_ref, v_ref, vt_ref, iqb_ref, ikw_ref, ik_ref, ikd_ref, u_ref, vv_ref, xc_ref = rest
    else:
        qb_ref, k_ref, kb_ref, v_ref, vt_ref, iqb_ref, ikw_ref, ikd_ref, u_ref, vv_ref, xc_ref = rest
    mxu = w_ref.dtype
    xn = _rms(x_ref[...], g1_ref[...]).astype(mxu)

    def mm(c0, c1):
        return jnp.dot(xn, w_ref[:, c0:c1], preferred_element_type=F32)

    cqk, sqk = tab_ref[:, 0:128], tab_ref[:, 128:256]
    ciq, siq = tab_ref[:, 256:384], tab_ref[:, 384:512]
    cikw, sikw = tab_ref[:, 512:640], tab_ref[:, 640:768]
    seg = seg_ref[...]
    half_qk = HEAD_DIM // 2
    half_idx = IDX_ROPE // 2

    q = _head_rms(mm(0, 512), seg, qg_ref[...])
    for c in range(4):
        sl = slice(LANES * c, LANES * (c + 1))
        qb_ref[:, sl] = (_rope128(q[:, sl], cqk, sqk, half_qk) * Q_SCALE).astype(mxu)
    k = _head_rms(mm(512, 1024), seg, kg_ref[...])
    for c in range(4):
        sl = slice(LANES * c, LANES * (c + 1))
        kr = _rope128(k[:, sl], cqk, sqk, half_qk)
        if stacked:
            k_ref[sl, :] = jnp.transpose(kr)
        else:
            k_ref[:, sl] = kr
        kb_ref[:, sl] = kr.astype(mxu)
    v = mm(1024, 1536)
    vt = jnp.transpose(v)
    v_ref[...] = vt if stacked else v
    vt_ref[...] = vt.astype(mxu)
    iq = mm(1536, 2048)
    for c in range(4):
        sl = slice(LANES * c, LANES * (c + 1))
        iqb_ref[:, sl] = (_rope128(iq[:, sl], ciq, siq, half_idx) * (IDX_DIM ** -0.5)).astype(mxu)
    ikw = _rope128(mm(2048, 2176), cikw, sikw, half_idx)
    ikw_ref[...] = ikw
    if stacked:
        ik_ref[...] = jnp.transpose(ikw)[:IDX_DIM, :]
    ikd_ref[...] = _rope128(mm(2176, 2304), ciq, siq, half_idx).astype(mxu)
    sg = jax.nn.gelu(mm(2304, 2816))
    u_ref[...] = sg[:, :SGU_WIDTH]
    vv_ref[...] = _rms(sg[:, SGU_WIDTH:], sgg_ref[...])
    xc_ref[...] = mm(2816, 3072)


def _proj(x, g1, w, qg, kg, sgg, tab, seg, *, tm, tab_tiles, stack=None):
    T, D = x.shape
    mxu = w.dtype
    n = T // tm
    row = lambda width: pl.BlockSpec((tm, width), lambda i: (i, 0))
    natural = lambda width, dt: (jax.ShapeDtypeStruct((T, width), dt), row(width))
    outs = dict(
        qb=natural(A_WIDTH, mxu),
        k=natural(A_WIDTH, F32), kb=natural(A_WIDTH, mxu), v=natural(A_WIDTH, F32),
        vt=(jax.ShapeDtypeStruct((A_WIDTH, T), mxu), pl.BlockSpec((A_WIDTH, tm), lambda i: (0, i))),
        iqb=natural(IDX_HEADS * IDX_DIM, mxu),
        ikw=natural(LANES, F32),
        ik=None,
        ikd=natural(LANES, mxu),
        u=natural(SGU_WIDTH, F32), vv=natural(SGU_WIDTH, F32), xc=natural(POOL_WIDTH, F32),
    )
    in_specs = [
        row(D), _const_spec(g1.shape), _const_spec(w.shape), _const_spec(qg.shape), _const_spec(kg.shape),
        _const_spec(sgg.shape),
        pl.BlockSpec((tm, tab.shape[1]), lambda i: (i % tab_tiles, 0)),
        _const_spec(seg.shape),
    ]
    args = [x, g1, w, qg, kg, sgg, tab, seg]
    aliases = {}
    if stack is not None:
        layer, depth, B, S, state = stack
        tps = S // tm
        state_out = lambda feat: (jax.ShapeDtypeStruct((depth, B, feat, S), F32),
                                  pl.BlockSpec((None, None, feat, tm), lambda i: (layer, i // tps, 0, i % tps)))
        outs.update(k=state_out(A_WIDTH), v=state_out(A_WIDTH), ik=state_out(IDX_DIM))
        if state is not None:
            names = list(k for k, o in outs.items() if o is not None)
            for buf, name in zip(state, ("k", "v", "ik")):
                aliases[len(args)] = names.index(name)
                args.append(buf)
                in_specs.append(pl.BlockSpec(memory_space=pl.ANY))
    outs = {k: o for k, o in outs.items() if o is not None}
    kern = functools.partial(_proj_kernel, stacked=stack is not None, n_alias=len(aliases))
    res = pl.pallas_call(
        kern, grid=(n,), in_specs=in_specs, out_specs=[o[1] for o in outs.values()],
        out_shape=[o[0] for o in outs.values()], input_output_aliases=aliases,
        compiler_params=_cparams(1), name="proj",
    )(*args)
    return dict(zip(outs.keys(), res))


def _sortable_key(s):
    b = lax.bitcast_convert_type(s, I32)
    key = jnp.where(b < 0, b ^ jnp.int32(0x7FFFFFFF), b)
    return jnp.where(key == -1, 0, key)


def _fold_lanes(m):
    out = m[:, 0:LANES]
    for c in range(1, m.shape[1] // LANES):
        out = out + m[:, c * LANES:(c + 1) * LANES]
    return out


def _fold_sublanes(m):
    out = m[0:SUBLANES]
    for r in range(1, m.shape[0] // SUBLANES):
        out = out + m[r * SUBLANES:(r + 1) * SUBLANES]
    return out


I16 = jnp.int16
PACK16 = 2 * SUBLANES


def _fold_pack16(m):
    out = m[0:PACK16]
    for r in range(1, m.shape[0] // PACK16):
        out = out + m[r * PACK16:(r + 1) * PACK16]
    return out


def _tied_below_counter(kint_ref, iota_ref, eq_ref, t, n_tiles, tile, rows):
    iota_ref[0:tile, :] = lax.broadcasted_iota(I32, (tile, rows), 0).astype(I16)

    def build(j, carry):
        off = pl.multiple_of(j * tile, tile)
        eq_ref[pl.ds(off, tile), :] = jnp.where(kint_ref[pl.ds(off, tile), :] == t, 1, 0).astype(I16)
        return carry

    lax.fori_loop(0, n_tiles, build, 0)

    def counter(cand):
        def body(j, c):
            off = pl.multiple_of(j * tile, tile)
            local = jnp.clip(cand - off, 0, tile).astype(I16)
            m = jnp.where(iota_ref[0:tile, :] < local, eq_ref[pl.ds(off, tile), :], jnp.int16(0))
            return c + _fold_pack16(m)
        part = lax.fori_loop(0, n_tiles, body, jnp.zeros((PACK16, rows), I16))
        return jnp.sum(part.astype(F32), axis=0, keepdims=True)

    return counter


def _kth_largest_by_halves(khi_ref, klo_ref, low_ref, n_tiles, tile, rows, kf):
    def count_ge(ref, cand):
        c16 = cand.astype(I16)

        def body(j, c):
            off = pl.multiple_of(j * tile, tile)
            m = jnp.where(ref[pl.ds(off, tile), :] >= c16, jnp.int16(1), jnp.int16(0))
            return c + _fold_pack16(m)
        part = lax.fori_loop(0, n_tiles, body, jnp.zeros((PACK16, rows), I16))
        return jnp.sum(part.astype(F32), axis=0, keepdims=True)

    def bisect(ref):
        c0 = count_ge(ref, jnp.zeros((1, rows), I32))
        ok = c0 >= kf
        state = (jnp.where(ok, jnp.int32(0), jnp.int32(-32768)), jnp.where(ok, c0, 0.0))

        def bit_body(b, state):
            t, c = state
            cand = t + jnp.left_shift(jnp.int32(1), 14 - b)
            cnt = count_ge(ref, cand)
            ok = cnt >= kf
            return jnp.where(ok, cand, t), jnp.where(ok, cnt, c)

        return lax.fori_loop(0, 15, bit_body, state)

    hi, cnt_hi = bisect(khi_ref)
    hi16 = hi.astype(I16)

    def build(j, carry):
        off = pl.multiple_of(j * tile, tile)
        kh = khi_ref[pl.ds(off, tile), :]
        low_ref[pl.ds(off, tile), :] = jnp.where(
            kh > hi16, jnp.int16(32767), jnp.where(kh == hi16, klo_ref[pl.ds(off, tile), :], jnp.int16(-32768)))
        return carry

    lax.fori_loop(0, n_tiles, build, 0)
    lo, cnt = bisect(low_ref)
    return jnp.left_shift(hi, 16) + (lo + 32768), jnp.where(lo == -32768, cnt_hi, cnt)


def _topk_threshold(kint_ref, thr_ref, cut_ref, n_tiles, tile, rows, topk, n_cols, keys_axis=1, halves=None):
    stat_shape = (rows, 1) if keys_axis == 1 else (1, rows)

    def count(pred):
        def body(j, c):
            off = pl.multiple_of(j * tile, tile)
            if keys_axis == 1:
                key = kint_ref[:, pl.ds(off, tile)]
                idx = off + lax.broadcasted_iota(I32, (rows, tile), 1)
                return c + _fold_lanes(pred(key, idx).astype(F32))
            key = kint_ref[pl.ds(off, tile), :]
            idx = off + lax.broadcasted_iota(I32, (tile, rows), 0)
            return c + _fold_sublanes(pred(key, idx).astype(F32))
        init = jnp.zeros((rows, LANES) if keys_axis == 1 else (SUBLANES, rows), F32)
        part = lax.fori_loop(0, n_tiles, body, init)
        return jnp.sum(part, axis=keys_axis, keepdims=True)

    kf = float(topk)
    if halves is None:
        c0 = count(lambda key, col: key >= 0)
        t = jnp.where(c0 >= kf, jnp.int32(0), jnp.int32(INT_MIN))

        def bit_body(b, t):
            cand = t + jnp.left_shift(jnp.int32(1), 30 - b)
            cnt = count(lambda key, col: key >= cand)
            return jnp.where(cnt >= kf, cand, t)

        t = lax.fori_loop(0, 31, bit_body, t)
        t = jnp.maximum(t, jnp.int32(INT_MIN + 1))
        cge = count(lambda key, col: key >= t)
    else:
        t, cge = _kth_largest_by_halves(*halves, n_tiles, tile, rows, kf)
        t = jnp.maximum(t, jnp.int32(INT_MIN + 1))
    thr_ref[...] = t
    cut_ref[...] = jnp.full(stat_shape, n_cols, I32)

    @pl.when(jnp.max(cge) > kf)
    def _():
        need = kf - count(lambda key, col: key > t)
        nbits = max(1, (n_cols - 1).bit_length())

        if halves is None:
            count_tied_below = lambda cand: count(lambda key, col: (key == t) & (col < cand))
        else:
            count_tied_below = _tied_below_counter(kint_ref, halves[1], halves[2], t, n_tiles, tile, rows)

        def idx_body(b, p):
            cand = p + jnp.left_shift(jnp.int32(1), nbits - 1 - b)
            return jnp.where(count_tied_below(cand) < need, cand, p)

        p = lax.fori_loop(0, nbits, idx_body, jnp.zeros(stat_shape, I32))
        cut_ref[...] = jnp.where(cge > kf, p, jnp.int32(n_cols))


def _selected(key, col, t, cut):
    return (key > t) | ((key == t) & (col <= cut))


KEY_TILE = 128


def _dsa_prompt_kernel(qb_ref, iqb_ref, ikw_ref, kb_ref, vt_ref, ikd_ref, o_ref,
                       kint_ref, khi_ref, klo_ref, low_ref,
                       qm_ref, iqm_ref, w_ref, s0_ref, s1_ref, p0_ref, p1_ref, a0_ref, a1_ref,
                       acc_ref, m_ref, l_ref, thr_ref, cut_ref, *, topk, qb, seq):
    i = pl.program_id(1)
    kt = KEY_TILE
    assert qb == 2 * kt
    n_tiles = 2 * (i + 1)
    last = n_tiles - 1
    s_slots, p_slots, a_slots = (s0_ref, s1_ref), (p0_ref, p1_ref), (a0_ref, a1_ref)
    lane = lax.broadcasted_iota(I32, (qb, LANES), 1)
    lo = lane < HEAD_DIM
    for h in range(N_HEADS):
        sl = slice(LANES * (h // 2), LANES * (h // 2 + 1))
        msk = lo if h % 2 == 0 else jnp.logical_not(lo)
        qm_ref[h] = jnp.where(msk, qb_ref[0, :, sl], jnp.zeros((), qb_ref.dtype))
        iqm_ref[h] = jnp.where(msk, iqb_ref[0, :, sl], jnp.zeros((), iqb_ref.dtype))
    w_ref[...] = jnp.transpose(ikw_ref[0])[IDX_DIM:IDX_DIM + IDX_HEADS, :]
    qpos = i * qb + lax.broadcasted_iota(I32, (1, qb), 1)

    def tile_off(tile):
        return pl.multiple_of(jnp.clip(tile, 0, last) * kt, kt)

    def idx_dots(tile, s_out):
        ik = ikd_ref[0, pl.ds(tile_off(tile), kt), :]
        for h in range(IDX_HEADS):
            s_out[h] = lax.dot_general(ik, iqm_ref[h], _NT, preferred_element_type=F32)

    def combine(tile, s_in):
        sc = jnp.zeros((kt, qb), F32)
        for h in range(IDX_HEADS):
            sc = sc + jnp.maximum(s_in[h], 0.0) * w_ref[h:h + 1, :]
        off = tile_off(tile)
        kpos = off + lax.broadcasted_iota(I32, (kt, qb), 0)
        key = jnp.where(kpos <= qpos, _sortable_key(sc), jnp.int32(INT_MIN))
        kint_ref[pl.ds(off, kt), :] = key
        khi_ref[pl.ds(off, kt), :] = jnp.right_shift(key, 16).astype(I16)
        klo_ref[pl.ds(off, kt), :] = ((key & 0xFFFF) - 32768).astype(I16)

    idx_dots(0, s0_ref)

    def score_pair(jj, carry):
        for step in range(2):
            tile = 2 * jj + step
            combine(tile, s_slots[step])
            idx_dots(tile + 1, s_slots[1 - step])
        return carry

    lax.fori_loop(0, i + 1, score_pair, 0)
    pad = pl.ds(pl.multiple_of((i + 1) * qb, qb), qb)
    kint_ref[pad, :] = jnp.full((qb, qb), INT_MIN, I32)
    khi_ref[pad, :] = jnp.full((qb, qb), -32768, I16)
    klo_ref[pad, :] = jnp.full((qb, qb), -32768, I16)
    _topk_threshold(kint_ref, thr_ref, cut_ref, (i + 2) // 2, 2 * qb, qb, topk, seq, keys_axis=0,
                    halves=(khi_ref, klo_ref, low_ref))
    thr = thr_ref[...]
    cut = cut_ref[...]

    acc_ref[...] = jnp.zeros_like(acc_ref)
    m_ref[...] = jnp.full(m_ref.shape, NEG_BIG, F32)
    l_ref[...] = jnp.zeros_like(l_ref)
    p1_ref[...] = jnp.zeros_like(p1_ref)
    a1_ref[...] = jnp.ones_like(a1_ref)

    def qk_dots(tile, s_out):
        off = tile_off(tile)
        for h in range(N_HEADS):
            kk = kb_ref[0, pl.ds(off, kt), LANES * (h // 2):LANES * (h // 2 + 1)]
            s_out[h] = lax.dot_general(kk, qm_ref[h], _NT, preferred_element_type=F32)

    def softmax(tile, s_in, p_out, a_out):
        off = tile_off(tile)
        live = tile <= last
        kpos = off + lax.broadcasted_iota(I32, (kt, qb), 0)
        sel = _selected(kint_ref[pl.ds(off, kt), :], kpos,
                        jnp.where(live, thr, jnp.int32(2 ** 31 - 1)), jnp.where(live, cut, jnp.int32(-1)))
        bias = jnp.where(sel, 0.0, NEG_BIG)
        for h in range(N_HEADS):
            s = s_in[h] + bias
            m_old = m_ref[h]
            m_new = jnp.maximum(m_old, jnp.max(s, axis=0, keepdims=True))
            alpha = jnp.exp2(m_old - m_new)
            pm = jnp.exp2(s - m_new)
            m_ref[h] = m_new
            p_out[h] = pm.astype(p_out.dtype)
            a_out[h] = alpha

    ones_rows = jnp.ones((PACK16, kt), vt_ref.dtype)

    def pv_dots(tile, p_in, a_in):
        off = tile_off(tile)
        for h in range(N_HEADS):
            dr = slice(HEAD_DIM * h, HEAD_DIM * (h + 1))
            lhs = jnp.concatenate([vt_ref[dr, pl.ds(off, kt)], ones_rows], axis=0)
            pv = jnp.dot(lhs, p_in[h], preferred_element_type=F32)
            acc_ref[dr, :] = a_in[h] * acc_ref[dr, :] + pv[:HEAD_DIM]
            l_ref[h] = a_in[h] * l_ref[h] + pv[HEAD_DIM:HEAD_DIM + 1]

    qk_dots(0, s0_ref)

    def att_pair(jj, carry):
        for step in range(2):
            tile = 2 * jj + step
            softmax(tile, s_slots[step], p_slots[step], a_slots[step])
            qk_dots(tile + 1, s_slots[1 - step])
            pv_dots(tile - 1, p_slots[1 - step], a_slots[1 - step])
        return carry

    lax.fori_loop(0, i + 2, att_pair, 0)
    for h in range(N_HEADS):
        dr = slice(HEAD_DIM * h, HEAD_DIM * (h + 1))
        acc_ref[dr, :] = acc_ref[dr, :] / l_ref[h]
    o_ref[0] = jnp.transpose(acc_ref[...]).astype(o_ref.dtype)


def _dsa_prompt(qb_, iqb, ikw, kb, vt, ikd, *, qb, topk):
    B, S, _ = qb_.shape
    mxu = qb_.dtype
    blk = lambda width: pl.BlockSpec((1, qb, width), lambda b, i: (b, i, 0))
    full = lambda width: pl.BlockSpec((1, S, width), lambda b, i: (b, 0, 0))
    kern = functools.partial(_dsa_prompt_kernel, topk=topk, qb=qb, seq=S)
    return pl.pallas_call(
        kern, grid=(B, S // qb),
        in_specs=[blk(A_WIDTH), blk(A_WIDTH), blk(LANES), full(A_WIDTH),
                  pl.BlockSpec((A_WIDTH, S), lambda b, i: (0, b)), full(LANES)],
        out_specs=blk(A_WIDTH),
        out_shape=jax.ShapeDtypeStruct((B, S, A_WIDTH), mxu),
        scratch_shapes=[
            pltpu.VMEM((S + qb, qb), I32),
            pltpu.VMEM((S + qb, qb), I16), pltpu.VMEM((S + qb, qb), I16), pltpu.VMEM((S + qb, qb), I16),
            pltpu.VMEM((N_HEADS, qb, LANES), mxu),
            pltpu.VMEM((N_HEADS, qb, LANES), mxu),
            pltpu.VMEM((IDX_HEADS, qb), F32),
            pltpu.VMEM((N_HEADS, KEY_TILE, qb), F32),
            pltpu.VMEM((N_HEADS, KEY_TILE, qb), F32),
            pltpu.VMEM((N_HEADS, KEY_TILE, qb), mxu),
            pltpu.VMEM((N_HEADS, KEY_TILE, qb), mxu),
            pltpu.VMEM((N_HEADS, 1, qb), F32),
            pltpu.VMEM((N_HEADS, 1, qb), F32),
            pltpu.VMEM((A_WIDTH, qb), F32),
            pltpu.VMEM((N_HEADS, 1, qb), F32),
            pltpu.VMEM((N_HEADS, 1, qb), F32),
            pltpu.VMEM((1, qb), I32),
            pltpu.VMEM((1, qb), I32),
        ],
        compiler_params=_cparams(2), name="dsa_prompt",
    )(qb_, iqb, ikw, kb, vt, ikd)


PAGES_PER_STEP = 32


def _head_sum(s):
    return jnp.concatenate(
        [jnp.sum(s[IDX_HEADS * t:IDX_HEADS * (t + 1)], axis=0, keepdims=True) for t in range(s.shape[0] // IDX_HEADS)],
        axis=0)


def _sample_scores_kernel(pt_ref, iq_ref, w_ref, ikn_ref, *rest, n_steps):
    pages = rest[:PAGES_PER_STEP]
    sc_ref, scn_ref = rest[PAGES_PER_STEP:]
    c = pl.program_id(1)
    iq = iq_ref[0]
    w = w_ref[0]

    def scores(ik):
        s = jnp.dot(iq, ik.astype(iq.dtype), preferred_element_type=F32)
        return _head_sum(jnp.maximum(s, 0.0) * w)

    sc_ref[0] = scores(jnp.concatenate([r[...].astype(iq.dtype) for r in pages], axis=1))

    @pl.when(c == n_steps - 1)
    def _():
        s = scores(ikn_ref[0])
        col = lax.broadcasted_iota(I32, s.shape, 1)
        row = lax.broadcasted_iota(I32, s.shape, 0)
        scn_ref[0] = jnp.where(col <= row, s, -jnp.inf)


def _page_specs(layer, rows, page):
    def make(r):
        def imap(b, c, pt):
            return (layer, pt[b, c * PAGES_PER_STEP + r], 0, 0)
        return pl.BlockSpec((None, None, rows, page), imap)
    return [make(r) for r in range(PAGES_PER_STEP)]


def _sample_scores(page_table, iq_rows, w_rows, ik_new, cache_ik, *, layer):
    DB, n_pages = page_table.shape
    page = cache_ik.shape[3]
    n_steps = n_pages // PAGES_PER_STEP
    rows = iq_rows.shape[1]
    nt = rows // IDX_HEADS
    step_cols = PAGES_PER_STEP * page
    grid_spec = pltpu.PrefetchScalarGridSpec(
        num_scalar_prefetch=1, grid=(DB, n_steps),
        in_specs=[
            pl.BlockSpec((1, rows, IDX_DIM), lambda b, c, pt: (b, 0, 0)),
            pl.BlockSpec((1, rows, 1), lambda b, c, pt: (b, 0, 0)),
            pl.BlockSpec((1, IDX_DIM, page), lambda b, c, pt: (b, 0, 0)),
        ] + _page_specs(layer, IDX_DIM, page),
        out_specs=[pl.BlockSpec((1, nt, step_cols), lambda b, c, pt: (b, 0, c)),
                   pl.BlockSpec((1, nt, page), lambda b, c, pt: (b, 0, 0))],
    )
    kern = functools.partial(_sample_scores_kernel, n_steps=n_steps)
    return pl.pallas_call(
        kern, grid_spec=grid_spec,
        out_shape=[jax.ShapeDtypeStruct((DB, nt, n_pages * page), F32), jax.ShapeDtypeStruct((DB, nt, page), F32)],
        compiler_params=_cparams(2), name="sample_scores",
    )(page_table, iq_rows, w_rows, ik_new, *([cache_ik] * PAGES_PER_STEP))


def _score_key(sc):
    return jnp.where(sc == -jnp.inf, jnp.int32(INT_MIN), _sortable_key(sc))


def _sample_threshold_kernel(sc_ref, scn_ref, thr_ref, cut_ref, kint_ref, t_ref, c_ref, *, topk, tile):
    rows, past = sc_ref.shape
    page = scn_ref.shape[1]
    cols = kint_ref.shape[1]
    kint_ref[:, 0:past] = _score_key(sc_ref[...])
    kint_ref[:, past:past + page] = _score_key(scn_ref[...])
    kint_ref[:, past + page:] = jnp.full((rows, cols - past - page), INT_MIN, I32)
    _topk_threshold(kint_ref, t_ref, c_ref, cols // tile, tile, rows, topk, cols)
    thr_ref[...] = jnp.broadcast_to(t_ref[...], thr_ref.shape)
    cut_ref[...] = jnp.broadcast_to(c_ref[...], cut_ref.shape)


def _sample_threshold(sc, sc_new, *, topk):
    rows, past = sc.shape
    page = sc_new.shape[1]
    tile = 2 * LANES
    cols = pl.cdiv(past + page, tile) * tile
    kern = functools.partial(_sample_threshold_kernel, topk=topk, tile=tile)
    return pl.pallas_call(
        kern, grid=(1,),
        in_specs=[_const_spec(sc.shape), _const_spec(sc_new.shape)],
        out_specs=[_const_spec((rows, LANES)), _const_spec((rows, LANES))],
        out_shape=[jax.ShapeDtypeStruct((rows, LANES), I32)] * 2,
        scratch_shapes=[pltpu.VMEM((rows, cols), I32), pltpu.VMEM((rows, 1), I32), pltpu.VMEM((rows, 1), I32)],
        compiler_params=_cparams(1), name="sample_threshold",
    )(sc, sc_new)


def _sample_attn_kernel(pt_ref, q_ref, sc_ref, scn_ref, thr_ref, cut_ref, kn_ref, vn_ref, hm_ref, *rest, n_steps):
    kp = rest[:PAGES_PER_STEP]
    vp = rest[PAGES_PER_STEP:2 * PAGES_PER_STEP]
    o_ref = rest[2 * PAGES_PER_STEP]
    m_ref, l_ref, acc_ref = rest[2 * PAGES_PER_STEP + 1:]
    c = pl.program_id(1)
    q = q_ref[0]
    mxu = q.dtype
    nt = sc_ref.shape[1]
    step_cols = sc_ref.shape[2]

    @pl.when(c == 0)
    def _():
        m_ref[...] = jnp.full(m_ref.shape, NEG_BIG, F32)
        l_ref[...] = jnp.zeros_like(l_ref)
        acc_ref[...] = jnp.zeros_like(acc_ref)

    def rows_th(x):
        return jnp.concatenate([jnp.broadcast_to(x[t:t + 1], (N_HEADS, x.shape[1])) for t in range(nt)], axis=0)

    def update(keys, vals, sc, col0):
        key = _score_key(sc)
        col = col0 + lax.broadcasted_iota(I32, key.shape, 1)
        sel = rows_th(_selected(key, col, thr_ref[0, :, 0:1], cut_ref[0, :, 0:1]).astype(F32)) > 0.5
        kcat = jnp.concatenate([kk.astype(mxu) for kk in keys], axis=1)
        s = jnp.dot(q, kcat, preferred_element_type=F32)
        s = jnp.where(sel, s, NEG_BIG)
        m_old = m_ref[...]
        m_new = jnp.maximum(m_old, jnp.max(s, axis=1, keepdims=True))
        alpha = jnp.exp2(m_old - m_new)
        pm = jnp.exp2(s - m_new)
        l_ref[...] = alpha * l_ref[...] + jnp.sum(pm, axis=1, keepdims=True)
        m_ref[...] = m_new
        vcat = jnp.concatenate([vv.astype(mxu) for vv in vals], axis=1)
        pv = lax.dot_general(pm.astype(mxu), vcat, _NT, preferred_element_type=F32)
        acc_ref[...] = alpha * acc_ref[...] + pv

    update([r[...] for r in kp], [r[...] for r in vp], sc_ref[0], c * step_cols)

    @pl.when(c == n_steps - 1)
    def _():
        update([kn_ref[0]], [vn_ref[0]], scn_ref[0], n_steps * step_cols)
        out = acc_ref[...] / l_ref[...] * hm_ref[...]
        o_ref[0] = _head_sum(out).astype(o_ref.dtype)


def _sample_attn(page_table, q_rows, sc, sc_new, thr, cut, k_new, v_new, head_mask, cache_k, cache_v, *, layer):
    DB, n_pages = page_table.shape
    page = cache_k.shape[3]
    n_steps = n_pages // PAGES_PER_STEP
    rows = q_rows.shape[1]
    nt = rows // N_HEADS
    step_cols = PAGES_PER_STEP * page
    per_b = lambda shape: pl.BlockSpec((1,) + shape, lambda b, c, pt: (b, 0, 0))
    grid_spec = pltpu.PrefetchScalarGridSpec(
        num_scalar_prefetch=1, grid=(DB, n_steps),
        in_specs=[
            per_b((rows, A_WIDTH)),
            pl.BlockSpec((1, nt, step_cols), lambda b, c, pt: (b, 0, c)),
            per_b((nt, page)), per_b((nt, LANES)), per_b((nt, LANES)),
            per_b((A_WIDTH, page)), per_b((A_WIDTH, page)),
            pl.BlockSpec((rows, A_WIDTH), lambda b, c, pt: (0, 0)),
        ] + _page_specs(layer, A_WIDTH, page) * 2,
        out_specs=per_b((nt, A_WIDTH)),
        scratch_shapes=[pltpu.VMEM((rows, 1), F32), pltpu.VMEM((rows, 1), F32), pltpu.VMEM((rows, A_WIDTH), F32)],
    )
    kern = functools.partial(_sample_attn_kernel, n_steps=n_steps)
    return pl.pallas_call(
        kern, grid_spec=grid_spec,
        out_shape=jax.ShapeDtypeStruct((DB, nt, A_WIDTH), q_rows.dtype),
        compiler_params=_cparams(2), name="sample_attn",
    )(page_table, q_rows, sc, sc_new, thr, cut, k_new, v_new, head_mask,
      *([cache_k] * PAGES_PER_STEP), *([cache_v] * PAGES_PER_STEP))


def _shift_rows(e, k):
    return pltpu.roll(e, k, 0)


def _merge_kernel(x_ref, att_ref, u_ref, vv_ref, xc_ref, halo_ref, cnt_ref,
                  g1_ref, wgt_ref, wa_ref, wb_ref, wc_ref, wout_ref, ws_ref, btab_ref, wpool_ref, pscale_ref,
                  o_ref, *, row_stride, tiles_per_seq, zero_first_halo):
    mxu = wgt_ref.dtype
    tm = x_ref.shape[0]
    x = x_ref[...]
    xn = _rms(x, g1_ref[...]).astype(mxu)
    d = x.shape[1]

    def gate(n):
        return jax.nn.sigmoid(jnp.dot(xn, wgt_ref[:, n * d:(n + 1) * d], preferred_element_type=F32))

    merged = gate(0) * jnp.dot(att_ref[...], wa_ref[...], preferred_element_type=F32)

    lane = lax.broadcasted_iota(I32, (SGU_CHUNK, LANES), 1)
    lo = lane < (SGU_WIDTH // SGU_GROUPS)
    sgo = []
    for c in range(tm // SGU_CHUNK):
        rs = slice(c * SGU_CHUNK, (c + 1) * SGU_CHUNK)
        vvb = vv_ref[rs, :].astype(mxu)
        mix = []
        for p in range(SGU_GROUPS // 2):
            pair = vvb[:, LANES * p:LANES * (p + 1)]
            r0 = jnp.dot(ws_ref[2 * p], pair, preferred_element_type=F32)
            r1 = jnp.dot(ws_ref[2 * p + 1], pair, preferred_element_type=F32)
            mix.append(jnp.where(lo, r0, r1))
        sgo.append(u_ref[rs, :] * (jnp.concatenate(mix, axis=1) + btab_ref[...]))
    sgo = jnp.concatenate(sgo, axis=0).astype(mxu)
    merged = merged + gate(1) * jnp.dot(sgo, wb_ref[...], preferred_element_type=F32)

    xc = xc_ref[...]
    halo = halo_ref[...]
    if zero_first_halo:
        first = (pl.program_id(0) % tiles_per_seq) == 0
        halo = jnp.where(first, jnp.zeros_like(halo), halo)
    hp = halo.shape[0]
    e = jnp.concatenate([halo, xc], axis=0)
    s1 = e + _shift_rows(e, row_stride)
    s2 = s1 + _shift_rows(s1, 2 * row_stride)
    s3 = s2 + _shift_rows(s2, 4 * row_stride)
    s4 = s3 + _shift_rows(s3, 8 * row_stride)
    gd = POOL_WIDTH // POOL_GROUPS
    wsum = jnp.concatenate([s[hp:, g * gd:(g + 1) * gd] for g, s in enumerate((s1, s2, s3, s4))], axis=1)
    pooled = (wsum / cnt_ref[...] - xc).astype(mxu)
    po = (jnp.dot(pooled, wpool_ref[...], preferred_element_type=F32) * pscale_ref[...]).astype(mxu)
    merged = merged + gate(2) * jnp.dot(po, wc_ref[...], preferred_element_type=F32)

    o_ref[...] = x + jnp.dot(merged.astype(mxu), wout_ref[...], preferred_element_type=F32)


def _merge(x, att, u, vv, xc, halo, cnt, g1, wgt, wa, wb, wc, wout, ws, btab, wpool, pscale,
           *, tm, row_stride, tiles_per_seq, halo_rows, halo_from_xc, cnt_tiles):
    T, D = x.shape
    n = T // tm
    row = lambda width: pl.BlockSpec((tm, width), lambda i: (i, 0))
    if halo_from_xc:
        per = tm // halo_rows
        halo_spec = pl.BlockSpec((halo_rows, xc.shape[1]), lambda i: (jnp.maximum(i * per - 1, 0), 0))
    else:
        halo_spec = _const_spec(halo.shape)
    consts = [g1, wgt, wa, wb, wc, wout, ws, btab, wpool, pscale]
    kern = functools.partial(_merge_kernel, row_stride=row_stride, tiles_per_seq=tiles_per_seq,
                             zero_first_halo=halo_from_xc)
    return pl.pallas_call(
        kern, grid=(n,),
        in_specs=[row(D), row(att.shape[1]), row(u.shape[1]), row(vv.shape[1]), row(xc.shape[1]), halo_spec,
                  pl.BlockSpec((tm, cnt.shape[1]), lambda i: (i % cnt_tiles, 0))]
        + [_const_spec(c.shape) for c in consts],
        out_specs=row(D), out_shape=jax.ShapeDtypeStruct((T, D), F32),
        compiler_params=_cparams(1), name="merge",
    )(x, att, u, vv, xc, halo, cnt, *consts)


FFN_CHUNK = 2816


def _ffn_kernel(x_ref, g2_ref, wa_ref, wu_ref, cw_ref, cb_ref, wd_ref, halo_ref, o_ref, tail_ref, carry_ref,
                *, row_stride, tiles_per_seq, use_carry):
    mxu = wa_ref.dtype
    x = x_ref[...]
    tm = x.shape[0]
    xn = _rms(x, g2_ref[...]).astype(mxu)
    dff = wa_ref.shape[1]
    tail = tail_ref.shape[1]
    acc = jnp.zeros(x.shape, F32)
    if use_carry:
        first = (pl.program_id(0) % tiles_per_seq) == 0
    for c0 in range(0, dff, FFN_CHUNK):
        cs = slice(c0, c0 + FFN_CHUNK)
        a = jnp.dot(xn, wa_ref[:, cs], preferred_element_type=F32)
        up = jnp.dot(xn, wu_ref[:, cs], preferred_element_type=F32)
        if use_carry:
            halo = jnp.where(first, 0.0, carry_ref[:, cs])
            carry_ref[:, cs] = a[tm - carry_ref.shape[0]:, :]
        else:
            halo = halo_ref[:, cs]
        hp = halo.shape[0]
        e = jnp.concatenate([halo, a], axis=0)
        p1 = _shift_rows(e, row_stride)[hp:]
        p2 = _shift_rows(e, 2 * row_stride)[hp:]
        conv = p2 * cw_ref[0:1, cs] + p1 * cw_ref[1:2, cs] + a * cw_ref[2:3, cs] + cb_ref[:, cs]
        act = (jax.nn.silu(conv) * up).astype(mxu)
        acc = acc + jnp.dot(act, wd_ref[cs, :], preferred_element_type=F32)
        tail_ref[0, :, cs] = a[tm - tail:, :]
    o_ref[...] = x + acc


def _ffn(x, g2, wa, wu, cw, cb, wd, halo, *, tm, row_stride, tiles_per_seq, use_carry, tail):
    T, D = x.shape
    n = T // tm
    dff = wa.shape[1]
    row = pl.BlockSpec((tm, D), lambda i: (i, 0))
    consts = [g2, wa, wu, cw, cb, wd, halo]
    kern = functools.partial(_ffn_kernel, row_stride=row_stride, tiles_per_seq=tiles_per_seq, use_carry=use_carry)
    return pl.pallas_call(
        kern, grid=(n,),
        in_specs=[row] + [_const_spec(c.shape) for c in consts],
        out_specs=[row, pl.BlockSpec((1, tail, dff), lambda i: (i, 0, 0))],
        out_shape=[jax.ShapeDtypeStruct((T, D), F32), jax.ShapeDtypeStruct((n, tail, dff), F32)],
        scratch_shapes=[pltpu.VMEM((SUBLANES, dff), F32)],
        compiler_params=_cparams(1), name="ffn",
    )(x, *consts)


def _pack_w_in(w_in):
    d = w_in.shape[0]
    o = 0
    q, k, v = (w_in[:, o + i * A_WIDTH:o + (i + 1) * A_WIDTH] for i in range(3))
    o += 3 * A_WIDTH
    iq = w_in[:, o:o + IDX_HEADS * IDX_DIM]
    o += IDX_HEADS * IDX_DIM
    ik = w_in[:, o:o + IDX_DIM]
    o += IDX_DIM
    iw = w_in[:, o:o + IDX_HEADS]
    o += IDX_HEADS
    sg = w_in[:, o:o + 2 * SGU_WIDTH]
    o += 2 * SGU_WIDTH
    xc = w_in[:, o:o + POOL_WIDTH]
    o += POOL_WIDTH
    gt = w_in[:, o:]
    pad = jnp.zeros((d, LANES - IDX_DIM - IDX_HEADS), w_in.dtype)
    packed = jnp.concatenate([q, k, v, iq, ik, iw, pad, ik, ik, sg, xc], axis=1)
    assert packed.shape[1] == PROJ_COLS
    return packed, gt


def _rope_tables(pos):
    posf = np.asarray(pos, np.float64)[:, None]

    def cs(half):
        inv = ROPE_THETA ** (-np.arange(half, dtype=np.float64) / half)
        ang = posf * inv[None, :]
        return np.cos(ang), np.sin(ang)

    t = posf.shape[0]
    c32, s32 = cs(HEAD_DIM // 2)
    cqk = np.tile(np.concatenate([c32, c32], axis=1), (1, 2))
    sqk = np.tile(np.concatenate([-s32, s32], axis=1), (1, 2))
    c16, s16 = cs(IDX_ROPE // 2)
    rest = IDX_DIM - IDX_ROPE
    ci = np.concatenate([c16, c16, np.ones((t, rest))], axis=1)
    si = np.concatenate([-s16, s16, np.zeros((t, rest))], axis=1)
    ciq, siq = np.tile(ci, (1, 2)), np.tile(si, (1, 2))
    wpad = LANES - IDX_DIM - IDX_HEADS
    cikw = np.concatenate([ci, np.full((t, IDX_HEADS), IDX_HEADS ** -0.5), np.zeros((t, wpad))], axis=1)
    sikw = np.concatenate([si, np.zeros((t, LANES - IDX_DIM))], axis=1)
    return jnp.asarray(np.concatenate([cqk, sqk, ciq, siq, cikw, sikw], axis=1).astype(np.float32))


def _block_diag(blocks):
    n = len(blocks)
    r, c = blocks[0].shape
    out = jnp.zeros((n * r, n * c), blocks[0].dtype)
    for i, b in enumerate(blocks):
        out = out.at[i * r:(i + 1) * r, i * c:(i + 1) * c].set(b)
    return out


def _pool_counts(pos):
    gd = POOL_WIDTH // POOL_GROUPS
    pos = np.asarray(pos)
    cols = [np.broadcast_to(np.minimum(pos + 1, w).astype(np.float32)[:, None], (pos.shape[0], gd))
            for w in POOL_WINDOWS]
    return jnp.asarray(np.concatenate(cols, axis=1))


def _head_segments(dtype):
    return jnp.asarray(np.kron(np.eye(N_HEADS, dtype=np.float32), np.ones((HEAD_DIM, HEAD_DIM), np.float32)), dtype)


def _layer_weights(l, norm1_g, w_in, q_norm_g, k_norm_g, sgu_w, sgu_b, sgu_norm_g, pool_w, pool_scale,
                   w_br_a, w_br_b, w_br_c, w_out, norm2_g, w_ff_in, ff_conv_w, ff_conv_b, w_ff_down):
    mxu = MXU_DTYPE
    packed, gt = _pack_w_in(w_in[l])
    dff = w_ff_in.shape[2] // 2
    tril = jnp.tril(jnp.ones((SGU_CHUNK, SGU_CHUNK), bool))
    return dict(
        g1=norm1_g[l][None, :], w_proj=packed.astype(mxu), w_gt=gt.astype(mxu),
        qg=jnp.tile(q_norm_g[l], N_HEADS)[None, :], kg=jnp.tile(k_norm_g[l], N_HEADS)[None, :],
        sgg=sgu_norm_g[l][None, :],
        ws=jnp.where(tril[None], sgu_w[l], 0.0),
        sgu_b=sgu_b[l],
        wpool=_block_diag([pool_w[l, g] for g in range(POOL_GROUPS)]).astype(mxu),
        pscale=pool_scale[l][None, :],
        wa=w_br_a[l].astype(mxu), wb=w_br_b[l].astype(mxu), wc=w_br_c[l].astype(mxu), wout=w_out[l].astype(mxu),
        g2=norm2_g[l][None, :],
        w_ffa=w_ff_in[l, :, :dff].astype(mxu), w_ffu=w_ff_in[l, :, dff:].astype(mxu),
        cw=jnp.pad(ff_conv_w[l], ((0, SUBLANES - CONV_W), (0, 0))), cb=ff_conv_b[l][None, :],
        w_ffd=w_ff_down[l].astype(mxu),
    )


def _sgu_bias_table(sgu_b, t_of_row):
    gd = SGU_WIDTH // SGU_GROUPS
    return jnp.repeat(jnp.transpose(sgu_b)[t_of_row], gd, axis=1)


def _prompt_layer(x, lw, tabs, cnt, l, depth, state, *, B, S):
    mxu = MXU_DTYPE
    T = B * S
    tm_proj, tm_merge, tm_ffn, qb = 512, 512, 512, 256
    pr = _proj(x, lw["g1"], lw["w_proj"], lw["qg"], lw["kg"], lw["sgg"], tabs, _head_segments(mxu),
               tm=tm_proj, tab_tiles=S // tm_proj, stack=(l, depth, B, S, state))
    u, vv, xc = pr["u"], pr["vv"], pr["xc"]
    r3 = lambda a: a.reshape(B, S, a.shape[1])
    topk = min(TOPK_MAX, S // 4)
    att = _dsa_prompt(r3(pr["qb"]), r3(pr["iqb"]), r3(pr["ikw"]), r3(pr["kb"]), pr["vt"], r3(pr["ikd"]),
                      qb=qb, topk=topk).reshape(T, A_WIDTH)
    btab = _sgu_bias_table(lw["sgu_b"], jnp.arange(SGU_CHUNK))
    x1 = _merge(x, att, u, vv, xc, xc, cnt, lw["g1"], lw["w_gt"], lw["wa"], lw["wb"], lw["wc"], lw["wout"],
                lw["ws"].astype(mxu), btab, lw["wpool"], lw["pscale"],
                tm=tm_merge, row_stride=1, tiles_per_seq=S // tm_merge, halo_rows=16, halo_from_xc=True,
                cnt_tiles=S // tm_merge)
    dff = lw["w_ffa"].shape[1]
    x2, tails = _ffn(x1, lw["g2"], lw["w_ffa"], lw["w_ffu"], lw["cw"], lw["cb"], lw["w_ffd"],
                     jnp.zeros((SUBLANES, dff), F32),
                     tm=tm_ffn, row_stride=1, tiles_per_seq=S // tm_ffn, use_carry=True, tail=SUBLANES)
    n_t = S // tm_ffn
    ff_state = tails.reshape(B, n_t, SUBLANES, dff)[:, -1, SUBLANES - (CONV_W - 1):, :]
    pool_state = xc.reshape(B, S, POOL_WIDTH)[:, S - POOL_STATE:, :]
    return x2, (pr["k"], pr["v"], pr["ik"]), (pool_state, ff_state)


def _sample_layer(x, lw, tabs, cnt, l, cache_k, cache_v, cache_ik, state_pool, state_ffn, page_table, *, DB, TS):
    mxu = MXU_DTYPE
    T = TS * DB
    page = cache_ik.shape[3]
    past = page_table.shape[1] * page
    pr = _proj(x, lw["g1"], lw["w_proj"], lw["qg"], lw["kg"], lw["sgg"], tabs, _head_segments(mxu),
               tm=T, tab_tiles=1)
    qb_, k, kb, v, vt, iqb, ikw, ikd, u, vv, xc = (
        pr[n] for n in ("qb", "k", "kb", "v", "vt", "iqb", "ikw", "ikd", "u", "vv", "xc"))
    bm = lambda a: jnp.transpose(a.reshape(TS, DB, a.shape[1]), (1, 0, 2))

    iq_rows = bm(iqb).reshape(DB, TS * IDX_HEADS, IDX_DIM)
    w_rows = bm(ikw)[:, :, IDX_DIM:IDX_DIM + IDX_HEADS].reshape(DB, TS * IDX_HEADS, 1)
    new_t = lambda a: jnp.pad(jnp.transpose(a, (0, 2, 1)), ((0, 0), (0, 0), (0, page - TS)))
    ik_new = new_t(bm(ikd)[:, :, :IDX_DIM])
    sc, sc_new = _sample_scores(page_table, iq_rows, w_rows, ik_new, cache_ik, layer=l)
    topk = min(TOPK_MAX, (past + TS) // 4)
    thr, cut = _sample_threshold(sc.reshape(DB * TS, past), sc_new.reshape(DB * TS, page), topk=topk)
    hm = jnp.asarray(np.repeat(np.eye(N_HEADS, dtype=np.float32), HEAD_DIM, axis=1))
    q_rows = (bm(qb_)[:, :, None, :] * hm[None, None].astype(mxu)).reshape(DB, TS * N_HEADS, A_WIDTH)
    att = _sample_attn(page_table, q_rows, sc, sc_new, thr.reshape(DB, TS, LANES), cut.reshape(DB, TS, LANES),
                       new_t(bm(kb)), new_t(bm(jnp.transpose(vt))), jnp.tile(hm, (TS, 1)), cache_k, cache_v, layer=l)
    att = jnp.transpose(att, (1, 0, 2)).reshape(T, A_WIDTH)

    eye = jnp.eye(DB, dtype=F32)
    ws = jnp.stack([jnp.kron(lw["ws"][g, :TS, :TS], eye) for g in range(SGU_GROUPS)])
    pc = SGU_CHUNK - T
    ws = jnp.pad(ws, ((0, 0), (0, pc), (0, pc)))
    btab = _sgu_bias_table(lw["sgu_b"], jnp.minimum(jnp.arange(SGU_CHUNK) // DB, SGU_CHUNK - 1))
    halo = jnp.concatenate([jnp.zeros((DB, POOL_WIDTH), F32),
                            jnp.transpose(state_pool[l], (1, 0, 2)).reshape(POOL_STATE * DB, POOL_WIDTH)], axis=0)
    x1 = _merge(x, att, u, vv, xc, halo, cnt, lw["g1"], lw["w_gt"], lw["wa"], lw["wb"], lw["wc"], lw["wout"],
                ws.astype(mxu), btab, lw["wpool"], lw["pscale"],
                tm=T, row_stride=DB, tiles_per_seq=1, halo_rows=halo.shape[0], halo_from_xc=False, cnt_tiles=1)
    dff = lw["w_ffa"].shape[1]
    ff_halo = jnp.transpose(state_ffn[l], (1, 0, 2)).reshape((CONV_W - 1) * DB, dff)
    tail = (CONV_W - 1) * DB
    x2, tails = _ffn(x1, lw["g2"], lw["w_ffa"], lw["w_ffu"], lw["cw"], lw["cb"], lw["w_ffd"], ff_halo,
                     tm=T, row_stride=DB, tiles_per_seq=1, use_carry=False, tail=tail)
    ff_state = jnp.transpose(tails.reshape(CONV_W - 1, DB, dff), (1, 0, 2))
    pool_state = jnp.concatenate([state_pool[l], bm(xc)], axis=1)[:, -POOL_STATE:, :]
    hd = lambda a: bm(a).reshape(DB, TS, N_HEADS, HEAD_DIM)
    return x2, (hd(k), hd(v), bm(ikw)[:, :, :IDX_DIM], bm(vv), pool_state, ff_state)


def kernel(x_prompt, x_sample, cache_k, cache_v, cache_idx_k, state_pool, state_ffn_conv, page_table, norm1_g, w_in, q_norm_g, k_norm_g, sgu_w, sgu_b, sgu_norm_g, pool_w, pool_scale, w_br_a, w_br_b, w_br_c, w_out, norm2_g, w_ff_in, ff_conv_w, ff_conv_b, w_ff_down):
    weights = (norm1_g, w_in, q_norm_g, k_norm_g, sgu_w, sgu_b, sgu_norm_g, pool_w, pool_scale,
               w_br_a, w_br_b, w_br_c, w_out, norm2_g, w_ff_in, ff_conv_w, ff_conv_b, w_ff_down)
    B, S, D = x_prompt.shape
    DB, TS, _ = x_sample.shape
    depth = w_in.shape[0]
    page = cache_idx_k.shape[2]
    past = page_table.shape[1] * page
    pool = cache_k.shape[1]
    cache_kt = jnp.transpose(cache_k, (0, 1, 3, 4, 2)).reshape(depth, pool, A_WIDTH, page)
    cache_vt = jnp.transpose(cache_v, (0, 1, 3, 4, 2)).reshape(depth, pool, A_WIDTH, page)
    cache_ikt = jnp.transpose(cache_idx_k, (0, 1, 3, 2))
    assert S % 512 == 0 and S >= POOL_STATE and TS * DB == SGU_CHUNK and TS >= CONV_W - 1 and TS <= page
    assert past % SGU_CHUNK == 0 and page_table.shape[1] % PAGES_PER_STEP == 0

    pos_p = np.arange(S)
    pos_s = past + np.repeat(np.arange(TS), DB)
    tabs_p, tabs_s = _rope_tables(pos_p), _rope_tables(pos_s)
    cnt_p, cnt_s = _pool_counts(pos_p), _pool_counts(pos_s)

    xp = x_prompt.reshape(B * S, D)
    xs = jnp.transpose(x_sample, (1, 0, 2)).reshape(TS * DB, D)
    outs_p, outs_s = [], []
    kv_state = None
    for l in range(depth):
        lw = _layer_weights(l, *weights)
        xp, kv_state, st_p = _prompt_layer(xp, lw, tabs_p, cnt_p, l, depth, kv_state, B=B, S=S)
        xs, st_s = _sample_layer(xs, lw, tabs_s, cnt_s, l, cache_kt, cache_vt, cache_ikt, state_pool,
                                 state_ffn_conv, page_table, DB=DB, TS=TS)
        outs_p.append(st_p)
        outs_s.append(st_s)
    stack = lambda outs, i: jnp.stack([o[i] for o in outs])
    y_p = xp.reshape(B, S, D)
    y_s = jnp.transpose(xs.reshape(TS, DB, D), (1, 0, 2))
    k_all, v_all, ik_all = kv_state
    heads = lambda a: jnp.transpose(a.reshape(depth, B, N_HEADS, HEAD_DIM, S), (0, 1, 4, 2, 3))
    return (y_p, y_s,
            heads(k_all), heads(v_all), jnp.transpose(ik_all, (0, 1, 3, 2)), stack(outs_p, 0), stack(outs_p, 1),
            stack(outs_s, 0), stack(outs_s, 1), stack(outs_s, 2), stack(outs_s, 3), stack(outs_s, 4), stack(outs_s, 5))
```

```python
import functools

import jax
import jax.numpy as jnp
import numpy as np
from jax import lax
from jax.experimental import pallas as pl
from jax.experimental.pallas import tpu as pltpu

MXU_DTYPE = jnp.bfloat16
F32 = jnp.float32
I32 = jnp.int32

N_HEADS = 8
HEAD_DIM = 64
A_WIDTH = N_HEADS * HEAD_DIM
IDX_HEADS = 8
IDX_DIM = 64
IDX_ROPE = 32
TOPK_MAX = 256
SGU_GROUPS = 4
SGU_WIDTH = 256
SGU_CHUNK = 128
POOL_GROUPS = 4
POOL_WIDTH = 256
POOL_WINDOWS = (2, 4, 8, 16)
POOL_STATE = 15
CONV_W = 3
ROPE_THETA = 10000.0
EPS = 1e-6

LANES = 128
SUBLANES = 8
INT_MIN = -2 ** 31
NEG_BIG = -1e30
Q_SCALE = HEAD_DIM ** -0.5 * float(np.log2(np.e))
PROJ_COLS = 3072
VMEM_LIMIT = 56 * 1024 * 1024

_NT = (((1,), (1,)), ((), ()))


def _cparams(n_axes):
    return pltpu.CompilerParams(dimension_semantics=("arbitrary",) * n_axes, vmem_limit_bytes=VMEM_LIMIT)


def _const_spec(shape):
    nd = len(shape)
    return pl.BlockSpec(shape, lambda *_: (0,) * nd)


def _rope128(x, cos, sin, half):
    lane = lax.broadcasted_iota(I32, x.shape, 1)
    lo = (lane & (HEAD_DIM - 1)) < half
    rot = jnp.where(lo, pltpu.roll(x, LANES - half, 1), pltpu.roll(x, half, 1))
    return x * cos + rot * sin


def _head_rms(x, seg, g):
    ss = jnp.dot((x * x).astype(seg.dtype), seg, preferred_element_type=F32)
    return x * lax.rsqrt(ss * (1.0 / HEAD_DIM) + EPS) * g


def _rms(x, g):
    return x * lax.rsqrt(jnp.mean(x * x, axis=-1, keepdims=True) + EPS) * g


def _proj_kernel(x_ref, g1_ref, w_ref, qg_ref, kg_ref, sgg_ref, tab_ref, seg_ref, *rest, stacked, n_alias):
    rest = rest[n_alias:]
    if stacked:
        qb_ref, k_ref, kb_ref, v_ref, vt_ref, iqb_ref, ikw_ref, ik_ref, ikd_ref, u_ref, vv_ref, xc_ref = rest
    else:
        qb_ref, k_ref, kb_ref, v_ref, vt_ref, iqb_ref, ikw_ref, ikd_ref, u_ref, vv_ref, xc_ref = rest
    mxu = w_ref.dtype
    xn = _rms(x_ref[...], g1_ref[...]).astype(mxu)

    def mm(c0, c1):
        return jnp.dot(xn, w_ref[:, c0:c1], preferred_element_type=F32)

    cqk, sqk = tab_ref[:, 0:128], tab_ref[:, 128:256]
    ciq, siq = tab_ref[:, 256:384], tab_ref[:, 384:512]
    cikw, sikw = tab_ref[:, 512:640], tab_ref[:, 640:768]
    seg = seg_ref[...]
    half_qk = HEAD_DIM // 2
    half_idx = IDX_ROPE // 2

    q = _head_rms(mm(0, 512), seg, qg_ref[...])
    for c in range(4):
        sl = slice(LANES * c, LANES * (c + 1))
        qb_ref[:, sl] = (_rope128(q[:, sl], cqk, sqk, half_qk) * Q_SCALE).astype(mxu)
    k = _head_rms(mm(512, 1024), seg, kg_ref[...])
    for c in range(4):
        sl = slice(LANES * c, LANES * (c + 1))
        kr = _rope128(k[:, sl], cqk, sqk, half_qk)
        if stacked:
            k_ref[sl, :] = jnp.transpose(kr)
        else:
            k_ref[:, sl] = kr
        kb_ref[:, sl] = kr.astype(mxu)
    v = mm(1024, 1536)
    vt = jnp.transpose(v)
    v_ref[...] = vt if stacked else v
    vt_ref[...] = vt.astype(mxu)
    iq = mm(1536, 2048)
    for c in range(4):
        sl = slice(LANES * c, LANES * (c + 1))
        iqb_ref[:, sl] = (_rope128(iq[:, sl], ciq, siq, half_idx) * (IDX_DIM ** -0.5)).astype(mxu)
    ikw = _rope128(mm(2048, 2176), cikw, sikw, half_idx)
    ikw_ref[...] = ikw
    if stacked:
        ik_ref[...] = jnp.transpose(ikw)[:IDX_DIM, :]
    ikd_ref[...] = _rope128(mm(2176, 2304), ciq, siq, half_idx).astype(mxu)
    sg = jax.nn.gelu(mm(2304, 2816))
    u_ref[...] = sg[:, :SGU_WIDTH]
    vv_ref[...] = _rms(sg[:, SGU_WIDTH:], sgg_ref[...])
    xc_ref[...] = mm(2816, 3072)


def _proj(x, g1, w, qg, kg, sgg, tab, seg, *, tm, tab_tiles, stack=None):
    T, D = x.shape
    mxu = w.dtype
    n = T // tm
    row = lambda width: pl.BlockSpec((tm, width), lambda i: (i, 0))
    natural = lambda width, dt: (jax.ShapeDtypeStruct((T, width), dt), row(width))
    outs = dict(
        qb=natural(A_WIDTH, mxu),
        k=natural(A_WIDTH, F32), kb=natural(A_WIDTH, mxu), v=natural(A_WIDTH, F32),
        vt=(jax.ShapeDtypeStruct((A_WIDTH, T), mxu), pl.BlockSpec((A_WIDTH, tm), lambda i: (0, i))),
        iqb=natural(IDX_HEADS * IDX_DIM, mxu),
        ikw=natural(LANES, F32),
        ik=None,
        ikd=natural(LANES, mxu),
        u=natural(SGU_WIDTH, F32), vv=natural(SGU_WIDTH, F32), xc=natural(POOL_WIDTH, F32),
    )
    in_specs = [
        row(D), _const_spec(g1.shape), _const_spec(w.shape), _const_spec(qg.shape), _const_spec(kg.shape),
        _const_spec(sgg.shape),
        pl.BlockSpec((tm, tab.shape[1]), lambda i: (i % tab_tiles, 0)),
        _const_spec(seg.shape),
    ]
    args = [x, g1, w, qg, kg, sgg, tab, seg]
    aliases = {}
    if stack is not None:
        layer, depth, B, S, state = stack
        tps = S // tm
        state_out = lambda feat: (jax.ShapeDtypeStruct((depth, B, feat, S), F32),
                                  pl.BlockSpec((None, None, feat, tm), lambda i: (layer, i // tps, 0, i % tps)))
        outs.update(k=state_out(A_WIDTH), v=state_out(A_WIDTH), ik=state_out(IDX_DIM))
        if state is not None:
            names = list(k for k, o in outs.items() if o is not None)
            for buf, name in zip(state, ("k", "v", "ik")):
                aliases[len(args)] = names.index(name)
                args.append(buf)
                in_specs.append(pl.BlockSpec(memory_space=pl.ANY))
    outs = {k: o for k, o in outs.items() if o is not None}
    kern = functools.partial(_proj_kernel, stacked=stack is not None, n_alias=len(aliases))
    res = pl.pallas_call(
        kern, grid=(n,), in_specs=in_specs, out_specs=[o[1] for o in outs.values()],
        out_shape=[o[0] for o in outs.values()], input_output_aliases=aliases,
        compiler_params=_cparams(1), name="proj",
    )(*args)
    return dict(zip(outs.keys(), res))


def _sortable_key(s):
    b = lax.bitcast_convert_type(s, I32)
    key = jnp.where(b < 0, b ^ jnp.int32(0x7FFFFFFF), b)
    return jnp.where(key == -1, 0, key)


def _fold_lanes(m):
    out = m[:, 0:LANES]
    for c in range(1, m.shape[1] // LANES):
        out = out + m[:, c * LANES:(c + 1) * LANES]
    return out


def _fold_sublanes(m):
    out = m[0:SUBLANES]
    for r in range(1, m.shape[0] // SUBLANES):
        out = out + m[r * SUBLANES:(r + 1) * SUBLANES]
    return out


I16 = jnp.int16
PACK16 = 2 * SUBLANES


def _fold_pack16(m):
    out = m[0:PACK16]
    for r in range(1, m.shape[0] // PACK16):
        out = out + m[r * PACK16:(r + 1) * PACK16]
    return out


def _tied_below_counter(kint_ref, iota_ref, eq_ref, t, n_tiles, tile, rows):
    iota_ref[0:tile, :] = lax.broadcasted_iota(I32, (tile, rows), 0).astype(I16)

    def build(j, carry):
        off = pl.multiple_of(j * tile, tile)
        eq_ref[pl.ds(off, tile), :] = jnp.where(kint_ref[pl.ds(off, tile), :] == t, 1, 0).astype(I16)
        return carry

    lax.fori_loop(0, n_tiles, build, 0)

    def counter(cand):
        def body(j, c):
            off = pl.multiple_of(j * tile, tile)
            local = jnp.clip(cand - off, 0, tile).astype(I16)
            m = jnp.where(iota_ref[0:tile, :] < local, eq_ref[pl.ds(off, tile), :], jnp.int16(0))
            return c + _fold_pack16(m)
        part = lax.fori_loop(0, n_tiles, body, jnp.zeros((PACK16, rows), I16))
        return jnp.sum(part.astype(F32), axis=0, keepdims=True)

    return counter


def _kth_largest_by_halves(khi_ref, klo_ref, low_ref, n_tiles, tile, rows, kf):
    def count_ge(ref, cand):
        c16 = cand.astype(I16)

        def body(j, c):
            off = pl.multiple_of(j * tile, tile)
            m = jnp.where(ref[pl.ds(off, tile), :] >= c16, jnp.int16(1), jnp.int16(0))
            return c + _fold_pack16(m)
        part = lax.fori_loop(0, n_tiles, body, jnp.zeros((PACK16, rows), I16))
        return jnp.sum(part.astype(F32), axis=0, keepdims=True)

    def bisect(ref):
        c0 = count_ge(ref, jnp.zeros((1, rows), I32))
        ok = c0 >= kf
        state = (jnp.where(ok, jnp.int32(0), jnp.int32(-32768)), jnp.where(ok, c0, 0.0))

        def bit_body(b, state):
            t, c = state
            cand = t + jnp.left_shift(jnp.int32(1), 14 - b)
            cnt = count_ge(ref, cand)
            ok = cnt >= kf
            return jnp.where(ok, cand, t), jnp.where(ok, cnt, c)

        return lax.fori_loop(0, 15, bit_body, state)

    hi, cnt_hi = bisect(khi_ref)
    hi16 = hi.astype(I16)

    def build(j, carry):
        off = pl.multiple_of(j * tile, tile)
        kh = khi_ref[pl.ds(off, tile), :]
        low_ref[pl.ds(off, tile), :] = jnp.where(
            kh > hi16, jnp.int16(32767), jnp.where(kh == hi16, klo_ref[pl.ds(off, tile), :], jnp.int16(-32768)))
        return carry

    lax.fori_loop(0, n_tiles, build, 0)
    lo, cnt = bisect(low_ref)
    return jnp.left_shift(hi, 16) + (lo + 32768), jnp.where(lo == -32768, cnt_hi, cnt)


def _topk_threshold(kint_ref, thr_ref, cut_ref, n_tiles, tile, rows, topk, n_cols, keys_axis=1, halves=None):
    stat_shape = (rows, 1) if keys_axis == 1 else (1, rows)

    def count(pred):
        def body(j, c):
            off = pl.multiple_of(j * tile, tile)
            if keys_axis == 1:
                key = kint_ref[:, pl.ds(off, tile)]
                idx = off + lax.broadcasted_iota(I32, (rows, tile), 1)
                return c + _fold_lanes(pred(key, idx).astype(F32))
            key = kint_ref[pl.ds(off, tile), :]
            idx = off + lax.broadcasted_iota(I32, (tile, rows), 0)
            return c + _fold_sublanes(pred(key, idx).astype(F32))
        init = jnp.zeros((rows, LANES) if keys_axis == 1 else (SUBLANES, rows), F32)
        part = lax.fori_loop(0, n_tiles, body, init)
        return jnp.sum(part, axis=keys_axis, keepdims=True)

    kf = float(topk)
    if halves is None:
        c0 = count(lambda key, col: key >= 0)
        t = jnp.where(c0 >= kf, jnp.int32(0), jnp.int32(INT_MIN))

        def bit_body(b, t):
            cand = t + jnp.left_shift(jnp.int32(1), 30 - b)
            cnt = count(lambda key, col: key >= cand)
            return jnp.where(cnt >= kf, cand, t)

        t = lax.fori_loop(0, 31, bit_body, t)
        t = jnp.maximum(t, jnp.int32(INT_MIN + 1))
        cge = count(lambda key, col: key >= t)
    else:
        t, cge = _kth_largest_by_halves(*halves, n_tiles, tile, rows, kf)
        t = jnp.maximum(t, jnp.int32(INT_MIN + 1))
    thr_ref[...] = t
    cut_ref[...] = jnp.full(stat_shape, n_cols, I32)

    @pl.when(jnp.max(cge) > kf)
    def _():
        need = kf - count(lambda key, col: key > t)
        nbits = max(1, (n_cols - 1).bit_length())

        if halves is None:
            count_tied_below = lambda cand: count(lambda key, col: (key == t) & (col < cand))
        else:
            count_tied_below = _tied_below_counter(kint_ref, halves[1], halves[2], t, n_tiles, tile, rows)

        def idx_body(b, p):
            cand = p + jnp.left_shift(jnp.int32(1), nbits - 1 - b)
            return jnp.where(count_tied_below(cand) < need, cand, p)

        p = lax.fori_loop(0, nbits, idx_body, jnp.zeros(stat_shape, I32))
        cut_ref[...] = jnp.where(cge > kf, p, jnp.int32(n_cols))


def _selected(key, col, t, cut):
    return (key > t) | ((key == t) & (col <= cut))


KEY_TILE = 128


def _dsa_prompt_kernel(qb_ref, iqb_ref, ikw_ref, kb_ref, vt_ref, ikd_ref, o_ref,
                       kint_ref, khi_ref, klo_ref, low_ref,
                       qm_ref, iqm_ref, w_ref, s0_ref, s1_ref, p0_ref, p1_ref, a0_ref, a1_ref,
                       acc_ref, m_ref, l_ref, thr_ref, cut_ref, *, topk, qb, seq):
    i = pl.program_id(1)
    kt = KEY_TILE
    assert qb == 2 * kt
    n_tiles = 2 * (i + 1)
    last = n_tiles - 1
    s_slots, p_slots, a_slots = (s0_ref, s1_ref), (p0_ref, p1_ref), (a0_ref, a1_ref)
    lane = lax.broadcasted_iota(I32, (qb, LANES), 1)
    lo = lane < HEAD_DIM
    for h in range(N_HEADS):
        sl = slice(LANES * (h // 2), LANES * (h // 2 + 1))
        msk = lo if h % 2 == 0 else jnp.logical_not(lo)
        qm_ref[h] = jnp.where(msk, qb_ref[0, :, sl], jnp.zeros((), qb_ref.dtype))
        iqm_ref[h] = jnp.where(msk, iqb_ref[0, :, sl], jnp.zeros((), iqb_ref.dtype))
    w_ref[...] = jnp.transpose(ikw_ref[0])[IDX_DIM:IDX_DIM + IDX_HEADS, :]
    qpos = i * qb + lax.broadcasted_iota(I32, (1, qb), 1)

    def tile_off(tile):
        return pl.multiple_of(jnp.clip(tile, 0, last) * kt, kt)

    def idx_dots(tile, s_out):
        ik = ikd_ref[0, pl.ds(tile_off(tile), kt), :]
        for h in range(IDX_HEADS):
            s_out[h] = lax.dot_general(ik, iqm_ref[h], _NT, preferred_element_type=F32)

    def combine(tile, s_in):
        sc = jnp.zeros((kt, qb), F32)
        for h in range(IDX_HEADS):
            sc = sc + jnp.maximum(s_in[h], 0.0) * w_ref[h:h + 1, :]
        off = tile_off(tile)
        kpos = off + lax.broadcasted_iota(I32, (kt, qb), 0)
        key = jnp.where(kpos <= qpos, _sortable_key(sc), jnp.int32(INT_MIN))
        kint_ref[pl.ds(off, kt), :] = key
        khi_ref[pl.ds(off, kt), :] = jnp.right_shift(key, 16).astype(I16)
        klo_ref[pl.ds(off, kt), :] = ((key & 0xFFFF) - 32768).astype(I16)

    idx_dots(0, s0_ref)

    def score_pair(jj, carry):
        for step in range(2):
            tile = 2 * jj + step
            combine(tile, s_slots[step])
            idx_dots(tile + 1, s_slots[1 - step])
        return carry

    lax.fori_loop(0, i + 1, score_pair, 0)
    pad = pl.ds(pl.multiple_of((i + 1) * qb, qb), qb)
    kint_ref[pad, :] = jnp.full((qb, qb), INT_MIN, I32)
    khi_ref[pad, :] = jnp.full((qb, qb), -32768, I16)
    klo_ref[pad, :] = jnp.full((qb, qb), -32768, I16)
    _topk_threshold(kint_ref, thr_ref, cut_ref, (i + 2) // 2, 2 * qb, qb, topk, seq, keys_axis=0,
                    halves=(khi_ref, klo_ref, low_ref))
    thr = thr_ref[...]
    cut = cut_ref[...]

    acc_ref[...] = jnp.zeros_like(acc_ref)
    m_ref[...] = jnp.full(m_ref.shape, NEG_BIG, F32)
    l_ref[...] = jnp.zeros_like(l_ref)
    p1_ref[...] = jnp.zeros_like(p1_ref)
    a1_ref[...] = jnp.ones_like(a1_ref)

    def qk_dots(tile, s_out):
        off = tile_off(tile)
        for h in range(N_HEADS):
            kk = kb_ref[0, pl.ds(off, kt), LANES * (h // 2):LANES * (h // 2 + 1)]
            s_out[h] = lax.dot_general(kk, qm_ref[h], _NT, preferred_element_type=F32)

    def softmax(tile, s_in, p_out, a_out):
        off = tile_off(tile)
        live = tile <= last
        kpos = off + lax.broadcasted_iota(I32, (kt, qb), 0)
        sel = _selected(kint_ref[pl.ds(off, kt), :], kpos,
                        jnp.where(live, thr, jnp.int32(2 ** 31 - 1)), jnp.where(live, cut, jnp.int32(-1)))
        bias = jnp.where(sel, 0.0, NEG_BIG)
        for h in range(N_HEADS):
            for ls in (slice(0, LANES), slice(LANES, 2 * LANES)):
                s = s_in[h, :, ls] + bias[:, ls]
                m_old = m_ref[h, :, ls]
                m_new = jnp.maximum(m_old, jnp.max(s, axis=0, keepdims=True))
                alpha = jnp.exp2(m_old - m_new)
                pm = jnp.exp2(s - m_new)
                m_ref[h, :, ls] = m_new
                p_out[h, :, ls] = pm.astype(p_out.dtype)
                a_out[h, :, ls] = alpha

    ones_rows = jnp.ones((PACK16, kt), vt_ref.dtype)

    def pv_dots(tile, p_in, a_in):
        off = tile_off(tile)
        for h in range(N_HEADS):
            dr = slice(HEAD_DIM * h, HEAD_DIM * (h + 1))
            lhs = jnp.concatenate([vt_ref[dr, pl.ds(off, kt)], ones_rows], axis=0)
            pv = jnp.dot(lhs, p_in[h], preferred_element_type=F32)
            acc_ref[dr, :] = a_in[h] * acc_ref[dr, :] + pv[:HEAD_DIM]
            l_ref[h] = a_in[h] * l_ref[h] + pv[HEAD_DIM:HEAD_DIM + 1]

    qk_dots(0, s0_ref)

    def att_pair(jj, carry):
        for step in range(2):
            tile = 2 * jj + step
            softmax(tile, s_slots[step], p_slots[step], a_slots[step])
            qk_dots(tile + 1, s_slots[1 - step])
            pv_dots(tile - 1, p_slots[1 - step], a_slots[1 - step])
        return carry

    lax.fori_loop(0, i + 2, att_pair, 0)
    for h in range(N_HEADS):
        dr = slice(HEAD_DIM * h, HEAD_DIM * (h + 1))
        acc_ref[dr, :] = acc_ref[dr, :] / l_ref[h]
    o_ref[0] = jnp.transpose(acc_ref[...]).astype(o_ref.dtype)


def _dsa_prompt(qb_, iqb, ikw, kb, vt, ikd, *, qb, topk):
    B, S, _ = qb_.shape
    mxu = qb_.dtype
    blk = lambda width: pl.BlockSpec((1, qb, width), lambda b, i: (b, i, 0))
    full = lambda width: pl.BlockSpec((1, S, width), lambda b, i: (b, 0, 0))
    kern = functools.partial(_dsa_prompt_kernel, topk=topk, qb=qb, seq=S)
    return pl.pallas_call(
        kern, grid=(B, S // qb),
        in_specs=[blk(A_WIDTH), blk(A_WIDTH), blk(LANES), full(A_WIDTH),
                  pl.BlockSpec((A_WIDTH, S), lambda b, i: (0, b)), full(LANES)],
        out_specs=blk(A_WIDTH),
        out_shape=jax.ShapeDtypeStruct((B, S, A_WIDTH), mxu),
        scratch_shapes=[
            pltpu.VMEM((S + qb, qb), I32),
            pltpu.VMEM((S + qb, qb), I16), pltpu.VMEM((S + qb, qb), I16), pltpu.VMEM((S + qb, qb), I16),
            pltpu.VMEM((N_HEADS, qb, LANES), mxu),
            pltpu.VMEM((N_HEADS, qb, LANES), mxu),
            pltpu.VMEM((IDX_HEADS, qb), F32),
            pltpu.VMEM((N_HEADS, KEY_TILE, qb), F32),
            pltpu.VMEM((N_HEADS, KEY_TILE, qb), F32),
            pltpu.VMEM((N_HEADS, KEY_TILE, qb), mxu),
            pltpu.VMEM((N_HEADS, KEY_TILE, qb), mxu),
            pltpu.VMEM((N_HEADS, 1, qb), F32),
            pltpu.VMEM((N_HEADS, 1, qb), F32),
            pltpu.VMEM((A_WIDTH, qb), F32),
            pltpu.VMEM((N_HEADS, 1, qb), F32),
            pltpu.VMEM((N_HEADS, 1, qb), F32),
            pltpu.VMEM((1, qb), I32),
            pltpu.VMEM((1, qb), I32),
        ],
        compiler_params=_cparams(2), name="dsa_prompt",
    )(qb_, iqb, ikw, kb, vt, ikd)


PAGES_PER_STEP = 32


def _head_sum(s):
    return jnp.concatenate(
        [jnp.sum(s[IDX_HEADS * t:IDX_HEADS * (t + 1)], axis=0, keepdims=True) for t in range(s.shape[0] // IDX_HEADS)],
        axis=0)


def _sample_scores_kernel(pt_ref, iq_ref, w_ref, ikn_ref, *rest, n_steps):
    pages = rest[:PAGES_PER_STEP]
    sc_ref, scn_ref = rest[PAGES_PER_STEP:]
    c = pl.program_id(1)
    iq = iq_ref[0]
    w = w_ref[0]

    def scores(ik):
        s = jnp.dot(iq, ik.astype(iq.dtype), preferred_element_type=F32)
        return _head_sum(jnp.maximum(s, 0.0) * w)

    sc_ref[0] = scores(jnp.concatenate([r[...].astype(iq.dtype) for r in pages], axis=1))

    @pl.when(c == n_steps - 1)
    def _():
        s = scores(ikn_ref[0])
        col = lax.broadcasted_iota(I32, s.shape, 1)
        row = lax.broadcasted_iota(I32, s.shape, 0)
        scn_ref[0] = jnp.where(col <= row, s, -jnp.inf)


def _page_specs(layer, rows, page):
    def make(r):
        def imap(b, c, pt):
            return (layer, pt[b, c * PAGES_PER_STEP + r], 0, 0)
        return pl.BlockSpec((None, None, rows, page), imap)
    return [make(r) for r in range(PAGES_PER_STEP)]


def _sample_scores(page_table, iq_rows, w_rows, ik_new, cache_ik, *, layer):
    DB, n_pages = page_table.shape
    page = cache_ik.shape[3]
    n_steps = n_pages // PAGES_PER_STEP
    rows = iq_rows.shape[1]
    nt = rows // IDX_HEADS
    step_cols = PAGES_PER_STEP * page
    grid_spec = pltpu.PrefetchScalarGridSpec(
        num_scalar_prefetch=1, grid=(DB, n_steps),
        in_specs=[
            pl.BlockSpec((1, rows, IDX_DIM), lambda b, c, pt: (b, 0, 0)),
            pl.BlockSpec((1, rows, 1), lambda b, c, pt: (b, 0, 0)),
            pl.BlockSpec((1, IDX_DIM, page), lambda b, c, pt: (b, 0, 0)),
        ] + _page_specs(layer, IDX_DIM, page),
        out_specs=[pl.BlockSpec((1, nt, step_cols), lambda b, c, pt: (b, 0, c)),
                   pl.BlockSpec((1, nt, page), lambda b, c, pt: (b, 0, 0))],
    )
    kern = functools.partial(_sample_scores_kernel, n_steps=n_steps)
    return pl.pallas_call(
        kern, grid_spec=grid_spec,
        out_shape=[jax.ShapeDtypeStruct((DB, nt, n_pages * page), F32), jax.ShapeDtypeStruct((DB, nt, page), F32)],
        compiler_params=_cparams(2), name="sample_scores",
    )(page_table, iq_rows, w_rows, ik_new, *([cache_ik] * PAGES_PER_STEP))


def _score_key(sc):
    return jnp.where(sc == -jnp.inf, jnp.int32(INT_MIN), _sortable_key(sc))


def _sample_threshold_kernel(sc_ref, scn_ref, thr_ref, cut_ref, kint_ref, t_ref, c_ref, *, topk, tile):
    rows, past = sc_ref.shape
    page = scn_ref.shape[1]
    cols = kint_ref.shape[1]
    kint_ref[:, 0:past] = _score_key(sc_ref[...])
    kint_ref[:, past:past + page] = _score_key(scn_ref[...])
    kint_ref[:, past + page:] = jnp.full((rows, cols - past - page), INT_MIN, I32)
    _topk_threshold(kint_ref, t_ref, c_ref, cols // tile, tile, rows, topk, cols)
    thr_ref[...] = jnp.broadcast_to(t_ref[...], thr_ref.shape)
    cut_ref[...] = jnp.broadcast_to(c_ref[...], cut_ref.shape)


def _sample_threshold(sc, sc_new, *, topk):
    rows, past = sc.shape
    page = sc_new.shape[1]
    tile = 2 * LANES
    cols = pl.cdiv(past + page, tile) * tile
    kern = functools.partial(_sample_threshold_kernel, topk=topk, tile=tile)
    return pl.pallas_call(
        kern, grid=(1,),
        in_specs=[_const_spec(sc.shape), _const_spec(sc_new.shape)],
        out_specs=[_const_spec((rows, LANES)), _const_spec((rows, LANES))],
        out_shape=[jax.ShapeDtypeStruct((rows, LANES), I32)] * 2,
        scratch_shapes=[pltpu.VMEM((rows, cols), I32), pltpu.VMEM((rows, 1), I32), pltpu.VMEM((rows, 1), I32)],
        compiler_params=_cparams(1), name="sample_threshold",
    )(sc, sc_new)


def _sample_attn_kernel(pt_ref, q_ref, sc_ref, scn_ref, thr_ref, cut_ref, kn_ref, vn_ref, hm_ref, *rest, n_steps):
    kp = rest[:PAGES_PER_STEP]
    vp = rest[PAGES_PER_STEP:2 * PAGES_PER_STEP]
    o_ref = rest[2 * PAGES_PER_STEP]
    m_ref, l_ref, acc_ref = rest[2 * PAGES_PER_STEP + 1:]
    c = pl.program_id(1)
    q = q_ref[0]
    mxu = q.dtype
    nt = sc_ref.shape[1]
    step_cols = sc_ref.shape[2]

    @pl.when(c == 0)
    def _():
        m_ref[...] = jnp.full(m_ref.shape, NEG_BIG, F32)
        l_ref[...] = jnp.zeros_like(l_ref)
        acc_ref[...] = jnp.zeros_like(acc_ref)

    def rows_th(x):
        return jnp.concatenate([jnp.broadcast_to(x[t:t + 1], (N_HEADS, x.shape[1])) for t in range(nt)], axis=0)

    def update(keys, vals, sc, col0):
        key = _score_key(sc)
        col = col0 + lax.broadcasted_iota(I32, key.shape, 1)
        sel = rows_th(_selected(key, col, thr_ref[0, :, 0:1], cut_ref[0, :, 0:1]).astype(F32)) > 0.5
        kcat = jnp.concatenate([kk.astype(mxu) for kk in keys], axis=1)
        s = jnp.dot(q, kcat, preferred_element_type=F32)
        s = jnp.where(sel, s, NEG_BIG)
        m_old = m_ref[...]
        m_new = jnp.maximum(m_old, jnp.max(s, axis=1, keepdims=True))
        alpha = jnp.exp2(m_old - m_new)
        pm = jnp.exp2(s - m_new)
        l_ref[...] = alpha * l_ref[...] + jnp.sum(pm, axis=1, keepdims=True)
        m_ref[...] = m_new
        vcat = jnp.concatenate([vv.astype(mxu) for vv in vals], axis=1)
        pv = lax.dot_general(pm.astype(mxu), vcat, _NT, preferred_element_type=F32)
        acc_ref[...] = alpha * acc_ref[...] + pv

    update([r[...] for r in kp], [r[...] for r in vp], sc_ref[0], c * step_cols)

    @pl.when(c == n_steps - 1)
    def _():
        update([kn_ref[0]], [vn_ref[0]], scn_ref[0], n_steps * step_cols)
        out = acc_ref[...] / l_ref[...] * hm_ref[...]
        o_ref[0] = _head_sum(out).astype(o_ref.dtype)


def _sample_attn(page_table, q_rows, sc, sc_new, thr, cut, k_new, v_new, head_mask, cache_k, cache_v, *, layer):
    DB, n_pages = page_table.shape
    page = cache_k.shape[3]
    n_steps = n_pages // PAGES_PER_STEP
    rows = q_rows.shape[1]
    nt = rows // N_HEADS
    step_cols = PAGES_PER_STEP * page
    per_b = lambda shape: pl.BlockSpec((1,) + shape, lambda b, c, pt: (b, 0, 0))
    grid_spec = pltpu.PrefetchScalarGridSpec(
        num_scalar_prefetch=1, grid=(DB, n_steps),
        in_specs=[
            per_b((rows, A_WIDTH)),
            pl.BlockSpec((1, nt, step_cols), lambda b, c, pt: (b, 0, c)),
            per_b((nt, page)), per_b((nt, LANES)), per_b((nt, LANES)),
            per_b((A_WIDTH, page)), per_b((A_WIDTH, page)),
            pl.BlockSpec((rows, A_WIDTH), lambda b, c, pt: (0, 0)),
        ] + _page_specs(layer, A_WIDTH, page) * 2,
        out_specs=per_b((nt, A_WIDTH)),
        scratch_shapes=[pltpu.VMEM((rows, 1), F32), pltpu.VMEM((rows, 1), F32), pltpu.VMEM((rows, A_WIDTH), F32)],
    )
    kern = functools.partial(_sample_attn_kernel, n_steps=n_steps)
    return pl.pallas_call(
        kern, grid_spec=grid_spec,
        out_shape=jax.ShapeDtypeStruct((DB, nt, A_WIDTH), q_rows.dtype),
        compiler_params=_cparams(2), name="sample_attn",
    )(page_table, q_rows, sc, sc_new, thr, cut, k_new, v_new, head_mask,
      *([cache_k] * PAGES_PER_STEP), *([cache_v] * PAGES_PER_STEP))


def _shift_rows(e, k):
    return pltpu.roll(e, k, 0)


def _merge_kernel(x_ref, att_ref, u_ref, vv_ref, xc_ref, halo_ref, cnt_ref,
                  g1_ref, wgt_ref, wa_ref, wb_ref, wc_ref, wout_ref, ws_ref, btab_ref, wpool_ref, pscale_ref,
                  o_ref, *, row_stride, tiles_per_seq, zero_first_halo):
    mxu = wgt_ref.dtype
    tm = x_ref.shape[0]
    x = x_ref[...]
    xn = _rms(x, g1_ref[...]).astype(mxu)
    d = x.shape[1]

    def gate(n):
        return jax.nn.sigmoid(jnp.dot(xn, wgt_ref[:, n * d:(n + 1) * d], preferred_element_type=F32))

    merged = gate(0) * jnp.dot(att_ref[...], wa_ref[...], preferred_element_type=F32)

    lane = lax.broadcasted_iota(I32, (SGU_CHUNK, LANES), 1)
    lo = lane < (SGU_WIDTH // SGU_GROUPS)
    sgo = []
    for c in range(tm // SGU_CHUNK):
        rs = slice(c * SGU_CHUNK, (c + 1) * SGU_CHUNK)
        vvb = vv_ref[rs, :].astype(mxu)
        mix = []
        for p in range(SGU_GROUPS // 2):
            pair = vvb[:, LANES * p:LANES * (p + 1)]
            r0 = jnp.dot(ws_ref[2 * p], pair, preferred_element_type=F32)
            r1 = jnp.dot(ws_ref[2 * p + 1], pair, preferred_element_type=F32)
            mix.append(jnp.where(lo, r0, r1))
        sgo.append(u_ref[rs, :] * (jnp.concatenate(mix, axis=1) + btab_ref[...]))
    sgo = jnp.concatenate(sgo, axis=0).astype(mxu)
    merged = merged + gate(1) * jnp.dot(sgo, wb_ref[...], preferred_element_type=F32)

    xc = xc_ref[...]
    halo = halo_ref[...]
    if zero_first_halo:
        first = (pl.program_id(0) % tiles_per_seq) == 0
        halo = jnp.where(first, jnp.zeros_like(halo), halo)
    hp = halo.shape[0]
    e = jnp.concatenate([halo, xc], axis=0)
    s1 = e + _shift_rows(e, row_stride)
    s2 = s1 + _shift_rows(s1, 2 * row_stride)
    s3 = s2 + _shift_rows(s2, 4 * row_stride)
    s4 = s3 + _shift_rows(s3, 8 * row_stride)
    gd = POOL_WIDTH // POOL_GROUPS
    wsum = jnp.concatenate([s[hp:, g * gd:(g + 1) * gd] for g, s in enumerate((s1, s2, s3, s4))], axis=1)
    pooled = (wsum / cnt_ref[...] - xc).astype(mxu)
    po = (jnp.dot(pooled, wpool_ref[...], preferred_element_type=F32) * pscale_ref[...]).astype(mxu)
    merged = merged + gate(2) * jnp.dot(po, wc_ref[...], preferred_element_type=F32)

    o_ref[...] = x + jnp.dot(merged.astype(mxu), wout_ref[...], preferred_element_type=F32)


def _merge(x, att, u, vv, xc, halo, cnt, g1, wgt, wa, wb, wc, wout, ws, btab, wpool, pscale,
           *, tm, row_stride, tiles_per_seq, halo_rows, halo_from_xc, cnt_tiles):
    T, D = x.shape
    n = T // tm
    row = lambda width: pl.BlockSpec((tm, width), lambda i: (i, 0))
    if halo_from_xc:
        per = tm // halo_rows
        halo_spec = pl.BlockSpec((halo_rows, xc.shape[1]), lambda i: (jnp.maximum(i * per - 1, 0), 0))
    else:
        halo_spec = _const_spec(halo.shape)
    consts = [g1, wgt, wa, wb, wc, wout, ws, btab, wpool, pscale]
    kern = functools.partial(_merge_kernel, row_stride=row_stride, tiles_per_seq=tiles_per_seq,
                             zero_first_halo=halo_from_xc)
    return pl.pallas_call(
        kern, grid=(n,),
        in_specs=[row(D), row(att.shape[1]), row(u.shape[1]), row(vv.shape[1]), row(xc.shape[1]), halo_spec,
                  pl.BlockSpec((tm, cnt.shape[1]), lambda i: (i % cnt_tiles, 0))]
        + [_const_spec(c.shape) for c in consts],
        out_specs=row(D), out_shape=jax.ShapeDtypeStruct((T, D), F32),
        compiler_params=_cparams(1), name="merge",
    )(x, att, u, vv, xc, halo, cnt, *consts)


FFN_CHUNK = 2816


def _ffn_kernel(x_ref, g2_ref, wa_ref, wu_ref, cw_ref, cb_ref, wd_ref, halo_ref, o_ref, tail_ref, carry_ref,
                *, row_stride, tiles_per_seq, use_carry):
    mxu = wa_ref.dtype
    x = x_ref[...]
    tm = x.shape[0]
    xn = _rms(x, g2_ref[...]).astype(mxu)
    dff = wa_ref.shape[1]
    tail = tail_ref.shape[1]
    acc = jnp.zeros(x.shape, F32)
    if use_carry:
        first = (pl.program_id(0) % tiles_per_seq) == 0
    for c0 in range(0, dff, FFN_CHUNK):
        cs = slice(c0, c0 + FFN_CHUNK)
        a = jnp.dot(xn, wa_ref[:, cs], preferred_element_type=F32)
        up = jnp.dot(xn, wu_ref[:, cs], preferred_element_type=F32)
        if use_carry:
            halo = jnp.where(first, 0.0, carry_ref[:, cs])
            carry_ref[:, cs] = a[tm - carry_ref.shape[0]:, :]
        else:
            halo = halo_ref[:, cs]
        hp = halo.shape[0]
        e = jnp.concatenate([halo, a], axis=0)
        p1 = _shift_rows(e, row_stride)[hp:]
        p2 = _shift_rows(e, 2 * row_stride)[hp:]
        conv = p2 * cw_ref[0:1, cs] + p1 * cw_ref[1:2, cs] + a * cw_ref[2:3, cs] + cb_ref[:, cs]
        act = (jax.nn.silu(conv) * up).astype(mxu)
        acc = acc + jnp.dot(act, wd_ref[cs, :], preferred_element_type=F32)
        tail_ref[0, :, cs] = a[tm - tail:, :]
    o_ref[...] = x + acc


def _ffn(x, g2, wa, wu, cw, cb, wd, halo, *, tm, row_stride, tiles_per_seq, use_carry, tail):
    T, D = x.shape
    n = T // tm
    dff = wa.shape[1]
    row = pl.BlockSpec((tm, D), lambda i: (i, 0))
    consts = [g2, wa, wu, cw, cb, wd, halo]
    kern = functools.partial(_ffn_kernel, row_stride=row_stride, tiles_per_seq=tiles_per_seq, use_carry=use_carry)
    return pl.pallas_call(
        kern, grid=(n,),
        in_specs=[row] + [_const_spec(c.shape) for c in consts],
        out_specs=[row, pl.BlockSpec((1, tail, dff), lambda i: (i, 0, 0))],
        out_shape=[jax.ShapeDtypeStruct((T, D), F32), jax.ShapeDtypeStruct((n, tail, dff), F32)],
        scratch_shapes=[pltpu.VMEM((SUBLANES, dff), F32)],
        compiler_params=_cparams(1), name="ffn",
    )(x, *consts)


def _pack_w_in(w_in):
    d = w_in.shape[0]
    o = 0
    q, k, v = (w_in[:, o + i * A_WIDTH:o + (i + 1) * A_WIDTH] for i in range(3))
    o += 3 * A_WIDTH
    iq = w_in[:, o:o + IDX_HEADS * IDX_DIM]
    o += IDX_HEADS * IDX_DIM
    ik = w_in[:, o:o + IDX_DIM]
    o += IDX_DIM
    iw = w_in[:, o:o + IDX_HEADS]
    o += IDX_HEADS
    sg = w_in[:, o:o + 2 * SGU_WIDTH]
    o += 2 * SGU_WIDTH
    xc = w_in[:, o:o + POOL_WIDTH]
    o += POOL_WIDTH
    gt = w_in[:, o:]
    pad = jnp.zeros((d, LANES - IDX_DIM - IDX_HEADS), w_in.dtype)
    packed = jnp.concatenate([q, k, v, iq, ik, iw, pad, ik, ik, sg, xc], axis=1)
    assert packed.shape[1] == PROJ_COLS
    return packed, gt


def _rope_tables(pos):
    posf = np.asarray(pos, np.float64)[:, None]

    def cs(half):
        inv = ROPE_THETA ** (-np.arange(half, dtype=np.float64) / half)
        ang = posf * inv[None, :]
        return np.cos(ang), np.sin(ang)

    t = posf.shape[0]
    c32, s32 = cs(HEAD_DIM // 2)
    cqk = np.tile(np.concatenate([c32, c32], axis=1), (1, 2))
    sqk = np.tile(np.concatenate([-s32, s32], axis=1), (1, 2))
    c16, s16 = cs(IDX_ROPE // 2)
    rest = IDX_DIM - IDX_ROPE
    ci = np.concatenate([c16, c16, np.ones((t, rest))], axis=1)
    si = np.concatenate([-s16, s16, np.zeros((t, rest))], axis=1)
    ciq, siq = np.tile(ci, (1, 2)), np.tile(si, (1, 2))
    wpad = LANES - IDX_DIM - IDX_HEADS
    cikw = np.concatenate([ci, np.full((t, IDX_HEADS), IDX_HEADS ** -0.5), np.zeros((t, wpad))], axis=1)
    sikw = np.concatenate([si, np.zeros((t, LANES - IDX_DIM))], axis=1)
    return jnp.asarray(np.concatenate([cqk, sqk, ciq, siq, cikw, sikw], axis=1).astype(np.float32))


def _block_diag(blocks):
    n = len(blocks)
    r, c = blocks[0].shape
    out = jnp.zeros((n * r, n * c), blocks[0].dtype)
    for i, b in enumerate(blocks):
        out = out.at[i * r:(i + 1) * r, i * c:(i + 1) * c].set(b)
    return out


def _pool_counts(pos):
    gd = POOL_WIDTH // POOL_GROUPS
    pos = np.asarray(pos)
    cols = [np.broadcast_to(np.minimum(pos + 1, w).astype(np.float32)[:, None], (pos.shape[0], gd))
            for w in POOL_WINDOWS]
    return jnp.asarray(np.concatenate(cols, axis=1))


def _head_segments(dtype):
    return jnp.asarray(np.kron(np.eye(N_HEADS, dtype=np.float32), np.ones((HEAD_DIM, HEAD_DIM), np.float32)), dtype)


def _layer_weights(l, norm1_g, w_in, q_norm_g, k_norm_g, sgu_w, sgu_b, sgu_norm_g, pool_w, pool_scale,
                   w_br_a, w_br_b, w_br_c, w_out, norm2_g, w_ff_in, ff_conv_w, ff_conv_b, w_ff_down):
    mxu = MXU_DTYPE
    packed, gt = _pack_w_in(w_in[l])
    dff = w_ff_in.shape[2] // 2
    tril = jnp.tril(jnp.ones((SGU_CHUNK, SGU_CHUNK), bool))
    return dict(
        g1=norm1_g[l][None, :], w_proj=packed.astype(mxu), w_gt=gt.astype(mxu),
        qg=jnp.tile(q_norm_g[l], N_HEADS)[None, :], kg=jnp.tile(k_norm_g[l], N_HEADS)[None, :],
        sgg=sgu_norm_g[l][None, :],
        ws=jnp.where(tril[None], sgu_w[l], 0.0),
        sgu_b=sgu_b[l],
        wpool=_block_diag([pool_w[l, g] for g in range(POOL_GROUPS)]).astype(mxu),
        pscale=pool_scale[l][None, :],
        wa=w_br_a[l].astype(mxu), wb=w_br_b[l].astype(mxu), wc=w_br_c[l].astype(mxu), wout=w_out[l].astype(mxu),
        g2=norm2_g[l][None, :],
        w_ffa=w_ff_in[l, :, :dff].astype(mxu), w_ffu=w_ff_in[l, :, dff:].astype(mxu),
        cw=jnp.pad(ff_conv_w[l], ((0, SUBLANES - CONV_W), (0, 0))), cb=ff_conv_b[l][None, :],
        w_ffd=w_ff_down[l].astype(mxu),
    )


def _sgu_bias_table(sgu_b, t_of_row):
    gd = SGU_WIDTH // SGU_GROUPS
    return jnp.repeat(jnp.transpose(sgu_b)[t_of_row], gd, axis=1)


def _prompt_layer(x, lw, tabs, cnt, l, depth, state, *, B, S):
    mxu = MXU_DTYPE
    T = B * S
    tm_proj, tm_merge, tm_ffn, qb = 512, 512, 512, 256
    pr = _proj(x, lw["g1"], lw["w_proj"], lw["qg"], lw["kg"], lw["sgg"], tabs, _head_segments(mxu),
               tm=tm_proj, tab_tiles=S // tm_proj, stack=(l, depth, B, S, state))
    u, vv, xc = pr["u"], pr["vv"], pr["xc"]
    r3 = lambda a: a.reshape(B, S, a.shape[1])
    topk = min(TOPK_MAX, S // 4)
    att = _dsa_prompt(r3(pr["qb"]), r3(pr["iqb"]), r3(pr["ikw"]), r3(pr["kb"]), pr["vt"], r3(pr["ikd"]),
                      qb=qb, topk=topk).reshape(T, A_WIDTH)
    btab = _sgu_bias_table(lw["sgu_b"], jnp.arange(SGU_CHUNK))
    x1 = _merge(x, att, u, vv, xc, xc, cnt, lw["g1"], lw["w_gt"], lw["wa"], lw["wb"], lw["wc"], lw["wout"],
                lw["ws"].astype(mxu), btab, lw["wpool"], lw["pscale"],
                tm=tm_merge, row_stride=1, tiles_per_seq=S // tm_merge, halo_rows=16, halo_from_xc=True,
                cnt_tiles=S // tm_merge)
    dff = lw["w_ffa"].shape[1]
    x2, tails = _ffn(x1, lw["g2"], lw["w_ffa"], lw["w_ffu"], lw["cw"], lw["cb"], lw["w_ffd"],
                     jnp.zeros((SUBLANES, dff), F32),
                     tm=tm_ffn, row_stride=1, tiles_per_seq=S // tm_ffn, use_carry=True, tail=SUBLANES)
    n_t = S // tm_ffn
    ff_state = tails.reshape(B, n_t, SUBLANES, dff)[:, -1, SUBLANES - (CONV_W - 1):, :]
    pool_state = xc.reshape(B, S, POOL_WIDTH)[:, S - POOL_STATE:, :]
    return x2, (pr["k"], pr["v"], pr["ik"]), (pool_state, ff_state)


def _sample_layer(x, lw, tabs, cnt, l, cache_k, cache_v, cache_ik, state_pool, state_ffn, page_table, *, DB, TS):
    mxu = MXU_DTYPE
    T = TS * DB
    page = cache_ik.shape[3]
    past = page_table.shape[1] * page
    pr = _proj(x, lw["g1"], lw["w_proj"], lw["qg"], lw["kg"], lw["sgg"], tabs, _head_segments(mxu),
               tm=T, tab_tiles=1)
    qb_, k, kb, v, vt, iqb, ikw, ikd, u, vv, xc = (
        pr[n] for n in ("qb", "k", "kb", "v", "vt", "iqb", "ikw", "ikd", "u", "vv", "xc"))
    bm = lambda a: jnp.transpose(a.reshape(TS, DB, a.shape[1]), (1, 0, 2))

    iq_rows = bm(iqb).reshape(DB, TS * IDX_HEADS, IDX_DIM)
    w_rows = bm(ikw)[:, :, IDX_DIM:IDX_DIM + IDX_HEADS].reshape(DB, TS * IDX_HEADS, 1)
    new_t = lambda a: jnp.pad(jnp.transpose(a, (0, 2, 1)), ((0, 0), (0, 0), (0, page - TS)))
    ik_new = new_t(bm(ikd)[:, :, :IDX_DIM])
    sc, sc_new = _sample_scores(page_table, iq_rows, w_rows, ik_new, cache_ik, layer=l)
    topk = min(TOPK_MAX, (past + TS) // 4)
    thr, cut = _sample_threshold(sc.reshape(DB * TS, past), sc_new.reshape(DB * TS, page), topk=topk)
    hm = jnp.asarray(np.repeat(np.eye(N_HEADS, dtype=np.float32), HEAD_DIM, axis=1))
    q_rows = (bm(qb_)[:, :, None, :] * hm[None, None].astype(mxu)).reshape(DB, TS * N_HEADS, A_WIDTH)
    att = _sample_attn(page_table, q_rows, sc, sc_new, thr.reshape(DB, TS, LANES), cut.reshape(DB, TS, LANES),
                       new_t(bm(kb)), new_t(bm(jnp.transpose(vt))), jnp.tile(hm, (TS, 1)), cache_k, cache_v, layer=l)
    att = jnp.transpose(att, (1, 0, 2)).reshape(T, A_WIDTH)

    eye = jnp.eye(DB, dtype=F32)
    ws = jnp.stack([jnp.kron(lw["ws"][g, :TS, :TS], eye) for g in range(SGU_GROUPS)])
    pc = SGU_CHUNK - T
    ws = jnp.pad(ws, ((0, 0), (0, pc), (0, pc)))
    btab = _sgu_bias_table(lw["sgu_b"], jnp.minimum(jnp.arange(SGU_CHUNK) // DB, SGU_CHUNK - 1))
    halo = jnp.concatenate([jnp.zeros((DB, POOL_WIDTH), F32),
                            jnp.transpose(state_pool[l], (1, 0, 2)).reshape(POOL_STATE * DB, POOL_WIDTH)], axis=0)
    x1 = _merge(x, att, u, vv, xc, halo, cnt, lw["g1"], lw["w_gt"], lw["wa"], lw["wb"], lw["wc"], lw["wout"],
                ws.astype(mxu), btab, lw["wpool"], lw["pscale"],
                tm=T, row_stride=DB, tiles_per_seq=1, halo_rows=halo.shape[0], halo_from_xc=False, cnt_tiles=1)
    dff = lw["w_ffa"].shape[1]
    ff_halo = jnp.transpose(state_ffn[l], (1, 0, 2)).reshape((CONV_W - 1) * DB, dff)
    tail = (CONV_W - 1) * DB
    x2, tails = _ffn(x1, lw["g2"], lw["w_ffa"], lw["w_ffu"], lw["cw"], lw["cb"], lw["w_ffd"], ff_halo,
                     tm=T, row_stride=DB, tiles_per_seq=1, use_carry=False, tail=tail)
    ff_state = jnp.transpose(tails.reshape(CONV_W - 1, DB, dff), (1, 0, 2))
    pool_state = jnp.concatenate([state_pool[l], bm(xc)], axis=1)[:, -POOL_STATE:, :]
    hd = lambda a: bm(a).reshape(DB, TS, N_HEADS, HEAD_DIM)
    return x2, (hd(k), hd(v), bm(ikw)[:, :, :IDX_DIM], bm(vv), pool_state, ff_state)


def kernel(x_prompt, x_sample, cache_k, cache_v, cache_idx_k, state_pool, state_ffn_conv, page_table, norm1_g, w_in, q_norm_g, k_norm_g, sgu_w, sgu_b, sgu_norm_g, pool_w, pool_scale, w_br_a, w_br_b, w_br_c, w_out, norm2_g, w_ff_in, ff_conv_w, ff_conv_b, w_ff_down):
    weights = (norm1_g, w_in, q_norm_g, k_norm_g, sgu_w, sgu_b, sgu_norm_g, pool_w, pool_scale,
               w_br_a, w_br_b, w_br_c, w_out, norm2_g, w_ff_in, ff_conv_w, ff_conv_b, w_ff_down)
    B, S, D = x_prompt.shape
    DB, TS, _ = x_sample.shape
    depth = w_in.shape[0]
    page = cache_idx_k.shape[2]
    past = page_table.shape[1] * page
    pool = cache_k.shape[1]
    cache_kt = jnp.transpose(cache_k, (0, 1, 3, 4, 2)).reshape(depth, pool, A_WIDTH, page)
    cache_vt = jnp.transpose(cache_v, (0, 1, 3, 4, 2)).reshape(depth, pool, A_WIDTH, page)
    cache_ikt = jnp.transpose(cache_idx_k, (0, 1, 3, 2))
    assert S % 512 == 0 and S >= POOL_STATE and TS * DB == SGU_CHUNK and TS >= CONV_W - 1 and TS <= page
    assert past % SGU_CHUNK == 0 and page_table.shape[1] % PAGES_PER_STEP == 0

    pos_p = np.arange(S)
    pos_s = past + np.repeat(np.arange(TS), DB)
    tabs_p, tabs_s = _rope_tables(pos_p), _rope_tables(pos_s)
    cnt_p, cnt_s = _pool_counts(pos_p), _pool_counts(pos_s)

    xp = x_prompt.reshape(B * S, D)
    xs = jnp.transpose(x_sample, (1, 0, 2)).reshape(TS * DB, D)
    outs_p, outs_s = [], []
    kv_state = None
    for l in range(depth):
        lw = _layer_weights(l, *weights)
        xp, kv_state, st_p = _prompt_layer(xp, lw, tabs_p, cnt_p, l, depth, kv_state, B=B, S=S)
        xs, st_s = _sample_layer(xs, lw, tabs_s, cnt_s, l, cache_kt, cache_vt, cache_ikt, state_pool,
                                 state_ffn_conv, page_table, DB=DB, TS=TS)
        outs_p.append(st_p)
        outs_s.append(st_s)
    stack = lambda outs, i: jnp.stack([o[i] for o in outs])
    y_p = xp.reshape(B, S, D)
    y_s = jnp.transpose(xs.reshape(TS, DB, D), (1, 0, 2))
    k_all, v_all, ik_all = kv_state
    heads = lambda a: jnp.transpose(a.reshape(depth, B, N_HEADS, HEAD_DIM, S), (0, 1, 4, 2, 3))
    return (y_p, y_s,
            heads(k_all), heads(v_all), jnp.transpose(ik_all, (0, 1, 3, 2)), stack(outs_p, 0), stack(outs_p, 1),
            stack(outs_s, 0), stack(outs_s, 1), stack(outs_s, 2), stack(outs_s, 3), stack(outs_s, 4), stack(outs_s, 5))
```
